```python
import jax
import jax.numpy as jnp
from jax import lax
import numpy as np

D_MODEL = 2048
BATCH = 2
SEQ = 4096
DEPTH = 2

MEM_LEN = 256
BRANCH_W = 512
N_BRANCHES = 5
NORM_EPS = 1e-6

RWKV_HEAD = 64
RWKV_HEADS = BRANCH_W // RWKV_HEAD
RWKV_DECAY_LORA = 96
RWKV_A_LORA = 96
RWKV_GATE_LORA = 256
RWKV_V_LORA = 64
RWKV_GN_EPS = 64e-5

NSA_HEAD = 64
NSA_Q_HEADS = BRANCH_W // NSA_HEAD
NSA_KV_HEADS = 2
NSA_GROUP = NSA_Q_HEADS // NSA_KV_HEADS
NSA_KV_W = NSA_KV_HEADS * NSA_HEAD
COMP_L = 32
COMP_STRIDE = 16
COMP_HIDDEN = 2 * NSA_HEAD
SEL_L = 64
SEL_N = 16
WINDOW = 512
Q_BLOCK = 128

CONV_W = 3

POOL_WINDOWS = (2, 4, 8, 16)
POOL_GROUP = BRANCH_W // len(POOL_WINDOWS)
POOL_OUT = D_MODEL // len(POOL_WINDOWS)

MEM_HEADS = 4
MEM_HEAD = BRANCH_W // MEM_HEADS

D_FF = 5632
N_EXPERTS = 8
TOP_K = 2
EXPERT_FF = 2816
N_DENSE = (DEPTH + 1) // 2
N_MOE = DEPTH // 2

RWKV_COLS = 3 * BRANCH_W + RWKV_DECAY_LORA + RWKV_A_LORA + RWKV_GATE_LORA
NSA_COLS = BRANCH_W + 6 * NSA_KV_W + 3 * NSA_Q_HEADS
CONV_COLS = 3 * BRANCH_W
POOL_COLS = BRANCH_W
MEM_COLS = BRANCH_W
GATE_COLS = N_BRANCHES * D_MODEL
IN_SIZES = (RWKV_COLS, NSA_COLS, CONV_COLS, POOL_COLS, MEM_COLS, GATE_COLS)
N_IN = RWKV_COLS + NSA_COLS + CONV_COLS + POOL_COLS + MEM_COLS + GATE_COLS

kernel_name = 'hybrid_rwkv7_nsa_conv_pool_memxattn_moe'

F32 = jnp.float32


def _split(a, sizes):
    return jnp.split(a, np.cumsum(sizes)[:-1].tolist(), axis=-1)


def _rms(x, g, eps=NORM_EPS):
    xf = x.astype(F32)
    y = xf * lax.rsqrt(jnp.mean(xf * xf, axis=-1, keepdims=True) + eps)
    return (y * g.astype(F32)).astype(x.dtype)


def _masked_softmax(s, mask, scale):
    s = jnp.where(mask, s.astype(F32) * scale, -1e30)
    return jnp.where(mask, jax.nn.softmax(s, axis=-1), 0.0)


def _token_shift(u, mu):
    prev = jnp.pad(u, ((0, 0), (1, 0), (0, 0)))[:, :-1]
    return u + (prev - u) * mu


def _rwkv7_time_mix(u, mu, w0, w2, a0, a2, g2, k_k, k_a, r_k, ln_w, ln_b, v_first, vres):
    B, T, _ = u.shape
    heads = lambda t: t.reshape(B, T, RWKV_HEADS, RWKV_HEAD)
    uf = _token_shift(u, mu).astype(F32)
    r, k, v, wd, ad, gd = _split(uf, (BRANCH_W, BRANCH_W, BRANCH_W, RWKV_DECAY_LORA, RWKV_A_LORA, RWKV_GATE_LORA))
    w_log = -jax.nn.softplus(-(w0 + jnp.tanh(wd) @ w2)) - 0.5
    a = jax.nn.sigmoid(a0 + ad @ a2)
    g = jax.nn.sigmoid(gd) @ g2
    if vres is not None:
        vd, v0, v_up = vres
        v = v + (v_first - v) * jax.nn.sigmoid(v0 + vd.astype(F32) @ v_up)
    kk = heads(k * k_k)
    kk = kk * lax.rsqrt(jnp.maximum(jnp.sum(kk * kk, axis=-1, keepdims=True), 1e-24))
    k = k * (1.0 + (a - 1.0) * k_a)
    rh, kh, vh, ah = heads(r), heads(k), heads(v), heads(a)
    decay = heads(jnp.exp(-jnp.exp(w_log)))

    def step(S, inp):
        r_t, w_t, k_t, v_t, kk_t, a_t = inp
        s_kk = jnp.einsum('bhvk,bhk->bhv', S, kk_t)
        S = S * w_t[:, :, None, :] - s_kk[..., None] * (kk_t * a_t)[:, :, None, :] + v_t[..., None] * k_t[:, :, None, :]
        return S, jnp.einsum('bhvk,bhk->bhv', S, r_t)

    xs = tuple(jnp.moveaxis(t, 1, 0) for t in (rh, decay, kh, vh, kk, ah))
    S0 = jnp.zeros((B, RWKV_HEADS, RWKV_HEAD, RWKV_HEAD), F32)
    _, y = lax.scan(step, S0, xs)
    y = jnp.moveaxis(y, 0, 1)
    y_mu = jnp.mean(y, axis=-1, keepdims=True)
    y_var = jnp.mean(jnp.square(y - y_mu), axis=-1, keepdims=True)
    y = ((y - y_mu) * lax.rsqrt(y_var + RWKV_GN_EPS)).reshape(B, T, BRANCH_W) * ln_w + ln_b
    bonus = jnp.sum(rh * kh * r_k.reshape(RWKV_HEADS, RWKV_HEAD), axis=-1, keepdims=True) * vh
    out = (y + bonus.reshape(B, T, BRANCH_W)) * g
    return out.astype(u.dtype), v


def _nsa(u, qk_gain, cmp_pos, cmp_w1, cmp_b1, cmp_w2):
    B, T, _ = u.shape
    scale = NSA_HEAD ** -0.5
    q, kc, vc, ks, vs, kw, vw, gl = _split(u, (BRANCH_W,) + (NSA_KV_W,) * 6 + (3 * NSA_Q_HEADS,))
    kv_heads = lambda t: t.reshape(B, T, NSA_KV_HEADS, NSA_HEAD)
    q = _rms(q.reshape(B, T, NSA_KV_HEADS, NSA_GROUP, NSA_HEAD), qk_gain[0])
    t_pos = jnp.arange(T)

    n_cmp = (T - COMP_L) // COMP_STRIDE + 1
    blk = np.arange(n_cmp)[:, None] * COMP_STRIDE + np.arange(COMP_L)[None, :]

    def compress(t, i):
        blocks = kv_heads(t)[:, blk] + cmp_pos[i][:, None, :]
        blocks = jnp.moveaxis(blocks, 3, 2).reshape(B, n_cmp, NSA_KV_HEADS, COMP_L * NSA_HEAD)
        return jax.nn.gelu(blocks @ cmp_w1[i] + cmp_b1[i]) @ cmp_w2[i]

    k_cmp = _rms(compress(kc, 0), qk_gain[1])
    v_cmp = compress(vc, 1)
    cmp_mask = (np.arange(n_cmp) * COMP_STRIDE + COMP_L - 1)[None, :] <= t_pos[:, None]
    p_cmp = _masked_softmax(jnp.einsum('bthgd,bchd->bhgtc', q, k_cmp), cmp_mask, scale)
    o_cmp = jnp.einsum('bhgtc,bchd->bthgd', p_cmp, v_cmp)

    n_sel = T // SEL_L
    k_top = min(SEL_N, n_sel)
    c0 = np.arange(n_cmp)[:, None] * COMP_STRIDE
    s0 = np.arange(n_sel)[None, :] * SEL_L
    overlap = np.clip(np.minimum(c0 + COMP_L, s0 + SEL_L) - np.maximum(c0, s0), 0, None) / COMP_L
    imp = jnp.einsum('bhgtc,cs->bhts', p_cmp, jnp.asarray(overlap, F32))
    cur = (t_pos // SEL_L)[:, None]
    sid = jnp.arange(n_sel)[None, :]
    forced = (sid == 0) | (sid == cur) | (sid == cur - 1)
    imp = jnp.where(forced, 1e9, jnp.where(sid <= cur, imp, -1e9))
    top_val, top_idx = lax.top_k(imp, k_top)
    top_ok = top_val > -1e8

    def sel_blocks(t):
        return jnp.moveaxis(t.reshape(B, n_sel, SEL_L, NSA_KV_HEADS, NSA_HEAD), 3, 1)

    k_sel_all = sel_blocks(_rms(kv_heads(ks), qk_gain[2]))
    v_sel_all = sel_blocks(kv_heads(vs))
    pad = ((0, 0), (WINDOW, 0), (0, 0), (0, 0))
    k_win_all = jnp.pad(_rms(kv_heads(kw), qk_gain[3]), pad)
    v_win_all = jnp.pad(kv_heads(vw), pad)
    bi = jnp.arange(B)[:, None, None, None]
    hi = jnp.arange(NSA_KV_HEADS)[None, :, None, None]
    m_len = k_top * SEL_L

    def query_block(args):
        qb, idx, ok, start = args
        tq = start + jnp.arange(Q_BLOCK)
        k_g = k_sel_all[bi, hi, idx]
        v_g = v_sel_all[bi, hi, idx].reshape(B, NSA_KV_HEADS, Q_BLOCK, m_len, NSA_HEAD)
        kpos = idx[..., None] * SEL_L + jnp.arange(SEL_L)
        m_sel = (ok[..., None] & (kpos <= tq[None, None, :, None, None])).reshape(B, NSA_KV_HEADS, 1, Q_BLOCK, m_len)
        s_sel = jnp.einsum('bqhgd,bhqnkd->bhgqnk', qb, k_g).reshape(B, NSA_KV_HEADS, NSA_GROUP, Q_BLOCK, m_len)
        p_sel = _masked_softmax(s_sel, m_sel, scale)
        o_s = jnp.einsum('bhgqm,bhqmd->bqhgd', p_sel, v_g)
        k_w = lax.dynamic_slice_in_dim(k_win_all, start, WINDOW + Q_BLOCK, axis=1)
        v_w = lax.dynamic_slice_in_dim(v_win_all, start, WINDOW + Q_BLOCK, axis=1)
        wpos = start - WINDOW + jnp.arange(WINDOW + Q_BLOCK)
        m_win = (wpos[None, :] <= tq[:, None]) & (wpos[None, :] > tq[:, None] - WINDOW) & (wpos[None, :] >= 0)
        p_win = _masked_softmax(jnp.einsum('bqhgd,bkhd->bhgqk', qb, k_w), m_win, scale)
        o_w = jnp.einsum('bhgqk,bkhd->bqhgd', p_win, v_w)
        return o_s, o_w

    n_qb = T // Q_BLOCK
    to_blocks = lambda t, ax: jnp.moveaxis(t.reshape(t.shape[:ax] + (n_qb, Q_BLOCK) + t.shape[ax + 1:]), ax, 0)
    xs = (to_blocks(q, 1), to_blocks(top_idx, 2), to_blocks(top_ok, 2), jnp.arange(n_qb) * Q_BLOCK)
    o_sel, o_win = lax.map(query_block, xs)
    from_blocks = lambda t: jnp.moveaxis(t, 0, 1).reshape(B, T, NSA_KV_HEADS, NSA_GROUP, NSA_HEAD)

    g = jax.nn.sigmoid(gl.astype(F32)).reshape(B, T, NSA_KV_HEADS, NSA_GROUP, 3)
    o = g[..., 0:1] * o_cmp + g[..., 1:2] * from_blocks(o_sel) + g[..., 2:3] * from_blocks(o_win)
    return o.reshape(B, T, BRANCH_W).astype(u.dtype)


def _short_conv(u, conv_w):
    b_gate, c_gate, x_in = _split(u, (BRANCH_W, BRANCH_W, BRANCH_W))
    z = c_gate * x_in
    y = lax.conv_general_dilated(z, conv_w[:, None, :].astype(z.dtype), window_strides=(1,),
                                 padding=[(CONV_W - 1, 0)], dimension_numbers=('NWC', 'WIO', 'NWC'),
                                 feature_group_count=BRANCH_W)
    return b_gate * y


def _pool(u, pool_w, pool_scale):
    B, T, _ = u.shape
    uf = u.astype(F32)
    cs = jnp.cumsum(uf, axis=1)
    count = jnp.arange(1, T + 1, dtype=F32)[None, :, None]
    outs = []
    for gi, w in enumerate(POOL_WINDOWS):
        sl = slice(gi * POOL_GROUP, (gi + 1) * POOL_GROUP)
        c = cs[..., sl]
        lag = jnp.pad(c, ((0, 0), (w, 0), (0, 0)))[:, :T]
        outs.append((c - lag) / jnp.minimum(count, w) - uf[..., sl])
    pooled = jnp.stack(outs, axis=2).astype(u.dtype)
    y = jnp.einsum('btgc,gcd->btgd', pooled, pool_w).reshape(B, T, D_MODEL)
    return y * pool_scale


def _mem_attn(u, mem_n, w_kv, qk_gain):
    B, T, _ = u.shape
    M = mem_n.shape[1]
    q = _rms(u.reshape(B, T, MEM_HEADS, MEM_HEAD), qk_gain[0])
    k, v = jnp.split(mem_n @ w_kv, 2, axis=-1)
    k = _rms(k.reshape(B, M, MEM_HEADS, MEM_HEAD), qk_gain[1])
    v = v.reshape(B, M, MEM_HEADS, MEM_HEAD)
    s = jnp.einsum('bthd,bmhd->bhtm', q, k).astype(F32) * (MEM_HEAD ** -0.5)
    p = jax.nn.softmax(s, axis=-1)
    o = jnp.einsum('bhtm,bmhd->bthd', p, v)
    return o.reshape(B, T, BRANCH_W).astype(u.dtype)


def _swiglu(h, w1, w3, w2):
    return (jax.nn.silu(h @ w1) * (h @ w3)) @ w2


def _moe(h, router, w1, w3, w2):
    B, T, D = h.shape
    hf = h.reshape(B * T, D)
    logits = (hf @ router).astype(F32)
    top_logit, top_e = lax.top_k(logits, TOP_K)
    top_w = jax.nn.softmax(top_logit, axis=-1)
    combine = jnp.sum(jax.nn.one_hot(top_e, N_EXPERTS, dtype=F32) * top_w[..., None], axis=1)
    out = jnp.zeros((B * T, D), F32)
    for e in range(N_EXPERTS):
        out = out + combine[:, e:e + 1] * _swiglu(hf, w1[e], w3[e], w2[e])
    return out.reshape(B, T, D).astype(h.dtype)


def setup_inputs(seed: int = 0) -> dict:
    key = jax.random.key(seed)
    ks = iter(jax.random.split(key, 48))
    nrm = lambda shape, fan_in: jax.random.normal(next(ks), shape, F32) * (fan_in ** -0.5)
    gain = lambda shape: 1.0 + 0.02 * jax.random.normal(next(ks), shape, F32)
    small = lambda shape, s: s * jax.random.normal(next(ks), shape, F32)
    unif = lambda shape, lo, hi: jax.random.uniform(next(ks), shape, F32, lo, hi)
    L, V = DEPTH, DEPTH - 1
    return {
        'x': jax.random.normal(next(ks), (BATCH, SEQ, D_MODEL), F32),
        'mem': jax.random.normal(next(ks), (BATCH, MEM_LEN, D_MODEL), F32),
        'norm_mix': gain((L, D_MODEL)),
        'norm_ffn': gain((L, D_MODEL)),
        'norm_mem': gain((L, D_MODEL)),
        'w_in': nrm((L, D_MODEL, N_IN), D_MODEL),
        'rwkv_mu': unif((L, RWKV_COLS), 0.0, 1.0),
        'rwkv_w0': unif((L, BRANCH_W), -6.0, -1.0),
        'rwkv_w2': nrm((L, RWKV_DECAY_LORA, BRANCH_W), RWKV_DECAY_LORA),
        'rwkv_a0': small((L, BRANCH_W), 0.1),
        'rwkv_a2': nrm((L, RWKV_A_LORA, BRANCH_W), RWKV_A_LORA),
        'rwkv_g2': nrm((L, RWKV_GATE_LORA, BRANCH_W), RWKV_GATE_LORA),
        'rwkv_kk': 1.0 + small((L, BRANCH_W), 0.1),
        'rwkv_ka': 1.0 + small((L, BRANCH_W), 0.1),
        'rwkv_rk': small((L, BRANCH_W), 0.1),
        'rwkv_ln_w': gain((L, BRANCH_W)),
        'rwkv_ln_b': small((L, BRANCH_W), 0.02),
        'vres_in': nrm((V, D_MODEL, RWKV_V_LORA), D_MODEL),
        'vres_mu': unif((V, RWKV_V_LORA), 0.0, 1.0),
        'vres_v0': small((V, BRANCH_W), 0.1),
        'vres_up': nrm((V, RWKV_V_LORA, BRANCH_W), RWKV_V_LORA),
        'nsa_qk_gain': gain((L, 4, NSA_HEAD)),
        'nsa_cmp_pos': small((L, 2, COMP_L, NSA_HEAD), 0.02),
        'nsa_cmp_w1': nrm((L, 2, COMP_L * NSA_HEAD, COMP_HIDDEN), COMP_L * NSA_HEAD),
        'nsa_cmp_b1': small((L, 2, COMP_HIDDEN), 0.02),
        'nsa_cmp_w2': nrm((L, 2, COMP_HIDDEN, NSA_HEAD), COMP_HIDDEN),
        'conv_w': nrm((L, CONV_W, BRANCH_W), CONV_W),
        'pool_w': nrm((L, len(POOL_WINDOWS), POOL_GROUP, POOL_OUT), POOL_GROUP),
        'pool_scale': gain((L, D_MODEL)),
        'mem_wkv': nrm((L, D_MODEL, 2 * BRANCH_W), D_MODEL),
        'mem_qk_gain': gain((L, 2, MEM_HEAD)),
        'w_branch': nrm((L, 4, BRANCH_W, D_MODEL), BRANCH_W),
        'w_out': nrm((L, D_MODEL, D_MODEL), D_MODEL),
        'ffn_w1': nrm((N_DENSE, D_MODEL, D_FF), D_MODEL),
        'ffn_w3': nrm((N_DENSE, D_MODEL, D_FF), D_MODEL),
        'ffn_w2': nrm((N_DENSE, D_FF, D_MODEL), D_FF),
        'moe_router': nrm((N_MOE, D_MODEL, N_EXPERTS), D_MODEL),
        'moe_w1': nrm((N_MOE, N_EXPERTS, D_MODEL, EXPERT_FF), D_MODEL),
        'moe_w3': nrm((N_MOE, N_EXPERTS, D_MODEL, EXPERT_FF), D_MODEL),
        'moe_w2': nrm((N_MOE, N_EXPERTS, EXPERT_FF, D_MODEL), EXPERT_FF),
    }


def reference(x, mem, norm_mix, norm_ffn, norm_mem, w_in, rwkv_mu, rwkv_w0, rwkv_w2, rwkv_a0,
              rwkv_a2, rwkv_g2, rwkv_kk, rwkv_ka, rwkv_rk, rwkv_ln_w, rwkv_ln_b, vres_in, vres_mu,
              vres_v0, vres_up, nsa_qk_gain, nsa_cmp_pos, nsa_cmp_w1, nsa_cmp_b1, nsa_cmp_w2, conv_w,
              pool_w, pool_scale, mem_wkv, mem_qk_gain, w_branch, w_out, ffn_w1, ffn_w3, ffn_w2,
              moe_router, moe_w1, moe_w3, moe_w2):
    B, T, _ = x.shape
    v_first = None
    for l in range(DEPTH):
        h = _rms(x, norm_mix[l])
        if l == 0:
            p = h @ w_in[l]
            parts = _split(p, IN_SIZES)
            vres = None
        else:
            p = h @ jnp.concatenate([w_in[l], vres_in[l - 1]], axis=1)
            parts = _split(p, IN_SIZES + (RWKV_V_LORA,))
            vres = (_token_shift(parts[6], vres_mu[l - 1]), vres_v0[l - 1], vres_up[l - 1])
        p_rwkv, p_nsa, p_conv, p_pool, p_mem, p_gate = parts[:6]

        y_rwkv, v_l = _rwkv7_time_mix(p_rwkv, rwkv_mu[l], rwkv_w0[l], rwkv_w2[l], rwkv_a0[l], rwkv_a2[l],
                                      rwkv_g2[l], rwkv_kk[l], rwkv_ka[l], rwkv_rk[l], rwkv_ln_w[l],
                                      rwkv_ln_b[l], v_first, vres)
        if l == 0:
            v_first = v_l
        y_nsa = _nsa(p_nsa, nsa_qk_gain[l], nsa_cmp_pos[l], nsa_cmp_w1[l], nsa_cmp_b1[l], nsa_cmp_w2[l])
        y_conv = _short_conv(p_conv, conv_w[l])
        y_mem = _mem_attn(p_mem, _rms(mem, norm_mem[l]), mem_wkv[l], mem_qk_gain[l])
        z_pool = _pool(p_pool, pool_w[l], pool_scale[l])

        gates = jax.nn.sigmoid(p_gate.astype(F32)).reshape(B, T, N_BRANCHES, D_MODEL)
        merged = gates[:, :, 4] * z_pool
        ys = (y_rwkv, y_nsa, y_conv, y_mem)
        for i in range(4):
            merged = merged + gates[:, :, i] * (ys[i] @ w_branch[l, i])
        x = x + (merged.astype(x.dtype) @ w_out[l])

        h2 = _rms(x, norm_ffn[l])
        if l % 2 == 0:
            x = x + _swiglu(h2, ffn_w1[l // 2], ffn_w3[l // 2], ffn_w2[l // 2])
        else:
            x = x + _moe(h2, moe_router[l // 2], moe_w1[l // 2], moe_w3[l // 2], moe_w2[l // 2])
    return x
```

```python
import functools

import jax
import jax.numpy as jnp
import numpy as np
from jax import lax
from jax.experimental import pallas as pl
from jax.experimental.pallas import tpu as pltpu

F32 = jnp.float32
BF16 = jnp.bfloat16

D_MODEL = 2048
BRANCH_W = 512
HEAD = 64
RWKV_HEADS = 8
NSA_KV_HEADS = 2
NSA_GROUP = 4
COMP_L = 32
COMP_STRIDE = 16
SEL_L = 64
SEL_N = 16
WINDOW = 512
MEM_HEADS = 4
MEM_HEAD = 128
POOL_WINDOWS = (2, 4, 8, 16)
N_EXPERTS = 8
NORM_EPS = 1e-6
RWKV_GN_EPS = 64e-5
RWKV_CHUNK = 64

VMEM_LIMIT_BYTES = 56 * 1024 * 1024

C_RWKV, C_NSAQ, C_CONV, C_POOL, C_MEM = 0, 2048, 2560, 4096, 4608
C_KC, C_VC, C_KS, C_VS, C_KW, C_VW, C_GL, C_VRES = 5120, 5248, 5376, 5504, 5632, 5760, 5888, 6016
N_SMALL = 6144


def _params(n_axes):
    return pltpu.CompilerParams(dimension_semantics=("arbitrary",) * n_axes,
                                vmem_limit_bytes=VMEM_LIMIT_BYTES)


def _split(a, n):
    pieces, r = [], a
    for i in range(n):
        p = r.astype(BF16)
        pieces.append(p)
        if i + 1 < n:
            r = r - p.astype(F32)
    return pieces


def _dg(a, b, nt):
    dims = (((1,), (1,)), ((), ())) if nt else (((1,), (0,)), ((), ()))
    return lax.dot_general(a, b, dims, preferred_element_type=F32)


def _mm(a, b, pa=1, pb=1, nt=False):
    sa = _split(a, pa) if a.dtype != BF16 else [a]
    sb = _split(b, pb) if b.dtype != BF16 else [b]
    order = max(len(sa), len(sb))
    acc = None
    for i, x in enumerate(sa):
        for j, y in enumerate(sb):
            if i + j < order:
                t = _dg(x, y, nt)
                acc = t if acc is None else acc + t
    return acc


def _head_sum_matrix(width, head):
    r = lax.broadcasted_iota(jnp.int32, (width, width), 0) // head
    c = lax.broadcasted_iota(jnp.int32, (width, width), 1) // head
    return jnp.where(r == c, 1.0, 0.0).astype(BF16)


def _group_mean(x, head):
    hs = _head_sum_matrix(x.shape[-1], head)
    return _mm(x, hs, pa=3) * (1.0 / head)


def _rmsnorm_kernel(x_ref, g_ref, o_ref):
    x = x_ref[...]
    y = x * lax.rsqrt(jnp.mean(x * x, axis=-1, keepdims=True) + NORM_EPS)
    o_ref[...] = (y * g_ref[...]).astype(o_ref.dtype)


def _rmsnorm(x, g, tm=512):
    m, d = x.shape
    tm = min(tm, m)
    return pl.pallas_call(
        _rmsnorm_kernel,
        grid=(m // tm,),
        in_specs=[pl.BlockSpec((tm, d), lambda i: (i, 0)), pl.BlockSpec((1, d), lambda i: (0, 0))],
        out_specs=pl.BlockSpec((tm, d), lambda i: (i, 0)),
        out_shape=jax.ShapeDtypeStruct((m, d), BF16),
        compiler_params=_params(1),
        name="rmsnorm",
    )(x, g.reshape(1, d))


def _rmsnorm_router_kernel(x_ref, g_ref, wr_ref, o_ref, comb_ref):
    x = x_ref[...]
    y = x * lax.rsqrt(jnp.mean(x * x, axis=-1, keepdims=True) + NORM_EPS) * g_ref[...]
    o_ref[...] = y.astype(o_ref.dtype)
    logits = _mm(y, wr_ref[...], pa=3, pb=3)
    lane = lax.broadcasted_iota(jnp.int32, logits.shape, 1)
    neg = jnp.float32(-3e38)
    lg = jnp.where(lane < N_EXPERTS, logits, neg)
    m1 = jnp.max(lg, axis=-1, keepdims=True)
    i1 = jnp.min(jnp.where(lg == m1, lane, 1 << 20), axis=-1, keepdims=True)
    lg2 = jnp.where(lane == i1, neg, lg)
    m2 = jnp.max(lg2, axis=-1, keepdims=True)
    i2 = jnp.min(jnp.where(lg2 == m2, lane, 1 << 20), axis=-1, keepdims=True)
    e2 = jnp.exp(m2 - m1)
    w1 = 1.0 / (1.0 + e2)
    w2 = e2 / (1.0 + e2)
    comb_ref[...] = jnp.where(lane == i1, w1, 0.0) + jnp.where(lane == i2, w2, 0.0)


def _rmsnorm_router(x, g, router, tm=512):
    m, d = x.shape
    tm = min(tm, m)
    wr = jnp.pad(router, ((0, 0), (0, 128 - router.shape[1])))
    return pl.pallas_call(
        _rmsnorm_router_kernel,
        grid=(m // tm,),
        in_specs=[pl.BlockSpec((tm, d), lambda i: (i, 0)), pl.BlockSpec((1, d), lambda i: (0, 0)),
                  pl.BlockSpec((d, 128), lambda i: (0, 0))],
        out_specs=[pl.BlockSpec((tm, d), lambda i: (i, 0)), pl.BlockSpec((tm, 128), lambda i: (i, 0))],
        out_shape=[jax.ShapeDtypeStruct((m, d), BF16), jax.ShapeDtypeStruct((m, 128), F32)],
        compiler_params=_params(1),
        name="rmsnorm_router",
    )(x, g.reshape(1, d), wr)


def _matmul_kernel(*refs, nk, has_res, scale_col):
    a_ref, w_ref = refs[0], refs[1]
    pos = 2
    res_ref = comb_ref = None
    if has_res:
        res_ref = refs[pos]; pos += 1
    if scale_col is not None:
        comb_ref = refs[pos]; pos += 1
    o_ref, acc_ref = refs[pos], refs[pos + 1]
    k = pl.program_id(2)
    part = jnp.dot(a_ref[...], w_ref[...].astype(BF16), preferred_element_type=F32)

    @pl.when(k == 0)
    def _():
        acc_ref[...] = part

    @pl.when(k > 0)
    def _():
        acc_ref[...] += part

    @pl.when(k == nk - 1)
    def _():
        out = acc_ref[...]
        if scale_col is not None:
            comb = comb_ref[...]
            lane = lax.broadcasted_iota(jnp.int32, comb.shape, 1)
            out = out * jnp.sum(jnp.where(lane == scale_col, comb, 0.0), axis=-1, keepdims=True)
        if has_res:
            out = out + res_ref[...]
        o_ref[...] = out.astype(o_ref.dtype)


def _matmul(a, w, *, tm, tn, tk=None, out_dtype=F32, res=None, comb=None, scale_col=None, name="matmul"):
    m, kdim = a.shape
    n = w.shape[1]
    tk = kdim if tk is None else tk
    tm, tn = min(tm, m), min(tn, n)
    nk = kdim // tk
    assert m % tm == 0 and n % tn == 0 and kdim % tk == 0
    in_specs = [pl.BlockSpec((tm, tk), lambda j, i, k: (i, k)), pl.BlockSpec((tk, tn), lambda j, i, k: (k, j))]
    args = [a, w]
    if res is not None:
        in_specs.append(pl.BlockSpec((tm, tn), lambda j, i, k: (i, j)))
        args.append(res)
    if scale_col is not None:
        in_specs.append(pl.BlockSpec((tm, 128), lambda j, i, k: (i, 0)))
        args.append(comb)
    return pl.pallas_call(
        functools.partial(_matmul_kernel, nk=nk, has_res=res is not None, scale_col=scale_col),
        grid=(n // tn, m // tm, nk),
        in_specs=in_specs,
        out_specs=pl.BlockSpec((tm, tn), lambda j, i, k: (i, j)),
        out_shape=jax.ShapeDtypeStruct((m, n), out_dtype),
        scratch_shapes=[pltpu.VMEM((tm, tn), F32)],
        compiler_params=_params(3),
        name=name,
    )(*args)


def _swiglu_up_kernel(h_ref, w1_ref, w3_ref, o_ref):
    h = h_ref[...]
    a = jnp.dot(h, w1_ref[...].astype(BF16), preferred_element_type=F32)
    b = jnp.dot(h, w3_ref[...].astype(BF16), preferred_element_type=F32)
    o_ref[...] = (a * jax.nn.sigmoid(a) * b).astype(o_ref.dtype)


def _swiglu_up(h, w1, w3, tm=1024, tn=512):
    m, kdim = h.shape
    n = w1.shape[1]
    tm, tn = min(tm, m), min(tn, n)
    assert m % tm == 0 and n % tn == 0
    return pl.pallas_call(
        _swiglu_up_kernel,
        grid=(n // tn, m // tm),
        in_specs=[pl.BlockSpec((tm, kdim), lambda j, i: (i, 0)),
                  pl.BlockSpec((kdim, tn), lambda j, i: (0, j)),
                  pl.BlockSpec((kdim, tn), lambda j, i: (0, j))],
        out_specs=pl.BlockSpec((tm, tn), lambda j, i: (i, j)),
        out_shape=jax.ShapeDtypeStruct((m, n), BF16),
        compiler_params=_params(2),
        name="swiglu_up",
    )(h, w1, w3)


def _shift_rows(x, carry_row):
    rolled = pltpu.roll(x, 1, axis=0)
    row = lax.broadcasted_iota(jnp.int32, x.shape, 0)
    return jnp.where(row == 0, carry_row, rolled)


def _rwkv_prep_kernel(*refs, has_vres):
    (u_ref, mu_ref, w0_ref, w2_ref, a0_ref, a2_ref, g2_ref, kkw_ref, kaw_ref) = refs[:9]
    pos = 9
    if has_vres:
        vd_ref, vmu_ref, v0_ref, vup_ref, vfirst_ref = refs[pos:pos + 5]
        pos += 5
    r_ref, lw_ref, k_ref, v_ref, kk_ref, b_ref, g_ref = refs[pos:pos + 7]
    pos += 7
    cu_ref = refs[pos]
    cv_ref = refs[pos + 1] if has_vres else None
    tt = u_ref.shape[0]

    @pl.when(pl.program_id(1) == 0)
    def _():
        cu_ref[...] = jnp.zeros_like(cu_ref)
        if has_vres:
            cv_ref[...] = jnp.zeros_like(cv_ref)

    u = u_ref[...]
    prev = _shift_rows(u, cu_ref[0:1, :])
    cu_ref[0:1, :] = u[tt - 1:tt, :]
    uf = u + (prev - u) * mu_ref[...]
    r, k, v = uf[:, 0:512], uf[:, 512:1024], uf[:, 1024:1536]
    wd, ad, gd = uf[:, 1536:1664], uf[:, 1664:1792], uf[:, 1792:2048]

    x = w0_ref[...] + _mm(jnp.tanh(wd), w2_ref[...], 2, 2)
    softplus = jnp.maximum(-x, 0.0) + jnp.log(1.0 + jnp.exp(-jnp.abs(x)))
    lw_ref[...] = -jnp.exp(-softplus - 0.5)
    a = jax.nn.sigmoid(a0_ref[...] + _mm(ad, a2_ref[...], 2, 2))
    g_ref[...] = _mm(jax.nn.sigmoid(gd), g2_ref[...], 2, 2)
    if has_vres:
        vd = vd_ref[...]
        vprev = _shift_rows(vd, cv_ref[0:1, :])
        cv_ref[0:1, :] = vd[tt - 1:tt, :]
        vdf = vd + (vprev - vd) * vmu_ref[...]
        v = v + (vfirst_ref[...] - v) * jax.nn.sigmoid(v0_ref[...] + _mm(vdf, vup_ref[...], 2, 2))
    kk = k * kkw_ref[...]
    ss = _group_mean(kk * kk, HEAD) * HEAD
    kk = kk * lax.rsqrt(jnp.maximum(ss, 1e-24))
    r_ref[...] = r
    k_ref[...] = k * (1.0 + (a - 1.0) * kaw_ref[...])
    v_ref[...] = v
    kk_ref[...] = kk
    b_ref[...] = kk * a


def _rwkv_prep(p_small, bsz, seq, mu, w0, w2, a0, a2, g2, kkw, kaw, vres, tt=256):
    tt = min(tt, seq)
    nt = seq // tt
    n = bsz * seq
    row = lambda b, t: (b * nt + t, 0)
    const = lambda b, t: (0, 0)
    pad_rows = lambda w: jnp.pad(w, ((0, 128 - w.shape[0]), (0, 0)))
    mu_p = jnp.concatenate([mu[:1536], jnp.pad(mu[1536:1632], (0, 32)), jnp.pad(mu[1632:1728], (0, 32)), mu[1728:]])
    vec = lambda a: a.reshape(1, -1)
    args = [p_small, vec(mu_p), vec(w0), pad_rows(w2), vec(a0), pad_rows(a2), g2, vec(kkw), vec(kaw)]
    in_specs = [pl.BlockSpec((tt, 2048), row), pl.BlockSpec((1, 2048), const), pl.BlockSpec((1, 512), const),
                pl.BlockSpec((128, 512), const), pl.BlockSpec((1, 512), const), pl.BlockSpec((128, 512), const),
                pl.BlockSpec((256, 512), const), pl.BlockSpec((1, 512), const), pl.BlockSpec((1, 512), const)]
    scratch = [pltpu.VMEM((8, 2048), F32)]
    if vres is not None:
        vmu, v0, vup, vfirst = vres
        args += [p_small, vec(jnp.pad(vmu, (0, 64))), vec(v0), pad_rows(vup), vfirst]
        in_specs += [pl.BlockSpec((tt, 128), lambda b, t: (b * nt + t, C_VRES // 128)), pl.BlockSpec((1, 128), const),
                     pl.BlockSpec((1, 512), const), pl.BlockSpec((128, 512), const), pl.BlockSpec((tt, 512), row)]
        scratch.append(pltpu.VMEM((8, 128), F32))
    return pl.pallas_call(
        functools.partial(_rwkv_prep_kernel, has_vres=vres is not None),
        grid=(bsz, nt),
        in_specs=in_specs,
        out_specs=[pl.BlockSpec((tt, 512), row)] * 7,
        out_shape=[jax.ShapeDtypeStruct((n, 512), F32)] * 7,
        scratch_shapes=scratch,
        compiler_params=_params(2),
        name="rwkv_prep",
    )(*args)


def _rwkv_chunk_kernel(r_ref, lw_ref, k_ref, v_ref, kk_ref, b_ref, kt_ref, bt_ref, lwt_ref, y_ref, s_ref, *, hb, pw):
    c = r_ref.shape[2]

    @pl.when(pl.program_id(1) == 0)
    def _():
        s_ref[...] = jnp.zeros_like(s_ref)

    row = lax.broadcasted_iota(jnp.int32, (c, c), 0)
    col = lax.broadcasted_iota(jnp.int32, (c, c), 1)
    tril_incl = jnp.where(row >= col, 1.0, 0.0).astype(BF16)
    triu_incl = jnp.where(row <= col, 1.0, 0.0).astype(BF16)
    strict = row > col
    incl = row >= col
    eye = jnp.where(row == col, 1.0, 0.0)
    mm = functools.partial(_mm, pa=pw, pb=pw)

    for h in range(hb):
        r, lw, k, v, kk, b = (ref[h, 0] for ref in (r_ref, lw_ref, k_ref, v_ref, kk_ref, b_ref))
        kt, bt, lwt = kt_ref[h, 0], bt_ref[h, 0], lwt_ref[h, 0]
        cum = _mm(tril_incl, lw, pb=3)
        cum_t = _mm(lwt, triu_incl, pa=3)
        g_in = jnp.exp(cum)
        g_inv = jnp.exp(-cum)
        a_t = kk * jnp.exp(cum - lw)
        b_t = b * g_inv
        k_t = k * g_inv
        r_t = r * g_in
        a_ab = jnp.where(strict, mm(a_t, b_t, nt=True), 0.0)
        a_ak = jnp.where(strict, mm(a_t, k_t, nt=True), 0.0)
        a_rb = jnp.where(incl, mm(r_t, b_t, nt=True), 0.0)
        a_rk = jnp.where(incl, mm(r_t, k_t, nt=True), 0.0)
        x = eye - a_ab
        p = mm(a_ab, a_ab)
        steps = int(np.log2(c)) - 1
        for i in range(steps):
            x = x + mm(x, p)
            if i + 1 < steps:
                p = mm(p, p)
        w1 = mm(x, a_t)
        w2 = mm(x, mm(a_ak, v))
        y_in = mm(a_rk, v) - mm(a_rb, w2)
        q_h = r_t - mm(a_rb, w1)
        tail_t = jnp.exp(cum_t[:, c - 1:c] - cum_t)
        bh_t = bt * tail_t
        kh_t = kt * tail_t
        s = s_ref[h]
        m_s = jnp.exp(cum_t[:, c - 1:c]) * s - mm(bh_t, mm(w1, s))
        g_s = mm(kh_t, v) - mm(bh_t, w2)
        y_ref[h, 0] = y_in + mm(q_h, s)
        s_ref[h] = m_s + g_s


def _rwkv_recurrence(r, lw, k, v, kk, b, bsz, seq, hb=4, pw=2):
    c = min(RWKV_CHUNK, seq)
    nc = seq // c
    bh = bsz * RWKV_HEADS
    hm = lambda a: a.reshape(bsz, nc, c, RWKV_HEADS, HEAD).transpose(0, 3, 1, 2, 4).reshape(bh, nc, c, HEAD)
    tr = lambda a: hm(a).transpose(0, 1, 3, 2)
    spec = pl.BlockSpec((hb, 1, c, HEAD), lambda h, t: (h, t, 0, 0))
    spec_t = pl.BlockSpec((hb, 1, HEAD, c), lambda h, t: (h, t, 0, 0))
    y = pl.pallas_call(
        functools.partial(_rwkv_chunk_kernel, hb=hb, pw=pw),
        grid=(bh // hb, nc),
        in_specs=[spec] * 6 + [spec_t] * 3,
        out_specs=spec,
        out_shape=jax.ShapeDtypeStruct((bh, nc, c, HEAD), F32),
        scratch_shapes=[pltpu.VMEM((hb, HEAD, HEAD), F32)],
        compiler_params=_params(2),
        name="rwkv_chunk",
    )(hm(r), hm(lw), hm(k), hm(v), hm(kk), hm(b), tr(k), tr(b), tr(lw))
    return y.reshape(bsz, RWKV_HEADS, nc, c, HEAD).transpose(0, 2, 3, 1, 4).reshape(bsz * seq, BRANCH_W)


def _rwkv_post_kernel(y_ref, r_ref, k_ref, v_ref, g_ref, rk_ref, lnw_ref, lnb_ref, o_ref):
    y = y_ref[...]
    mean = _group_mean(y, HEAD)
    d = y - mean
    var = _group_mean(d * d, HEAD)
    yn = d * lax.rsqrt(var + RWKV_GN_EPS) * lnw_ref[...] + lnb_ref[...]
    bonus = _group_mean(r_ref[...] * k_ref[...] * rk_ref[...], HEAD) * HEAD * v_ref[...]
    o_ref[...] = ((yn + bonus) * g_ref[...]).astype(o_ref.dtype)


def _rwkv_post(y, r, k, v, g, rk, lnw, lnb, tt=512):
    n = y.shape[0]
    tt = min(tt, n)
    row = pl.BlockSpec((tt, 512), lambda i: (i, 0))
    cst = pl.BlockSpec((1, 512), lambda i: (0, 0))
    return pl.pallas_call(
        _rwkv_post_kernel,
        grid=(n // tt,),
        in_specs=[row] * 5 + [cst] * 3,
        out_specs=row,
        out_shape=jax.ShapeDtypeStruct((n, 512), BF16),
        compiler_params=_params(1),
        name="rwkv_post",
    )(y, r, k, v, g, rk.reshape(1, -1), lnw.reshape(1, -1), lnb.reshape(1, -1))


def _nsa_norm_kernel(q_ref, ks_ref, kw_ref, gq_ref, gs_ref, gw_ref, qo_ref, kso_ref, kwo_ref):
    for x_ref, g_ref, o_ref in ((q_ref, gq_ref, qo_ref), (ks_ref, gs_ref, kso_ref), (kw_ref, gw_ref, kwo_ref)):
        x = x_ref[...]
        ms = _group_mean(x * x, HEAD)
        o_ref[...] = (x * lax.rsqrt(ms + NORM_EPS) * g_ref[...]).astype(o_ref.dtype)


def _nsa_norm(p_small, qk_gain, tt=512):
    n = p_small.shape[0]
    tt = min(tt, n)
    col = lambda c, w: pl.BlockSpec((tt, w), lambda i: (i, c // w))
    cst = lambda w: pl.BlockSpec((1, w), lambda i: (0, 0))
    out = lambda w: pl.BlockSpec((tt, w), lambda i: (i, 0))
    gq = jnp.tile(qk_gain[0], 8).reshape(1, 512)
    gs = jnp.tile(qk_gain[2], 2).reshape(1, 128)
    gw = jnp.tile(qk_gain[3], 2).reshape(1, 128)
    return pl.pallas_call(
        _nsa_norm_kernel,
        grid=(n // tt,),
        in_specs=[col(C_NSAQ, 512), col(C_KS, 128), col(C_KW, 128), cst(512), cst(128), cst(128)],
        out_specs=[out(512), out(128), out(128)],
        out_shape=[jax.ShapeDtypeStruct((n, 512), BF16), jax.ShapeDtypeStruct((n, 128), BF16),
                   jax.ShapeDtypeStruct((n, 128), BF16)],
        compiler_params=_params(1),
        name="nsa_norm",
    )(p_small, p_small, p_small, gq, gs, gw)


def _gelu_tanh(x):
    return 0.5 * x * (1.0 + jnp.tanh(0.7978845608028654 * (x + 0.044715 * x * x * x)))


def _nsa_compress_kernel(x_ref, pos_ref, w1_ref, b1_ref, w2_ref, gain_ref, o_ref):
    half = x_ref.shape[-1]
    x = x_ref[0, 0, 0].astype(BF16)
    w1 = w1_ref[0].astype(BF16)
    z_lo = jnp.dot(x, w1[:half], preferred_element_type=F32)
    z_hi = jnp.dot(x, w1[half:], preferred_element_type=F32)
    nrow = z_hi.shape[0]
    pos = jnp.broadcast_to(pos_ref[0], (8, 2 * half))
    const = _mm(pos, w1, pa=2)[0:1] + b1_ref[0]
    pre = z_lo + pltpu.roll(z_hi, nrow - 1, axis=0) + const
    out = jnp.dot(_gelu_tanh(pre).astype(BF16), w2_ref[0].astype(BF16), preferred_element_type=F32)
    is_key = pl.program_id(0) == 0
    normed = out * lax.rsqrt(jnp.mean(out * out, axis=-1, keepdims=True) + NORM_EPS) * gain_ref[...]
    o_ref[0, 0, 0] = jnp.where(is_key, normed, out)


def _nsa_compress(kc_vc, cmp_pos, cmp_w1, cmp_b1, cmp_w2, gain):
    _, bsz, hkv, nc, wid = kc_vc.shape
    return pl.pallas_call(
        _nsa_compress_kernel,
        grid=(2, bsz, hkv),
        in_specs=[pl.BlockSpec((1, 1, 1, nc, wid), lambda i, b, h: (i, b, h, 0, 0)),
                  pl.BlockSpec((1, 1, 2 * wid), lambda i, b, h: (i, 0, 0)),
                  pl.BlockSpec((1, 2 * wid, 2 * HEAD), lambda i, b, h: (i, 0, 0)),
                  pl.BlockSpec((1, 1, 2 * HEAD), lambda i, b, h: (i, 0, 0)),
                  pl.BlockSpec((1, 2 * HEAD, HEAD), lambda i, b, h: (i, 0, 0)),
                  pl.BlockSpec((1, HEAD), lambda i, b, h: (0, 0))],
        out_specs=pl.BlockSpec((1, 1, 1, nc, HEAD), lambda i, b, h: (i, b, h, 0, 0)),
        out_shape=jax.ShapeDtypeStruct((2, bsz, hkv, nc, HEAD), F32),
        compiler_params=_params(3),
        name="nsa_compress",
    )(kc_vc, cmp_pos.reshape(2, 1, 2 * wid), cmp_w1, cmp_b1.reshape(2, 1, 2 * HEAD), cmp_w2, gain.reshape(1, HEAD))


def _nsa_cmp_kernel(q_ref, kc_ref, vc_ref, ov_ref, o_ref, sel_ref, *, tq, scale):
    g = q_ref.shape[2]
    ncmp = kc_ref.shape[2]
    nsel = ov_ref.shape[1]
    start = pl.program_id(2) * tq
    q = q_ref[0, 0].reshape(g * tq, HEAD)
    s = _dg(q, kc_ref[0, 0].astype(BF16), nt=True) * scale
    t_pos = start + (lax.broadcasted_iota(jnp.int32, s.shape, 0) & (tq - 1))
    c_end = lax.broadcasted_iota(jnp.int32, s.shape, 1) * COMP_STRIDE + (COMP_L - 1)
    mask = c_end <= t_pos
    s = jnp.where(mask, s, -1e30)
    e = jnp.exp(s - jnp.max(s, axis=-1, keepdims=True))
    p = jnp.where(mask, e / jnp.sum(e, axis=-1, keepdims=True), 0.0)
    o = jnp.dot(p.astype(BF16), vc_ref[0, 0].astype(BF16), preferred_element_type=F32)
    o_ref[0, 0] = o.reshape(g, tq, HEAD).astype(o_ref.dtype)
    p_sum = p[0:tq]
    for i in range(1, g):
        p_sum = p_sum + p[i * tq:(i + 1) * tq]
    imp = _mm(p_sum, ov_ref[...], pa=3)
    sid = lax.broadcasted_iota(jnp.int32, imp.shape, 1)
    cur = (start + lax.broadcasted_iota(jnp.int32, imp.shape, 0)) // SEL_L
    forced = (sid == 0) | (sid == cur) | (sid == cur - 1)
    val = jnp.where(forced, 1e9, jnp.where(sid <= cur, imp, -1e9))
    rank = jnp.zeros(imp.shape, jnp.int32)
    for j in range(nsel):
        cj = val[:, j:j + 1]
        ahead = (cj > val) | ((cj == val) & (sid > j))
        rank = rank + ahead.astype(jnp.int32)
    k_top = min(SEL_N, nsel)
    sel_ref[0, 0] = jnp.where((rank < k_top) & (val > -1e8), 1.0, 0.0).astype(sel_ref.dtype)


def _nsa_cmp(q_hm, k_cmp, v_cmp, overlap, tq=128):
    bsz, hkv, g, seq, _ = q_hm.shape
    ncmp, nsel = overlap.shape
    return pl.pallas_call(
        functools.partial(_nsa_cmp_kernel, tq=tq, scale=HEAD ** -0.5),
        grid=(bsz, hkv, seq // tq),
        in_specs=[pl.BlockSpec((1, 1, g, tq, HEAD), lambda b, h, t: (b, h, 0, t, 0)),
                  pl.BlockSpec((1, 1, ncmp, HEAD), lambda b, h, t: (b, h, 0, 0)),
                  pl.BlockSpec((1, 1, ncmp, HEAD), lambda b, h, t: (b, h, 0, 0)),
                  pl.BlockSpec((ncmp, nsel), lambda b, h, t: (0, 0))],
        out_specs=[pl.BlockSpec((1, 1, g, tq, HEAD), lambda b, h, t: (b, h, 0, t, 0)),
                   pl.BlockSpec((1, 1, tq, nsel), lambda b, h, t: (b, h, t, 0))],
        out_shape=[jax.ShapeDtypeStruct((bsz, hkv, g, seq, HEAD), F32),
                   jax.ShapeDtypeStruct((bsz, hkv, seq, nsel), BF16)],
        compiler_params=_params(3),
        name="nsa_cmp",
    )(q_hm, k_cmp, v_cmp, overlap)


def _nsa_attn_kernel(q_ref, sel_ref, ks_ref, vs_ref, kw_ref, vw_ref, osel_ref, owin_ref, *, tq, kt, scale):
    g = q_ref.shape[2]
    seq = ks_ref.shape[2]
    nsel = sel_ref.shape[3]
    qi = pl.program_id(2)
    start = qi * tq
    q = q_ref[0, 0].reshape(g * tq, HEAD)
    sel = sel_ref[0, 0]
    rows = g * tq
    t_pos = start + (lax.broadcasted_iota(jnp.int32, (rows, kt), 0) & (tq - 1))
    key_in_tile = lax.broadcasted_iota(jnp.int32, (rows, kt), 1)
    blk_row = lax.broadcasted_iota(jnp.int32, (nsel, kt), 0)
    blk_col = lax.broadcasted_iota(jnp.int32, (nsel, kt), 1) // SEL_L

    def body(j, carry):
        m, l, acc = carry
        base = pl.multiple_of(j * kt, kt)
        kb = ks_ref[0, 0, pl.ds(base, kt), :]
        vb = vs_ref[0, 0, pl.ds(base, kt), :]
        s = _dg(q, kb, nt=True) * scale
        expand = jnp.where(blk_row == blk_col + j * (kt // SEL_L), 1.0, 0.0).astype(BF16)
        chosen = jnp.dot(sel, expand, preferred_element_type=F32)
        chosen = jnp.concatenate([chosen] * g, axis=0) > 0.5
        mask = chosen & (key_in_tile + base <= t_pos)
        s = jnp.where(mask, s, -1e30)
        m_new = jnp.maximum(m, jnp.max(s, axis=-1, keepdims=True))
        alpha = jnp.exp(m - m_new)
        p = jnp.where(mask, jnp.exp(s - m_new), 0.0)
        l = alpha * l + jnp.sum(p, axis=-1, keepdims=True)
        acc = alpha * acc + jnp.dot(p.astype(BF16), vb, preferred_element_type=F32)
        return m_new, l, acc

    n_kt = (start + tq + kt - 1) // kt
    init = (jnp.full((rows, 1), -1e30, F32), jnp.zeros((rows, 1), F32), jnp.zeros((rows, HEAD), F32))
    m, l, acc = lax.fori_loop(0, n_kt, body, init)
    osel_ref[0, 0] = (acc / l).reshape(g, tq, HEAD).astype(osel_ref.dtype)

    span = WINDOW + tq
    wbase = jnp.maximum(start - WINDOW, 0) if seq > span else 0
    span = min(span, seq)
    wbase = pl.multiple_of(wbase, tq)
    kb = kw_ref[0, 0, pl.ds(wbase, span), :]
    vb = vw_ref[0, 0, pl.ds(wbase, span), :]
    s = _dg(q, kb, nt=True) * scale
    tw = start + (lax.broadcasted_iota(jnp.int32, s.shape, 0) & (tq - 1))
    wpos = wbase + lax.broadcasted_iota(jnp.int32, s.shape, 1)
    mask = (wpos <= tw) & (wpos > tw - WINDOW)
    s = jnp.where(mask, s, -1e30)
    e = jnp.where(mask, jnp.exp(s - jnp.max(s, axis=-1, keepdims=True)), 0.0)
    p = e / jnp.sum(e, axis=-1, keepdims=True)
    ow = jnp.dot(p.astype(BF16), vb, preferred_element_type=F32)
    owin_ref[0, 0] = ow.reshape(g, tq, HEAD).astype(owin_ref.dtype)


def _nsa_attn(q_hm, sel, ks, vs, kw, vw, tq=128, kt=512):
    bsz, hkv, g, seq, _ = q_hm.shape
    nsel = sel.shape[-1]
    kt = min(kt, seq)
    qspec = pl.BlockSpec((1, 1, g, tq, HEAD), lambda b, h, t: (b, h, 0, t, 0))
    kvspec = pl.BlockSpec((1, 1, seq, HEAD), lambda b, h, t: (b, h, 0, 0))
    return pl.pallas_call(
        functools.partial(_nsa_attn_kernel, tq=tq, kt=kt, scale=HEAD ** -0.5),
        grid=(bsz, hkv, seq // tq),
        in_specs=[qspec, pl.BlockSpec((1, 1, tq, nsel), lambda b, h, t: (b, h, t, 0))] + [kvspec] * 4,
        out_specs=[qspec, qspec],
        out_shape=[jax.ShapeDtypeStruct((bsz, hkv, g, seq, HEAD), F32)] * 2,
        compiler_params=_params(3),
        name="nsa_attn",
    )(q_hm, sel, ks, vs, kw, vw)


def _nsa_gate_kernel(gl_ref, oc_ref, os_ref, ow_ref, o_ref):
    gate = jax.nn.sigmoid(gl_ref[...])
    src = lax.broadcasted_iota(jnp.int32, (128, BRANCH_W), 0)
    head = lax.broadcasted_iota(jnp.int32, (128, BRANCH_W), 1) // HEAD
    acc = None
    for j, ref in enumerate((oc_ref, os_ref, ow_ref)):
        expand = jnp.where(src == head * 3 + j, 1.0, 0.0).astype(BF16)
        term = _mm(gate, expand, pa=3) * ref[...]
        acc = term if acc is None else acc + term
    o_ref[...] = acc.astype(o_ref.dtype)


def _nsa_gate(p_small, o_cmp, o_sel, o_win, tt=512):
    n = p_small.shape[0]
    tt = min(tt, n)
    row = pl.BlockSpec((tt, BRANCH_W), lambda i: (i, 0))
    return pl.pallas_call(
        _nsa_gate_kernel,
        grid=(n // tt,),
        in_specs=[pl.BlockSpec((tt, 128), lambda i: (i, C_GL // 128)), row, row, row],
        out_specs=row,
        out_shape=jax.ShapeDtypeStruct((n, BRANCH_W), BF16),
        compiler_params=_params(1),
        name="nsa_gate",
    )(p_small, o_cmp, o_sel, o_win)


def _nsa(p_small, bsz, seq, qk_gain, cmp_pos, cmp_w1, cmp_b1, cmp_w2):
    n = bsz * seq
    hkv, g = NSA_KV_HEADS, NSA_GROUP
    q_n, ks_n, kw_n = _nsa_norm(p_small, qk_gain)
    kv_hm = lambda a: a.reshape(bsz, seq, hkv, HEAD).transpose(0, 2, 1, 3)
    cols = lambda c: p_small[:, c:c + 128]
    q_hm = q_n.reshape(bsz, seq, hkv, g, HEAD).transpose(0, 2, 3, 1, 4)
    ncmp = seq // COMP_STRIDE
    kc_vc = jnp.stack([kv_hm(cols(C_KC)), kv_hm(cols(C_VC))]).reshape(2, bsz, hkv, ncmp, COMP_STRIDE * HEAD)
    cmp = _nsa_compress(kc_vc, cmp_pos, cmp_w1, cmp_b1, cmp_w2, qk_gain[1])
    nsel = seq // SEL_L
    c0 = np.arange(ncmp)[:, None] * COMP_STRIDE
    s0 = np.arange(nsel)[None, :] * SEL_L
    overlap = np.clip(np.minimum(c0 + COMP_L, s0 + SEL_L) - np.maximum(c0, s0), 0, None) / COMP_L
    o_cmp, sel = _nsa_cmp(q_hm, cmp[0], cmp[1], jnp.asarray(overlap, BF16))
    o_sel, o_win = _nsa_attn(q_hm, sel, kv_hm(ks_n), kv_hm(cols(C_VS)).astype(BF16),
                             kv_hm(kw_n), kv_hm(cols(C_VW)).astype(BF16))
    tm = lambda a: a.transpose(0, 3, 1, 2, 4).reshape(n, BRANCH_W)
    return _nsa_gate(p_small, tm(o_cmp), tm(o_sel), tm(o_win))


HALO = 16


def _conv_pool_kernel(bg_ref, cg_ref, xi_ref, pu_ref, cw_ref, conv_ref, pool_ref, cz_ref, cp_ref):
    tt = xi_ref.shape[0]
    ti = pl.program_id(1)

    @pl.when(ti == 0)
    def _():
        cz_ref[...] = jnp.zeros_like(cz_ref)
        cp_ref[...] = jnp.zeros_like(cp_ref)

    def history(x, carry_ref):
        ext = jnp.concatenate([carry_ref[...], x], axis=0)
        carry_ref[...] = x[tt - HALO:tt, :]
        return ext

    lag = lambda ext, s: pltpu.roll(ext, s, axis=0)
    body = lambda ext: ext[HALO:HALO + tt, :]

    z = cg_ref[...] * xi_ref[...]
    ze = history(z, cz_ref)
    cw = cw_ref[...]
    y = cw[2:3, :] * z + cw[1:2, :] * body(lag(ze, 1)) + cw[0:1, :] * body(lag(ze, 2))
    conv_ref[...] = (bg_ref[...] * y).astype(conv_ref.dtype)

    u = pu_ref[...]
    sums = [history(u, cp_ref)]
    for w in (1, 2, 4, 8):
        sums.append(sums[-1] + lag(sums[-1], w))
    count = (ti * tt + 1 + lax.broadcasted_iota(jnp.int32, (tt, 128), 0)).astype(F32)
    outs = []
    for gi, w in enumerate(POOL_WINDOWS):
        sl = slice(gi * 128, (gi + 1) * 128)
        outs.append(body(sums[gi + 1])[:, sl] / jnp.minimum(count, float(w)) - u[:, sl])
    pool_ref[...] = jnp.concatenate(outs, axis=1).astype(pool_ref.dtype)


def _conv_pool(p_small, bsz, seq, conv_w, tt=512):
    tt = min(tt, seq)
    nt = seq // tt
    n = bsz * seq
    col = lambda c: pl.BlockSpec((tt, 512), lambda b, t: (b * nt + t, c // 512))
    row = pl.BlockSpec((tt, 512), lambda b, t: (b * nt + t, 0))
    return pl.pallas_call(
        _conv_pool_kernel,
        grid=(bsz, nt),
        in_specs=[col(C_CONV), col(C_CONV + 512), col(C_CONV + 1024), col(C_POOL),
                  pl.BlockSpec((8, 512), lambda b, t: (0, 0))],
        out_specs=[row, row],
        out_shape=[jax.ShapeDtypeStruct((n, 512), BF16)] * 2,
        scratch_shapes=[pltpu.VMEM((HALO, 512), F32)] * 2,
        compiler_params=_params(2),
        name="conv_pool",
    )(p_small, p_small, p_small, p_small, jnp.pad(conv_w, ((0, 8 - conv_w.shape[0]), (0, 0))))


def _mem_attn_kernel(q_ref, kv_ref, gq_ref, gk_ref, o_ref, *, scale):
    outs = []
    for h in range(MEM_HEADS):
        sl = slice(h * MEM_HEAD, (h + 1) * MEM_HEAD)
        q = q_ref[:, sl]
        q = q * lax.rsqrt(jnp.mean(q * q, axis=-1, keepdims=True) + NORM_EPS) * gq_ref[...]
        k = kv_ref[0, :, sl]
        k = k * lax.rsqrt(jnp.mean(k * k, axis=-1, keepdims=True) + NORM_EPS) * gk_ref[...]
        v = kv_ref[0, :, BRANCH_W + h * MEM_HEAD:BRANCH_W + (h + 1) * MEM_HEAD]
        s = _dg(q.astype(BF16), k.astype(BF16), nt=True) * scale
        e = jnp.exp(s - jnp.max(s, axis=-1, keepdims=True))
        p = e / jnp.sum(e, axis=-1, keepdims=True)
        outs.append(jnp.dot(p.astype(BF16), v.astype(BF16), preferred_element_type=F32))
    o_ref[...] = jnp.concatenate(outs, axis=1).astype(o_ref.dtype)


def _mem_attn(p_small, kv, bsz, seq, qk_gain, tq=512):
    tq = min(tq, seq)
    nt = seq // tq
    mlen = kv.shape[1]
    return pl.pallas_call(
        functools.partial(_mem_attn_kernel, scale=MEM_HEAD ** -0.5),
        grid=(bsz, nt),
        in_specs=[pl.BlockSpec((tq, 512), lambda b, t: (b * nt + t, C_MEM // 512)),
                  pl.BlockSpec((1, mlen, 2 * BRANCH_W), lambda b, t: (b, 0, 0)),
                  pl.BlockSpec((1, MEM_HEAD), lambda b, t: (0, 0)),
                  pl.BlockSpec((1, MEM_HEAD), lambda b, t: (0, 0))],
        out_specs=pl.BlockSpec((tq, 512), lambda b, t: (b * nt + t, 0)),
        out_shape=jax.ShapeDtypeStruct((bsz * seq, 512), BF16),
        compiler_params=_params(2),
        name="mem_attn",
    )(p_small, kv, qk_gain[0].reshape(1, -1), qk_gain[1].reshape(1, -1))


def _merge_kernel(h_ref, wg_ref, y_ref, wb_ref, pooled_ref, wp_ref, ps_ref, o_ref):
    h = h_ref[...]
    gate = lambda i: jax.nn.sigmoid(jnp.dot(h, wg_ref[i], preferred_element_type=F32))
    z_pool = jnp.dot(pooled_ref[...], wp_ref[0].astype(BF16), preferred_element_type=F32) * ps_ref[...]
    acc = gate(4) * z_pool
    for i in range(4):
        acc = acc + gate(i) * jnp.dot(y_ref[i], wb_ref[i].astype(BF16), preferred_element_type=F32)
    o_ref[...] = acc.astype(o_ref.dtype)


def _merge(h, w_gate, ys, w_branch, pooled, pool_w, pool_scale, tm=512):
    n = h.shape[0]
    tm = min(tm, n)
    tn = 512
    return pl.pallas_call(
        _merge_kernel,
        grid=(D_MODEL // tn, n // tm),
        in_specs=[pl.BlockSpec((tm, D_MODEL), lambda j, i: (i, 0)),
                  pl.BlockSpec((5, D_MODEL, tn), lambda j, i: (0, 0, j)),
                  pl.BlockSpec((4, tm, BRANCH_W), lambda j, i: (0, i, 0)),
                  pl.BlockSpec((4, BRANCH_W, tn), lambda j, i: (0, 0, j)),
                  pl.BlockSpec((tm, 128), lambda j, i: (i, j)),
                  pl.BlockSpec((1, 128, tn), lambda j, i: (j, 0, 0)),
                  pl.BlockSpec((1, tn), lambda j, i: (0, j))],
        out_specs=pl.BlockSpec((tm, tn), lambda j, i: (i, j)),
        out_shape=jax.ShapeDtypeStruct((n, D_MODEL), BF16),
        compiler_params=_params(2),
        name="merge",
    )(h, w_gate, ys, w_branch, pooled, pool_w, pool_scale.reshape(1, -1))


def _pack_w_in(w, vres_w):
    pad = lambda a, width: jnp.pad(a, ((0, 0), (0, width - a.shape[1])))
    nsa = w[:, 1984:3288]
    vres_cols = pad(vres_w, 128) if vres_w is not None else jnp.zeros((D_MODEL, 128), w.dtype)
    small = jnp.concatenate([
        w[:, 0:1536], pad(w[:, 1536:1632], 128), pad(w[:, 1632:1728], 128), w[:, 1728:1984],
        nsa[:, 0:512], w[:, 3288:4824], w[:, 4824:5336], w[:, 5336:5848],
        nsa[:, 512:1280], pad(nsa[:, 1280:1304], 128), vres_cols], axis=1).astype(BF16)
    gate = w[:, 5848:].reshape(D_MODEL, 5, D_MODEL).transpose(1, 0, 2).astype(BF16)
    return small, gate


def kernel(x, mem, norm_mix, norm_ffn, norm_mem, w_in, rwkv_mu, rwkv_w0, rwkv_w2, rwkv_a0, rwkv_a2, rwkv_g2, rwkv_kk, rwkv_ka, rwkv_rk, rwkv_ln_w, rwkv_ln_b, vres_in, vres_mu, vres_v0, vres_up, nsa_qk_gain, nsa_cmp_pos, nsa_cmp_w1, nsa_cmp_b1, nsa_cmp_w2, conv_w, pool_w, pool_scale, mem_wkv, mem_qk_gain, w_branch, w_out, ffn_w1, ffn_w3, ffn_w2, moe_router, moe_w1, moe_w3, moe_w2):
    bsz, seq, d = x.shape
    n = bsz * seq
    depth = w_in.shape[0]
    mlen = mem.shape[1]
    xf = x.reshape(n, d)
    memf = mem.reshape(bsz * mlen, d)
    v_first = None
    for l in range(depth):
        h = _rmsnorm(xf, norm_mix[l])
        w_small, w_gate = _pack_w_in(w_in[l], vres_in[l - 1] if l > 0 else None)
        p_small = _matmul(h, w_small, tm=1024, tn=512, name="in_proj")
        vres = (vres_mu[l - 1], vres_v0[l - 1], vres_up[l - 1], v_first) if l > 0 else None
        r, lw, k, v, kk, b, g = _rwkv_prep(p_small, bsz, seq, rwkv_mu[l], rwkv_w0[l], rwkv_w2[l], rwkv_a0[l],
                                           rwkv_a2[l], rwkv_g2[l], rwkv_kk[l], rwkv_ka[l], vres)
        if l == 0:
            v_first = v
        y_rec = _rwkv_recurrence(r, lw, k, v, kk, b, bsz, seq)
        y_rwkv = _rwkv_post(y_rec, r, k, v, g, rwkv_rk[l], rwkv_ln_w[l], rwkv_ln_b[l])
        y_nsa = _nsa(p_small, bsz, seq, nsa_qk_gain[l], nsa_cmp_pos[l], nsa_cmp_w1[l], nsa_cmp_b1[l], nsa_cmp_w2[l])
        y_conv, pooled = _conv_pool(p_small, bsz, seq, conv_w[l])
        mem_n = _rmsnorm(memf, norm_mem[l])
        kv = _matmul(mem_n, mem_wkv[l], tm=512, tn=512, name="mem_kv").reshape(bsz, mlen, 2 * BRANCH_W)
        y_mem = _mem_attn(p_small, kv, bsz, seq, mem_qk_gain[l])
        merged = _merge(h, w_gate, jnp.stack([y_rwkv, y_nsa, y_conv, y_mem]), w_branch[l], pooled, pool_w[l],
                        pool_scale[l])
        xf = _matmul(merged, w_out[l], tm=1024, tn=512, res=xf, name="out_proj")

        if l % 2 == 0:
            h2 = _rmsnorm(xf, norm_ffn[l])
            e = l // 2
            act = _swiglu_up(h2, ffn_w1[e], ffn_w3[e])
            xf = _matmul(act, ffn_w2[e], tm=1024, tn=1024, tk=1408, res=xf, name="ffn_down")
        else:
            e = l // 2
            h2, comb = _rmsnorm_router(xf, norm_ffn[l], moe_router[e])
            for ex in range(N_EXPERTS):
                act = _swiglu_up(h2, moe_w1[e, ex], moe_w3[e, ex], tn=256)
                xf = _matmul(act, moe_w2[e, ex], tm=1024, tn=1024, tk=1408, res=xf, comb=comb, scale_col=ex,
                             name="moe_down")
    return xf.reshape(bsz, seq, d)
```

```python
import functools

import jax
import jax.numpy as jnp
import numpy as np
from jax import lax
from jax.experimental import pallas as pl
from jax.experimental.pallas import tpu as pltpu

F32 = jnp.float32
BF16 = jnp.bfloat16

D_MODEL = 2048
BRANCH_W = 512
HEAD = 64
RWKV_HEADS = 8
NSA_KV_HEADS = 2
NSA_GROUP = 4
COMP_L = 32
COMP_STRIDE = 16
SEL_L = 64
SEL_N = 16
WINDOW = 512
MEM_HEADS = 4
MEM_HEAD = 128
POOL_WINDOWS = (2, 4, 8, 16)
N_EXPERTS = 8
NORM_EPS = 1e-6
RWKV_GN_EPS = 64e-5
RWKV_CHUNK = 64

VMEM_LIMIT_BYTES = 56 * 1024 * 1024

C_RWKV, C_NSAQ, C_CONV, C_POOL, C_MEM = 0, 2048, 2560, 4096, 4608
C_KC, C_VC, C_KS, C_VS, C_KW, C_VW, C_GL, C_VRES = 5120, 5248, 5376, 5504, 5632, 5760, 5888, 6016
N_SMALL = 6144


def _params(n_axes):
    return pltpu.CompilerParams(dimension_semantics=("arbitrary",) * n_axes,
                                vmem_limit_bytes=VMEM_LIMIT_BYTES)


def _split(a, n):
    pieces, r = [], a
    for i in range(n):
        p = r.astype(BF16)
        pieces.append(p)
        if i + 1 < n:
            r = r - p.astype(F32)
    return pieces


def _dg(a, b, nt):
    dims = (((1,), (1,)), ((), ())) if nt else (((1,), (0,)), ((), ()))
    return lax.dot_general(a, b, dims, preferred_element_type=F32)


def _mm(a, b, pa=1, pb=1, nt=False):
    sa = _split(a, pa) if a.dtype != BF16 else [a]
    sb = _split(b, pb) if b.dtype != BF16 else [b]
    order = max(len(sa), len(sb))
    acc = None
    for i, x in enumerate(sa):
        for j, y in enumerate(sb):
            if i + j < order:
                t = _dg(x, y, nt)
                acc = t if acc is None else acc + t
    return acc


def _head_sum_matrix(width, head):
    r = lax.broadcasted_iota(jnp.int32, (width, width), 0) // head
    c = lax.broadcasted_iota(jnp.int32, (width, width), 1) // head
    return jnp.where(r == c, 1.0, 0.0).astype(BF16)


def _group_mean(x, head):
    hs = _head_sum_matrix(x.shape[-1], head)
    return _mm(x, hs, pa=3) * (1.0 / head)


def _rmsnorm_kernel(x_ref, g_ref, o_ref):
    x = x_ref[...]
    y = x * lax.rsqrt(jnp.mean(x * x, axis=-1, keepdims=True) + NORM_EPS)
    o_ref[...] = (y * g_ref[...]).astype(o_ref.dtype)


def _rmsnorm(x, g, tm=512):
    m, d = x.shape
    tm = min(tm, m)
    return pl.pallas_call(
        _rmsnorm_kernel,
        grid=(m // tm,),
        in_specs=[pl.BlockSpec((tm, d), lambda i: (i, 0)), pl.BlockSpec((1, d), lambda i: (0, 0))],
        out_specs=pl.BlockSpec((tm, d), lambda i: (i, 0)),
        out_shape=jax.ShapeDtypeStruct((m, d), BF16),
        compiler_params=_params(1),
        name="rmsnorm",
    )(x, g.reshape(1, d))


def _rmsnorm_router_kernel(x_ref, g_ref, wr_ref, o_ref, comb_ref, assign_ref, count_ref, carry_ref):
    tm = x_ref.shape[0]

    @pl.when(pl.program_id(0) == 0)
    def _():
        carry_ref[...] = jnp.zeros_like(carry_ref)

    x = x_ref[...]
    y = x * lax.rsqrt(jnp.mean(x * x, axis=-1, keepdims=True) + NORM_EPS) * g_ref[...]
    o_ref[...] = y
    logits = _mm(y, wr_ref[...], pa=3, pb=3)
    lane = lax.broadcasted_iota(jnp.int32, logits.shape, 1)
    neg = jnp.float32(-3e38)
    lg = jnp.where(lane < N_EXPERTS, logits, neg)
    m1 = jnp.max(lg, axis=-1, keepdims=True)
    i1 = jnp.min(jnp.where(lg == m1, lane, 1 << 20), axis=-1, keepdims=True)
    lg2 = jnp.where(lane == i1, neg, lg)
    m2 = jnp.max(lg2, axis=-1, keepdims=True)
    i2 = jnp.min(jnp.where(lg2 == m2, lane, 1 << 20), axis=-1, keepdims=True)
    e2 = jnp.exp(m2 - m1)
    w1 = 1.0 / (1.0 + e2)
    w2 = e2 / (1.0 + e2)
    comb_ref[...] = jnp.where(lane == i1, w1, 0.0) + jnp.where(lane == i2, w2, 0.0)
    assign = jnp.where((lane == i1) | (lane == i2), 1.0, 0.0)
    assign_ref[...] = assign
    row = lax.broadcasted_iota(jnp.int32, (tm, tm), 0)
    col = lax.broadcasted_iota(jnp.int32, (tm, tm), 1)
    tril = jnp.where(row >= col, 1.0, 0.0).astype(BF16)
    count = jnp.dot(tril, assign.astype(BF16), preferred_element_type=F32) + carry_ref[0:1, :]
    count_ref[...] = count
    carry_ref[0:1, :] = count[tm - 1:tm, :]


def _rmsnorm_router(x, g, router, tm=512):
    m, d = x.shape
    tm = min(tm, m)
    wr = jnp.pad(router, ((0, 0), (0, 128 - router.shape[1])))
    lanes = pl.BlockSpec((tm, 128), lambda i: (i, 0))
    return pl.pallas_call(
        _rmsnorm_router_kernel,
        grid=(m // tm,),
        in_specs=[pl.BlockSpec((tm, d), lambda i: (i, 0)), pl.BlockSpec((1, d), lambda i: (0, 0)),
                  pl.BlockSpec((d, 128), lambda i: (0, 0))],
        out_specs=[pl.BlockSpec((tm, d), lambda i: (i, 0)), lanes, lanes, lanes],
        out_shape=[jax.ShapeDtypeStruct((m, d), F32)] + [jax.ShapeDtypeStruct((m, 128), F32)] * 3,
        scratch_shapes=[pltpu.VMEM((8, 128), F32)],
        compiler_params=_params(1),
        name="rmsnorm_router",
    )(x, g.reshape(1, d), wr)


MOE_TILE = 256


def _row_copy(src_hbm, row, dst, slot, sem):
    return pltpu.make_async_copy(src_hbm.at[pl.ds(row, 1), :], dst.at[pl.ds(slot, 1), :], sem)


def _gather_rows_kernel(idx_ref, src_hbm, o_ref, buf_ref, sem):
    tm = o_ref.shape[0]

    def issue(r, carry):
        _row_copy(src_hbm, idx_ref[0, 0, r], buf_ref, r, sem).start()
        return carry

    def wait(r, carry):
        _row_copy(src_hbm, 0, buf_ref, r, sem).wait()
        return carry

    lax.fori_loop(0, tm, issue, 0)
    lax.fori_loop(0, tm, wait, 0)
    o_ref[...] = buf_ref[...].astype(o_ref.dtype)


def _gather_rows(src, idx, tm=MOE_TILE):
    r = idx.shape[0]
    d = src.shape[1]
    return pl.pallas_call(
        _gather_rows_kernel,
        grid=(r // tm,),
        in_specs=[pl.BlockSpec((1, 1, tm), lambda i: (i, 0, 0), memory_space=pltpu.SMEM),
                  pl.BlockSpec(memory_space=pl.ANY)],
        out_specs=pl.BlockSpec((tm, d), lambda i: (i, 0)),
        out_shape=jax.ShapeDtypeStruct((r, d), BF16),
        scratch_shapes=[pltpu.VMEM((tm, d), F32), pltpu.SemaphoreType.DMA(())],
        compiler_params=_params(1),
        name="moe_gather",
    )(idx.reshape(r // tm, 1, tm), src)


def _moe_up_kernel(te_ref, na_ref, x_ref, w1_ref, w3_ref, o_ref):
    active = pl.program_id(1) < na_ref[0]

    @pl.when(active)
    def _():
        x = x_ref[...]
        a = jnp.dot(x, w1_ref[0].astype(BF16), preferred_element_type=F32)
        b = jnp.dot(x, w3_ref[0].astype(BF16), preferred_element_type=F32)
        o_ref[...] = (a * jax.nn.sigmoid(a) * b).astype(o_ref.dtype)

    @pl.when(jnp.logical_not(active))
    def _():
        o_ref[...] = jnp.zeros_like(o_ref)


def _moe_down_kernel(te_ref, na_ref, a_ref, w_ref, o_ref):
    active = pl.program_id(1) < na_ref[0]

    @pl.when(active)
    def _():
        o_ref[...] = jnp.dot(a_ref[...], w_ref[0].astype(BF16), preferred_element_type=F32)

    @pl.when(jnp.logical_not(active))
    def _():
        o_ref[...] = jnp.zeros_like(o_ref)


def _moe_experts(x_sorted, tile_expert, n_active, w1, w3, w2, tm=MOE_TILE, tn_up=256, tn_down=512):
    rows, d = x_sorted.shape
    ff = w1.shape[2]
    n_tiles = rows // tm
    act = pl.pallas_call(
        _moe_up_kernel,
        grid_spec=pltpu.PrefetchScalarGridSpec(
            num_scalar_prefetch=2,
            grid=(ff // tn_up, n_tiles),
            in_specs=[pl.BlockSpec((tm, d), lambda j, i, te, na: (i, 0)),
                      pl.BlockSpec((1, d, tn_up), lambda j, i, te, na: (te[i], 0, j)),
                      pl.BlockSpec((1, d, tn_up), lambda j, i, te, na: (te[i], 0, j))],
            out_specs=pl.BlockSpec((tm, tn_up), lambda j, i, te, na: (i, j))),
        out_shape=jax.ShapeDtypeStruct((rows, ff), BF16),
        compiler_params=_params(2),
        name="moe_up",
    )(tile_expert, n_active, x_sorted, w1, w3)
    return pl.pallas_call(
        _moe_down_kernel,
        grid_spec=pltpu.PrefetchScalarGridSpec(
            num_scalar_prefetch=2,
            grid=(d // tn_down, n_tiles),
            in_specs=[pl.BlockSpec((tm, ff), lambda j, i, te, na: (i, 0)),
                      pl.BlockSpec((1, ff, tn_down), lambda j, i, te, na: (te[i], 0, j))],
            out_specs=pl.BlockSpec((tm, tn_down), lambda j, i, te, na: (i, j))),
        out_shape=jax.ShapeDtypeStruct((rows, d), F32),
        compiler_params=_params(2),
        name="moe_down",
    )(tile_expert, n_active, act, w2)


def _moe_combine_kernel(pa_ref, pb_ref, y_hbm, x_ref, w_ref, o_ref, buf_ref, sem):
    tm = x_ref.shape[0]

    def issue(r, carry):
        _row_copy(y_hbm, pa_ref[0, 0, r], buf_ref.at[0], r, sem).start()
        _row_copy(y_hbm, pb_ref[0, 0, r], buf_ref.at[1], r, sem).start()
        return carry

    def wait(r, carry):
        _row_copy(y_hbm, 0, buf_ref.at[0], r, sem).wait()
        _row_copy(y_hbm, 0, buf_ref.at[1], r, sem).wait()
        return carry

    lax.fori_loop(0, tm, issue, 0)
    lax.fori_loop(0, tm, wait, 0)
    w = w_ref[...]
    o_ref[...] = x_ref[...] + w[:, 0:1] * buf_ref[0] + w[:, 1:2] * buf_ref[1]


def _moe_combine(x, y_sorted, pos_a, pos_b, weights, tm=MOE_TILE):
    n, d = x.shape
    idx_spec = pl.BlockSpec((1, 1, tm), lambda i: (i, 0, 0), memory_space=pltpu.SMEM)
    return pl.pallas_call(
        _moe_combine_kernel,
        grid=(n // tm,),
        in_specs=[idx_spec, idx_spec, pl.BlockSpec(memory_space=pl.ANY),
                  pl.BlockSpec((tm, d), lambda i: (i, 0)), pl.BlockSpec((tm, 128), lambda i: (i, 0))],
        out_specs=pl.BlockSpec((tm, d), lambda i: (i, 0)),
        out_shape=jax.ShapeDtypeStruct((n, d), F32),
        scratch_shapes=[pltpu.VMEM((2, tm, d), F32), pltpu.SemaphoreType.DMA(())],
        compiler_params=_params(1),
        name="moe_combine",
    )(pos_a.reshape(n // tm, 1, tm), pos_b.reshape(n // tm, 1, tm), y_sorted, x, weights)


def _moe(x, norm_g, router, w1, w3, w2):
    n, d = x.shape
    tm = min(MOE_TILE, n)
    h2, comb, assign, count = _rmsnorm_router(x, norm_g, router)
    assigned = assign[:, :N_EXPERTS] > 0.5
    count = count[:, :N_EXPERTS].astype(jnp.int32)
    total = count[-1]
    padded = (total + tm - 1) // tm * tm
    ends = jnp.cumsum(padded)
    starts = ends - padded
    dest = starts[None, :] + count - 1
    rows = (2 * n // tm + N_EXPERTS) * tm
    token = jnp.broadcast_to(jnp.arange(n, dtype=jnp.int32)[:, None], dest.shape)
    src = jnp.zeros((rows + 1,), jnp.int32).at[jnp.where(assigned, dest, rows)].set(token)[:rows]
    first = jnp.argmax(assigned, axis=1)
    second = N_EXPERTS - 1 - jnp.argmax(assigned[:, ::-1], axis=1)
    take = lambda a, i: jnp.take_along_axis(a, i[:, None], axis=1)[:, 0]
    pos_a, pos_b = take(dest, first), take(dest, second)
    weights = jnp.pad(jnp.stack([take(comb, first), take(comb, second)], axis=1), ((0, 0), (0, 126)))
    tile_start = jnp.arange(rows // tm, dtype=jnp.int32) * tm
    tile_expert = jnp.minimum(jnp.sum(tile_start[:, None] >= ends[None, :], axis=1), N_EXPERTS - 1).astype(jnp.int32)
    n_active = (ends[-1:] // tm).astype(jnp.int32)

    x_sorted = _gather_rows(h2, src, tm)
    y_sorted = _moe_experts(x_sorted, tile_expert, n_active, w1, w3, w2, tm)
    return _moe_combine(x, y_sorted, pos_a, pos_b, weights, tm)


def _matmul_kernel(*refs, nk, has_res):
    a_ref, w_ref = refs[0], refs[1]
    res_ref = refs[2] if has_res else None
    o_ref, acc_ref = refs[-2], refs[-1]
    k = pl.program_id(2)
    part = jnp.dot(a_ref[...], w_ref[...].astype(BF16), preferred_element_type=F32)

    @pl.when(k == 0)
    def _():
        acc_ref[...] = part

    @pl.when(k > 0)
    def _():
        acc_ref[...] += part

    @pl.when(k == nk - 1)
    def _():
        out = acc_ref[...]
        if has_res:
            out = out + res_ref[...]
        o_ref[...] = out.astype(o_ref.dtype)


def _matmul(a, w, *, tm, tn, tk=None, out_dtype=F32, res=None, name="matmul"):
    m, kdim = a.shape
    n = w.shape[1]
    tk = kdim if tk is None else tk
    tm, tn = min(tm, m), min(tn, n)
    nk = kdim // tk
    assert m % tm == 0 and n % tn == 0 and kdim % tk == 0
    in_specs = [pl.BlockSpec((tm, tk), lambda j, i, k: (i, k)), pl.BlockSpec((tk, tn), lambda j, i, k: (k, j))]
    args = [a, w]
    if res is not None:
        in_specs.append(pl.BlockSpec((tm, tn), lambda j, i, k: (i, j)))
        args.append(res)
    return pl.pallas_call(
        functools.partial(_matmul_kernel, nk=nk, has_res=res is not None),
        grid=(n // tn, m // tm, nk),
        in_specs=in_specs,
        out_specs=pl.BlockSpec((tm, tn), lambda j, i, k: (i, j)),
        out_shape=jax.ShapeDtypeStruct((m, n), out_dtype),
        scratch_shapes=[pltpu.VMEM((tm, tn), F32)],
        compiler_params=_params(3),
        name=name,
    )(*args)


def _swiglu_up_kernel(h_ref, w1_ref, w3_ref, o_ref):
    h = h_ref[...]
    a = jnp.dot(h, w1_ref[...].astype(BF16), preferred_element_type=F32)
    b = jnp.dot(h, w3_ref[...].astype(BF16), preferred_element_type=F32)
    o_ref[...] = (a * jax.nn.sigmoid(a) * b).astype(o_ref.dtype)


def _swiglu_up(h, w1, w3, tm=1024, tn=512):
    m, kdim = h.shape
    n = w1.shape[1]
    tm, tn = min(tm, m), min(tn, n)
    assert m % tm == 0 and n % tn == 0
    return pl.pallas_call(
        _swiglu_up_kernel,
        grid=(n // tn, m // tm),
        in_specs=[pl.BlockSpec((tm, kdim), lambda j, i: (i, 0)),
                  pl.BlockSpec((kdim, tn), lambda j, i: (0, j)),
                  pl.BlockSpec((kdim, tn), lambda j, i: (0, j))],
        out_specs=pl.BlockSpec((tm, tn), lambda j, i: (i, j)),
        out_shape=jax.ShapeDtypeStruct((m, n), BF16),
        compiler_params=_params(2),
        name="swiglu_up",
    )(h, w1, w3)


def _shift_rows(x, carry_row):
    rolled = pltpu.roll(x, 1, axis=0)
    row = lax.broadcasted_iota(jnp.int32, x.shape, 0)
    return jnp.where(row == 0, carry_row, rolled)


def _rwkv_prep_kernel(*refs, has_vres):
    (u_ref, mu_ref, w0_ref, w2_ref, a0_ref, a2_ref, g2_ref, kkw_ref, kaw_ref) = refs[:9]
    pos = 9
    if has_vres:
        vd_ref, vmu_ref, v0_ref, vup_ref, vfirst_ref = refs[pos:pos + 5]
        pos += 5
    r_ref, lw_ref, k_ref, v_ref, kk_ref, b_ref, g_ref = refs[pos:pos + 7]
    pos += 7
    cu_ref = refs[pos]
    cv_ref = refs[pos + 1] if has_vres else None
    tt = u_ref.shape[0]

    @pl.when(pl.program_id(1) == 0)
    def _():
        cu_ref[...] = jnp.zeros_like(cu_ref)
        if has_vres:
            cv_ref[...] = jnp.zeros_like(cv_ref)

    u = u_ref[...]
    prev = _shift_rows(u, cu_ref[0:1, :])
    cu_ref[0:1, :] = u[tt - 1:tt, :]
    uf = u + (prev - u) * mu_ref[...]
    r, k, v = uf[:, 0:512], uf[:, 512:1024], uf[:, 1024:1536]
    wd, ad, gd = uf[:, 1536:1664], uf[:, 1664:1792], uf[:, 1792:2048]

    x = w0_ref[...] + _mm(jnp.tanh(wd), w2_ref[...], 2, 2)
    softplus = jnp.maximum(-x, 0.0) + jnp.log(1.0 + jnp.exp(-jnp.abs(x)))
    lw_ref[...] = -jnp.exp(-softplus - 0.5)
    a = jax.nn.sigmoid(a0_ref[...] + _mm(ad, a2_ref[...], 2, 2))
    g_ref[...] = _mm(jax.nn.sigmoid(gd), g2_ref[...], 2, 2)
    if has_vres:
        vd = vd_ref[...]
        vprev = _shift_rows(vd, cv_ref[0:1, :])
        cv_ref[0:1, :] = vd[tt - 1:tt, :]
        vdf = vd + (vprev - vd) * vmu_ref[...]
        v = v + (vfirst_ref[...] - v) * jax.nn.sigmoid(v0_ref[...] + _mm(vdf, vup_ref[...], 2, 2))
    kk = k * kkw_ref[...]
    ss = _group_mean(kk * kk, HEAD) * HEAD
    kk = kk * lax.rsqrt(jnp.maximum(ss, 1e-24))
    r_ref[...] = r
    k_ref[...] = k * (1.0 + (a - 1.0) * kaw_ref[...])
    v_ref[...] = v
    kk_ref[...] = kk
    b_ref[...] = kk * a


def _rwkv_prep(p_small, bsz, seq, mu, w0, w2, a0, a2, g2, kkw, kaw, vres, tt=256):
    tt = min(tt, seq)
    nt = seq // tt
    n = bsz * seq
    row = lambda b, t: (b * nt + t, 0)
    const = lambda b, t: (0, 0)
    pad_rows = lambda w: jnp.pad(w, ((0, 128 - w.shape[0]), (0, 0)))
    mu_p = jnp.concatenate([mu[:1536], jnp.pad(mu[1536:1632], (0, 32)), jnp.pad(mu[1632:1728], (0, 32)), mu[1728:]])
    vec = lambda a: a.reshape(1, -1)
    args = [p_small, vec(mu_p), vec(w0), pad_rows(w2), vec(a0), pad_rows(a2), g2, vec(kkw), vec(kaw)]
    in_specs = [pl.BlockSpec((tt, 2048), row), pl.BlockSpec((1, 2048), const), pl.BlockSpec((1, 512), const),
                pl.BlockSpec((128, 512), const), pl.BlockSpec((1, 512), const), pl.BlockSpec((128, 512), const),
                pl.BlockSpec((256, 512), const), pl.BlockSpec((1, 512), const), pl.BlockSpec((1, 512), const)]
    scratch = [pltpu.VMEM((8, 2048), F32)]
    if vres is not None:
        vmu, v0, vup, vfirst = vres
        args += [p_small, vec(jnp.pad(vmu, (0, 64))), vec(v0), pad_rows(vup), vfirst]
        in_specs += [pl.BlockSpec((tt, 128), lambda b, t: (b * nt + t, C_VRES // 128)), pl.BlockSpec((1, 128), const),
                     pl.BlockSpec((1, 512), const), pl.BlockSpec((128, 512), const), pl.BlockSpec((tt, 512), row)]
        scratch.append(pltpu.VMEM((8, 128), F32))
    return pl.pallas_call(
        functools.partial(_rwkv_prep_kernel, has_vres=vres is not None),
        grid=(bsz, nt),
        in_specs=in_specs,
        out_specs=[pl.BlockSpec((tt, 512), row)] * 7,
        out_shape=[jax.ShapeDtypeStruct((n, 512), F32)] * 7,
        scratch_shapes=scratch,
        compiler_params=_params(2),
        name="rwkv_prep",
    )(*args)


def _rwkv_chunk_kernel(r_ref, lw_ref, k_ref, v_ref, kk_ref, b_ref, g_ref, rk_ref, lnw_ref, lnb_ref, o_ref, s_ref,
                       *, pw):
    nb, c, _ = r_ref.shape

    @pl.when(pl.program_id(0) == 0)
    def _():
        s_ref[...] = jnp.zeros_like(s_ref)

    row = lax.broadcasted_iota(jnp.int32, (c, c), 0)
    col = lax.broadcasted_iota(jnp.int32, (c, c), 1)
    tril_incl = jnp.where(row >= col, 1.0, 0.0).astype(BF16)
    strict = row > col
    incl = row >= col
    eye = jnp.where(row == col, 1.0, 0.0)
    hrow = lax.broadcasted_iota(jnp.int32, (HEAD, HEAD), 0)
    hcol = lax.broadcasted_iota(jnp.int32, (HEAD, HEAD), 1)
    eye_h = jnp.where(hrow == hcol, 1.0, 0.0).astype(BF16)
    mm = functools.partial(_mm, pa=pw, pb=pw)
    transpose = lambda a, pieces: _mm(eye_h, a, pb=pieces, nt=True)

    heads = [(bi, h) for bi in range(nb) for h in range(RWKV_HEADS)]
    per_head = lambda full: [full[bi][:, h * HEAD:(h + 1) * HEAD] for bi, h in heads]
    each = lambda f, *lists: [f(*vals) for vals in zip(*lists)]

    r, lw, k, v, kk, b = ([ref[bi] for bi in range(nb)] for ref in (r_ref, lw_ref, k_ref, v_ref, kk_ref, b_ref))
    cum = [_mm(tril_incl, x, pb=3) for x in lw]
    g_inv = [jnp.exp(-x) for x in cum]
    a_t = per_head(each(lambda kk_, c_, lw_: kk_ * jnp.exp(c_ - lw_), kk, cum, lw))
    b_t = per_head(each(lambda b_, gi: b_ * gi, b, g_inv))
    k_t = per_head(each(lambda k_, gi: k_ * gi, k, g_inv))
    r_t = per_head(each(lambda r_, c_: r_ * jnp.exp(c_), r, cum))
    vh = per_head(v)
    tail = [jnp.exp(x[c - 1:c, :] - x) for x in cum]
    bh = per_head(each(lambda b_, t_: b_ * t_, b, tail))
    kh = per_head(each(lambda k_, t_: k_ * t_, k, tail))
    g_last = per_head([jnp.exp(x[c - 1:c, :]) for x in cum])

    a_ab = each(lambda x, y: jnp.where(strict, mm(x, y, nt=True), 0.0), a_t, b_t)
    a_ak = each(lambda x, y: jnp.where(strict, mm(x, y, nt=True), 0.0), a_t, k_t)
    a_rb = each(lambda x, y: jnp.where(incl, mm(x, y, nt=True), 0.0), r_t, b_t)
    a_rk = each(lambda x, y: jnp.where(incl, mm(x, y, nt=True), 0.0), r_t, k_t)
    x = [eye - n for n in a_ab]
    p = each(mm, a_ab, a_ab)
    steps = int(np.log2(c)) - 1
    for i in range(steps):
        x = each(lambda x_, p_: x_ + mm(x_, p_), x, p)
        if i + 1 < steps:
            p = each(mm, p, p)
    w1 = each(mm, x, a_t)
    akv = each(mm, a_ak, vh)
    w2 = each(mm, x, akv)
    y_in = each(lambda ark, v_, arb, w2_: mm(ark, v_) - mm(arb, w2_), a_rk, vh, a_rb, w2)
    q_h = each(lambda r_, arb, w1_: r_ - mm(arb, w1_), r_t, a_rb, w1)
    bh_t = [transpose(x_, pw) for x_ in bh]
    kh_t = [transpose(x_, pw) for x_ in kh]
    decay = [transpose(jnp.broadcast_to(x_, (HEAD, HEAD)), 3) for x_ in g_last]
    g_s = each(lambda kt_, v_, bt_, w2_: mm(kt_, v_) - mm(bt_, w2_), kh_t, vh, bh_t, w2)
    s = [s_ref[i] for i in range(len(heads))]
    ys = each(lambda yi, q_, s_: yi + mm(q_, s_), y_in, q_h, s)
    w1s = each(mm, w1, s)
    for i, (d_, s_, bt_, ws_, gs_) in enumerate(zip(decay, s, bh_t, w1s, g_s)):
        s_ref[i] = d_ * s_ - mm(bt_, ws_) + gs_

    for bi in range(nb):
        y = jnp.concatenate(ys[bi * RWKV_HEADS:(bi + 1) * RWKV_HEADS], axis=1)
        mean = _group_mean(y, HEAD)
        d = y - mean
        var = _group_mean(d * d, HEAD)
        yn = d * lax.rsqrt(var + RWKV_GN_EPS) * lnw_ref[...] + lnb_ref[...]
        bonus = _group_mean(r[bi] * k[bi] * rk_ref[...], HEAD) * HEAD * v[bi]
        o_ref[bi] = ((yn + bonus) * g_ref[bi]).astype(o_ref.dtype)


def _rwkv_recurrence(r, lw, k, v, kk, b, g, rk, lnw, lnb, bsz, seq, pw=1):
    c = min(RWKV_CHUNK, seq)
    blk = pl.BlockSpec((bsz, c, BRANCH_W), lambda t: (0, t, 0))
    cst = pl.BlockSpec((1, BRANCH_W), lambda t: (0, 0))
    as3 = lambda a: a.reshape(bsz, seq, BRANCH_W)
    out = pl.pallas_call(
        functools.partial(_rwkv_chunk_kernel, pw=pw),
        grid=(seq // c,),
        in_specs=[blk] * 7 + [cst] * 3,
        out_specs=blk,
        out_shape=jax.ShapeDtypeStruct((bsz, seq, BRANCH_W), BF16),
        scratch_shapes=[pltpu.VMEM((bsz * RWKV_HEADS, HEAD, HEAD), F32)],
        compiler_params=_params(1),
        name="rwkv_chunk",
    )(as3(r), as3(lw), as3(k), as3(v), as3(kk), as3(b), as3(g), rk.reshape(1, -1), lnw.reshape(1, -1),
      lnb.reshape(1, -1))
    return out.reshape(bsz * seq, BRANCH_W)


def _nsa_norm_kernel(q_ref, ks_ref, kw_ref, vs_ref, vw_ref, gq_ref, gs_ref, gw_ref,
                     qo_ref, kso_ref, kwo_ref, vso_ref, vwo_ref):
    for x_ref, g_ref, o_ref in ((q_ref, gq_ref, qo_ref), (ks_ref, gs_ref, kso_ref), (kw_ref, gw_ref, kwo_ref)):
        x = x_ref[...]
        ms = _group_mean(x * x, HEAD)
        o_ref[...] = (x * lax.rsqrt(ms + NORM_EPS) * g_ref[...]).astype(o_ref.dtype)
    vso_ref[...] = vs_ref[...].astype(vso_ref.dtype)
    vwo_ref[...] = vw_ref[...].astype(vwo_ref.dtype)


def _nsa_norm(p_small, qk_gain, tt=512):
    n = p_small.shape[0]
    tt = min(tt, n)
    col = lambda c, w: pl.BlockSpec((tt, w), lambda i: (i, c // w))
    cst = lambda w: pl.BlockSpec((1, w), lambda i: (0, 0))
    out = lambda w: pl.BlockSpec((tt, w), lambda i: (i, 0))
    gq = jnp.tile(qk_gain[0], 8).reshape(1, 512)
    gs = jnp.tile(qk_gain[2], 2).reshape(1, 128)
    gw = jnp.tile(qk_gain[3], 2).reshape(1, 128)
    return pl.pallas_call(
        _nsa_norm_kernel,
        grid=(n // tt,),
        in_specs=[col(C_NSAQ, 512), col(C_KS, 128), col(C_KW, 128), col(C_VS, 128), col(C_VW, 128),
                  cst(512), cst(128), cst(128)],
        out_specs=[out(512)] + [out(128)] * 4,
        out_shape=[jax.ShapeDtypeStruct((n, 512), BF16)] + [jax.ShapeDtypeStruct((n, 128), BF16)] * 4,
        compiler_params=_params(1),
        name="nsa_norm",
    )(p_small, p_small, p_small, p_small, p_small, gq, gs, gw)


def _gelu_tanh(x):
    return 0.5 * x * (1.0 + jnp.tanh(0.7978845608028654 * (x + 0.044715 * x * x * x)))


def _nsa_compress_kernel(x_ref, pos_ref, w1_ref, b1_ref, w2_ref, gain_ref, o_ref):
    half = x_ref.shape[-1]
    x = x_ref[0, 0, 0].astype(BF16)
    w1 = w1_ref[0].astype(BF16)
    z_lo = jnp.dot(x, w1[:half], preferred_element_type=F32)
    z_hi = jnp.dot(x, w1[half:], preferred_element_type=F32)
    nrow = z_hi.shape[0]
    pos = jnp.broadcast_to(pos_ref[0], (8, 2 * half))
    const = _mm(pos, w1, pa=2)[0:1] + b1_ref[0]
    pre = z_lo + pltpu.roll(z_hi, nrow - 1, axis=0) + const
    out = jnp.dot(_gelu_tanh(pre).astype(BF16), w2_ref[0].astype(BF16), preferred_element_type=F32)
    is_key = pl.program_id(0) == 0
    normed = out * lax.rsqrt(jnp.mean(out * out, axis=-1, keepdims=True) + NORM_EPS) * gain_ref[...]
    o_ref[0, 0, 0] = jnp.where(is_key, normed, out)


def _nsa_compress(kc_vc, cmp_pos, cmp_w1, cmp_b1, cmp_w2, gain):
    _, bsz, hkv, nc, wid = kc_vc.shape
    return pl.pallas_call(
        _nsa_compress_kernel,
        grid=(2, bsz, hkv),
        in_specs=[pl.BlockSpec((1, 1, 1, nc, wid), lambda i, b, h: (i, b, h, 0, 0)),
                  pl.BlockSpec((1, 1, 2 * wid), lambda i, b, h: (i, 0, 0)),
                  pl.BlockSpec((1, 2 * wid, 2 * HEAD), lambda i, b, h: (i, 0, 0)),
                  pl.BlockSpec((1, 1, 2 * HEAD), lambda i, b, h: (i, 0, 0)),
                  pl.BlockSpec((1, 2 * HEAD, HEAD), lambda i, b, h: (i, 0, 0)),
                  pl.BlockSpec((1, HEAD), lambda i, b, h: (0, 0))],
        out_specs=pl.BlockSpec((1, 1, 1, nc, HEAD), lambda i, b, h: (i, b, h, 0, 0)),
        out_shape=jax.ShapeDtypeStruct((2, bsz, hkv, nc, HEAD), F32),
        compiler_params=_params(3),
        name="nsa_compress",
    )(kc_vc, cmp_pos.reshape(2, 1, 2 * wid), cmp_w1, cmp_b1.reshape(2, 1, 2 * HEAD), cmp_w2, gain.reshape(1, HEAD))


def _nsa_cmp_kernel(q_ref, kc_ref, vc_ref, ov_ref, o_ref, sel_ref, *, tq, scale):
    hkv, ncmp = kc_ref.shape[1], kc_ref.shape[2]
    nsel = ov_ref.shape[1]
    g = NSA_GROUP
    start = pl.program_id(1) * tq
    q_all = q_ref[...]
    outs, sels = [], []
    for h in range(hkv):
        q = _group_rows(q_all, h, g)
        s = _dg(q, kc_ref[0, h].astype(BF16), nt=True) * scale
        t_pos = start + (lax.broadcasted_iota(jnp.int32, s.shape, 0) & (tq - 1))
        c_end = lax.broadcasted_iota(jnp.int32, s.shape, 1) * COMP_STRIDE + (COMP_L - 1)
        mask = c_end <= t_pos
        s = jnp.where(mask, s, -1e30)
        e = jnp.exp(s - jnp.max(s, axis=-1, keepdims=True))
        p = jnp.where(mask, e / jnp.sum(e, axis=-1, keepdims=True), 0.0)
        o = jnp.dot(p.astype(BF16), vc_ref[0, h].astype(BF16), preferred_element_type=F32)
        outs += [o[i * tq:(i + 1) * tq] for i in range(g)]
        p_sum = p[0:tq]
        for i in range(1, g):
            p_sum = p_sum + p[i * tq:(i + 1) * tq]
        imp = _mm(p_sum, ov_ref[...], pa=3)
        sid = lax.broadcasted_iota(jnp.int32, imp.shape, 1)
        cur = (start + lax.broadcasted_iota(jnp.int32, imp.shape, 0)) // SEL_L
        forced = (sid == 0) | (sid == cur) | (sid == cur - 1)
        val = jnp.where(forced, 1e9, jnp.where(sid <= cur, imp, -1e9))
        rank = jnp.zeros(imp.shape, jnp.int32)
        for j in range(nsel):
            cj = val[:, j:j + 1]
            ahead = (cj > val) | ((cj == val) & (sid > j))
            rank = rank + ahead.astype(jnp.int32)
        k_top = min(SEL_N, nsel)
        sels.append(jnp.where((rank < k_top) & (val > -1e8), 1.0, 0.0))
    o_ref[...] = jnp.concatenate(outs, axis=1)
    sel_ref[...] = jnp.concatenate(sels, axis=1).astype(sel_ref.dtype)


def _group_rows(q_all, h, g):
    return jnp.concatenate([q_all[:, (h * g + i) * HEAD:(h * g + i + 1) * HEAD] for i in range(g)], axis=0)


def _nsa_cmp(q_n, k_cmp, v_cmp, overlap, bsz, seq, tq=128):
    hkv = k_cmp.shape[1]
    ncmp, nsel = overlap.shape
    nt = seq // tq
    return pl.pallas_call(
        functools.partial(_nsa_cmp_kernel, tq=tq, scale=HEAD ** -0.5),
        grid=(bsz, nt),
        in_specs=[pl.BlockSpec((tq, BRANCH_W), lambda b, t: (b * nt + t, 0)),
                  pl.BlockSpec((1, hkv, ncmp, HEAD), lambda b, t: (b, 0, 0, 0)),
                  pl.BlockSpec((1, hkv, ncmp, HEAD), lambda b, t: (b, 0, 0, 0)),
                  pl.BlockSpec((ncmp, nsel), lambda b, t: (0, 0))],
        out_specs=[pl.BlockSpec((tq, BRANCH_W), lambda b, t: (b * nt + t, 0)),
                   pl.BlockSpec((tq, hkv * nsel), lambda b, t: (b * nt + t, 0))],
        out_shape=[jax.ShapeDtypeStruct((bsz * seq, BRANCH_W), F32),
                   jax.ShapeDtypeStruct((bsz * seq, hkv * nsel), BF16)],
        compiler_params=_params(2),
        name="nsa_cmp",
    )(q_n, k_cmp, v_cmp, overlap)


def _nsa_attn_kernel(q_ref, sel_ref, ks_ref, vs_ref, kw_ref, vw_ref, oc_ref, gl_ref, o_ref, *, tq, kt, scale):
    g = NSA_GROUP
    hkv = ks_ref.shape[1] // HEAD
    seq = ks_ref.shape[0]
    nsel = sel_ref.shape[1] // hkv
    start = pl.program_id(1) * tq
    q_all = q_ref[...]
    sel_all = sel_ref[...]
    qs = [_group_rows(q_all, h, g) for h in range(hkv)]
    sels = [sel_all[:, h * nsel:(h + 1) * nsel] for h in range(hkv)]
    head_cols = lambda x, h: x[:, h * HEAD:(h + 1) * HEAD]
    rows = g * tq
    t_pos = start + (lax.broadcasted_iota(jnp.int32, (rows, kt), 0) & (tq - 1))
    key_in_tile = lax.broadcasted_iota(jnp.int32, (rows, kt), 1)
    blk_row = lax.broadcasted_iota(jnp.int32, (nsel, kt), 0)
    blk_col = lax.broadcasted_iota(jnp.int32, (nsel, kt), 1) // SEL_L

    def body(j, carry):
        base = pl.multiple_of(j * kt, kt)
        kb2 = ks_ref[pl.ds(base, kt), :]
        vb2 = vs_ref[pl.ds(base, kt), :]
        expand = jnp.where(blk_row == blk_col + j * (kt // SEL_L), 1.0, 0.0).astype(BF16)
        causal = key_in_tile + base <= t_pos
        out = []
        for h in range(hkv):
            m, l, acc = carry[h]
            s = _dg(qs[h], head_cols(kb2, h), nt=True) * scale
            chosen = jnp.dot(sels[h], expand, preferred_element_type=F32)
            mask = (jnp.concatenate([chosen] * g, axis=0) > 0.5) & causal
            s = jnp.where(mask, s, -1e30)
            m_new = jnp.maximum(m, jnp.max(s, axis=-1, keepdims=True))
            alpha = jnp.exp(m - m_new)
            p = jnp.where(mask, jnp.exp(s - m_new), 0.0)
            l = alpha * l + jnp.sum(p, axis=-1, keepdims=True)
            acc = alpha * acc + jnp.dot(p.astype(BF16), head_cols(vb2, h), preferred_element_type=F32)
            out.append((m_new, l, acc))
        return tuple(out)

    n_kt = (start + tq + kt - 1) // kt
    init = tuple((jnp.full((rows, 1), -1e30, F32), jnp.zeros((rows, 1), F32), jnp.zeros((rows, HEAD), F32))
                 for _ in range(hkv))
    final = lax.fori_loop(0, n_kt, body, init)
    o_sel = [acc / l for _, l, acc in final]

    span = min(WINDOW + tq, seq)
    wbase = pl.multiple_of(jnp.maximum(start - WINDOW, 0), tq) if seq > span else 0
    kb2 = kw_ref[pl.ds(wbase, span), :]
    vb2 = vw_ref[pl.ds(wbase, span), :]
    tw = start + (lax.broadcasted_iota(jnp.int32, (rows, span), 0) & (tq - 1))
    wpos = wbase + lax.broadcasted_iota(jnp.int32, (rows, span), 1)
    wmask = (wpos <= tw) & (wpos > tw - WINDOW)
    o_win = []
    for h in range(hkv):
        s = jnp.where(wmask, _dg(qs[h], head_cols(kb2, h), nt=True) * scale, -1e30)
        e = jnp.where(wmask, jnp.exp(s - jnp.max(s, axis=-1, keepdims=True)), 0.0)
        p = e / jnp.sum(e, axis=-1, keepdims=True)
        o_win.append(jnp.dot(p.astype(BF16), head_cols(vb2, h), preferred_element_type=F32))

    token_major = lambda per_head: jnp.concatenate(
        [o[i * tq:(i + 1) * tq] for o in per_head for i in range(g)], axis=1)
    gate = jax.nn.sigmoid(gl_ref[...])
    src = lax.broadcasted_iota(jnp.int32, (128, BRANCH_W), 0)
    head = lax.broadcasted_iota(jnp.int32, (128, BRANCH_W), 1) // HEAD
    acc = None
    for j, branch in enumerate((oc_ref[...], token_major(o_sel), token_major(o_win))):
        expand = jnp.where(src == head * 3 + j, 1.0, 0.0).astype(BF16)
        term = _mm(gate, expand, pa=3) * branch
        acc = term if acc is None else acc + term
    o_ref[...] = acc.astype(o_ref.dtype)


def _nsa_attn(p_small, q_n, sel, ks, vs, kw, vw, o_cmp, bsz, seq, tq=128, kt=512):
    kt = min(kt, seq)
    nt = seq // tq
    row = lambda w: pl.BlockSpec((tq, w), lambda b, t: (b * nt + t, 0))
    kv = pl.BlockSpec((seq, 2 * HEAD), lambda b, t: (b, 0))
    return pl.pallas_call(
        functools.partial(_nsa_attn_kernel, tq=tq, kt=kt, scale=HEAD ** -0.5),
        grid=(bsz, nt),
        in_specs=[row(BRANCH_W), row(sel.shape[1]), kv, kv, kv, kv, row(BRANCH_W),
                  pl.BlockSpec((tq, 128), lambda b, t: (b * nt + t, C_GL // 128))],
        out_specs=row(BRANCH_W),
        out_shape=jax.ShapeDtypeStruct((bsz * seq, BRANCH_W), BF16),
        compiler_params=_params(2),
        name="nsa_attn",
    )(q_n, sel, ks, vs, kw, vw, o_cmp, p_small)


def _nsa(p_small, bsz, seq, qk_gain, cmp_pos, cmp_w1, cmp_b1, cmp_w2):
    hkv = NSA_KV_HEADS
    q_n, ks_n, kw_n, vs_b, vw_b = _nsa_norm(p_small, qk_gain)
    ncmp = seq // COMP_STRIDE
    grouped = lambda c: p_small[:, c:c + 128].reshape(bsz, ncmp, COMP_STRIDE, hkv, HEAD).transpose(0, 3, 1, 2, 4)
    kc_vc = jnp.stack([grouped(C_KC), grouped(C_VC)]).reshape(2, bsz, hkv, ncmp, COMP_STRIDE * HEAD)
    cmp = _nsa_compress(kc_vc, cmp_pos, cmp_w1, cmp_b1, cmp_w2, qk_gain[1])
    nsel = seq // SEL_L
    c0 = np.arange(ncmp)[:, None] * COMP_STRIDE
    s0 = np.arange(nsel)[None, :] * SEL_L
    overlap = np.clip(np.minimum(c0 + COMP_L, s0 + SEL_L) - np.maximum(c0, s0), 0, None) / COMP_L
    o_cmp, sel = _nsa_cmp(q_n, cmp[0], cmp[1], jnp.asarray(overlap, BF16), bsz, seq)
    return _nsa_attn(p_small, q_n, sel, ks_n, vs_b, kw_n, vw_b, o_cmp, bsz, seq)


HALO = 16


def _conv_pool_kernel(bg_ref, cg_ref, xi_ref, pu_ref, cw_ref, conv_ref, pool_ref, cz_ref, cp_ref):
    tt = xi_ref.shape[0]
    ti = pl.program_id(1)

    @pl.when(ti == 0)
    def _():
        cz_ref[...] = jnp.zeros_like(cz_ref)
        cp_ref[...] = jnp.zeros_like(cp_ref)

    def history(x, carry_ref):
        ext = jnp.concatenate([carry_ref[...], x], axis=0)
        carry_ref[...] = x[tt - HALO:tt, :]
        return ext

    lag = lambda ext, s: pltpu.roll(ext, s, axis=0)
    body = lambda ext: ext[HALO:HALO + tt, :]

    z = cg_ref[...] * xi_ref[...]
    ze = history(z, cz_ref)
    cw = cw_ref[...]
    y = cw[2:3, :] * z + cw[1:2, :] * body(lag(ze, 1)) + cw[0:1, :] * body(lag(ze, 2))
    conv_ref[...] = (bg_ref[...] * y).astype(conv_ref.dtype)

    u = pu_ref[...]
    sums = [history(u, cp_ref)]
    for w in (1, 2, 4, 8):
        sums.append(sums[-1] + lag(sums[-1], w))
    count = (ti * tt + 1 + lax.broadcasted_iota(jnp.int32, (tt, 128), 0)).astype(F32)
    outs = []
    for gi, w in enumerate(POOL_WINDOWS):
        sl = slice(gi * 128, (gi + 1) * 128)
        outs.append(body(sums[gi + 1])[:, sl] / jnp.minimum(count, float(w)) - u[:, sl])
    pool_ref[...] = jnp.concatenate(outs, axis=1).astype(pool_ref.dtype)


def _conv_pool(p_small, bsz, seq, conv_w, tt=512):
    tt = min(tt, seq)
    nt = seq // tt
    n = bsz * seq
    col = lambda c: pl.BlockSpec((tt, 512), lambda b, t: (b * nt + t, c // 512))
    row = pl.BlockSpec((tt, 512), lambda b, t: (b * nt + t, 0))
    return pl.pallas_call(
        _conv_pool_kernel,
        grid=(bsz, nt),
        in_specs=[col(C_CONV), col(C_CONV + 512), col(C_CONV + 1024), col(C_POOL),
                  pl.BlockSpec((8, 512), lambda b, t: (0, 0))],
        out_specs=[row, row],
        out_shape=[jax.ShapeDtypeStruct((n, 512), BF16)] * 2,
        scratch_shapes=[pltpu.VMEM((HALO, 512), F32)] * 2,
        compiler_params=_params(2),
        name="conv_pool",
    )(p_small, p_small, p_small, p_small, jnp.pad(conv_w, ((0, 8 - conv_w.shape[0]), (0, 0))))


def _mem_attn_kernel(q_ref, kv_ref, gq_ref, gk_ref, o_ref, *, scale):
    outs = []
    for h in range(MEM_HEADS):
        sl = slice(h * MEM_HEAD, (h + 1) * MEM_HEAD)
        q = q_ref[:, sl]
        q = q * lax.rsqrt(jnp.mean(q * q, axis=-1, keepdims=True) + NORM_EPS) * gq_ref[...]
        k = kv_ref[0, :, sl]
        k = k * lax.rsqrt(jnp.mean(k * k, axis=-1, keepdims=True) + NORM_EPS) * gk_ref[...]
        v = kv_ref[0, :, BRANCH_W + h * MEM_HEAD:BRANCH_W + (h + 1) * MEM_HEAD]
        s = _dg(q.astype(BF16), k.astype(BF16), nt=True) * scale
        e = jnp.exp(s - jnp.max(s, axis=-1, keepdims=True))
        p = e / jnp.sum(e, axis=-1, keepdims=True)
        outs.append(jnp.dot(p.astype(BF16), v.astype(BF16), preferred_element_type=F32))
    o_ref[...] = jnp.concatenate(outs, axis=1).astype(o_ref.dtype)


def _mem_attn(p_small, kv, bsz, seq, qk_gain, tq=512):
    tq = min(tq, seq)
    nt = seq // tq
    mlen = kv.shape[1]
    return pl.pallas_call(
        functools.partial(_mem_attn_kernel, scale=MEM_HEAD ** -0.5),
        grid=(bsz, nt),
        in_specs=[pl.BlockSpec((tq, 512), lambda b, t: (b * nt + t, C_MEM // 512)),
                  pl.BlockSpec((1, mlen, 2 * BRANCH_W), lambda b, t: (b, 0, 0)),
                  pl.BlockSpec((1, MEM_HEAD), lambda b, t: (0, 0)),
                  pl.BlockSpec((1, MEM_HEAD), lambda b, t: (0, 0))],
        out_specs=pl.BlockSpec((tq, 512), lambda b, t: (b * nt + t, 0)),
        out_shape=jax.ShapeDtypeStruct((bsz * seq, 512), BF16),
        compiler_params=_params(2),
        name="mem_attn",
    )(p_small, kv, qk_gain[0].reshape(1, -1), qk_gain[1].reshape(1, -1))


def _merge_kernel(h_ref, *refs):
    wg_refs, y_refs = refs[0:5], refs[5:9]
    wb_ref, pooled_ref, wp_ref, ps_ref, o_ref = refs[9:]
    h = h_ref[...]
    gate = lambda i: jax.nn.sigmoid(jnp.dot(h, wg_refs[i][...], preferred_element_type=F32))
    z_pool = jnp.dot(pooled_ref[...], wp_ref[0].astype(BF16), preferred_element_type=F32) * ps_ref[...]
    acc = gate(4) * z_pool
    for i in range(4):
        acc = acc + gate(i) * jnp.dot(y_refs[i][...], wb_ref[i].astype(BF16), preferred_element_type=F32)
    o_ref[...] = acc.astype(o_ref.dtype)


def _merge(h, w_gate, ys, w_branch, pooled, pool_w, pool_scale, tm=512):
    n = h.shape[0]
    tm = min(tm, n)
    tn = 512
    nj = D_MODEL // tn
    gate_spec = lambda g: pl.BlockSpec((D_MODEL, tn), lambda j, i: (0, g * nj + j))
    y_spec = pl.BlockSpec((tm, BRANCH_W), lambda j, i: (i, 0))
    return pl.pallas_call(
        _merge_kernel,
        grid=(nj, n // tm),
        in_specs=[pl.BlockSpec((tm, D_MODEL), lambda j, i: (i, 0))] + [gate_spec(g) for g in range(5)]
        + [y_spec] * 4
        + [pl.BlockSpec((4, BRANCH_W, tn), lambda j, i: (0, 0, j)),
           pl.BlockSpec((tm, 128), lambda j, i: (i, j)),
           pl.BlockSpec((1, 128, tn), lambda j, i: (j, 0, 0)),
           pl.BlockSpec((1, tn), lambda j, i: (0, j))],
        out_specs=pl.BlockSpec((tm, tn), lambda j, i: (i, j)),
        out_shape=jax.ShapeDtypeStruct((n, D_MODEL), BF16),
        compiler_params=_params(2),
        name="merge",
    )(h, *([w_gate] * 5), *ys, w_branch, pooled, pool_w, pool_scale.reshape(1, -1))


def _pack_w_in(w, vres_w):
    pad = lambda a, width: jnp.pad(a, ((0, 0), (0, width - a.shape[1])))
    w = w.astype(BF16)
    nsa = w[:, 1984:3288]
    vres_cols = pad(vres_w.astype(BF16), 128) if vres_w is not None else jnp.zeros((D_MODEL, 128), BF16)
    small = jnp.concatenate([
        w[:, 0:1536], pad(w[:, 1536:1632], 128), pad(w[:, 1632:1728], 128), w[:, 1728:1984],
        nsa[:, 0:512], w[:, 3288:4824], w[:, 4824:5336], w[:, 5336:5848],
        nsa[:, 512:1280], pad(nsa[:, 1280:1304], 128), vres_cols], axis=1)
    return small, w[:, 5848:]


def kernel(x, mem, norm_mix, norm_ffn, norm_mem, w_in, rwkv_mu, rwkv_w0, rwkv_w2, rwkv_a0, rwkv_a2, rwkv_g2, rwkv_kk, rwkv_ka, rwkv_rk, rwkv_ln_w, rwkv_ln_b, vres_in, vres_mu, vres_v0, vres_up, nsa_qk_gain, nsa_cmp_pos, nsa_cmp_w1, nsa_cmp_b1, nsa_cmp_w2, conv_w, pool_w, pool_scale, mem_wkv, mem_qk_gain, w_branch, w_out, ffn_w1, ffn_w3, ffn_w2, moe_router, moe_w1, moe_w3, moe_w2):
    bsz, seq, d = x.shape
    n = bsz * seq
    depth = w_in.shape[0]
    mlen = mem.shape[1]
    xf = x.reshape(n, d)
    memf = mem.reshape(bsz * mlen, d)
    v_first = None
    for l in range(depth):
        h = _rmsnorm(xf, norm_mix[l])
        w_small, w_gate = _pack_w_in(w_in[l], vres_in[l - 1] if l > 0 else None)
        p_small = _matmul(h, w_small, tm=1024, tn=512, name="in_proj")
        vres = (vres_mu[l - 1], vres_v0[l - 1], vres_up[l - 1], v_first) if l > 0 else None
        r, lw, k, v, kk, b, g = _rwkv_prep(p_small, bsz, seq, rwkv_mu[l], rwkv_w0[l], rwkv_w2[l], rwkv_a0[l],
                                           rwkv_a2[l], rwkv_g2[l], rwkv_kk[l], rwkv_ka[l], vres)
        if l == 0:
            v_first = v
        y_rwkv = _rwkv_recurrence(r, lw, k, v, kk, b, g, rwkv_rk[l], rwkv_ln_w[l], rwkv_ln_b[l], bsz, seq)
        y_nsa = _nsa(p_small, bsz, seq, nsa_qk_gain[l], nsa_cmp_pos[l], nsa_cmp_w1[l], nsa_cmp_b1[l], nsa_cmp_w2[l])
        y_conv, pooled = _conv_pool(p_small, bsz, seq, conv_w[l])
        mem_n = _rmsnorm(memf, norm_mem[l])
        kv = _matmul(mem_n, mem_wkv[l], tm=512, tn=512, name="mem_kv").reshape(bsz, mlen, 2 * BRANCH_W)
        y_mem = _mem_attn(p_small, kv, bsz, seq, mem_qk_gain[l])
        merged = _merge(h, w_gate, (y_rwkv, y_nsa, y_conv, y_mem), w_branch[l], pooled, pool_w[l], pool_scale[l])
        xf = _matmul(merged, w_out[l], tm=1024, tn=512, res=xf, name="out_proj")

        if l % 2 == 0:
            h2 = _rmsnorm(xf, norm_ffn[l])
            e = l // 2
            act = _swiglu_up(h2, ffn_w1[e], ffn_w3[e])
            xf = _matmul(act, ffn_w2[e], tm=1024, tn=1024, tk=1408, res=xf, name="ffn_down")
        else:
            e = l // 2
            xf = _moe(xf, norm_ffn[l], moe_router[e], moe_w1[e], moe_w3[e], moe_w2[e])
    return xf.reshape(bsz, seq, d)
```

```python
import functools

import jax
import jax.numpy as jnp
import numpy as np
from jax import lax
from jax.experimental import pallas as pl
from jax.experimental.pallas import tpu as pltpu

F32 = jnp.float32
BF16 = jnp.bfloat16

D_MODEL = 2048
BRANCH_W = 512
HEAD = 64
RWKV_HEADS = 8
NSA_KV_HEADS = 2
NSA_GROUP = 4
COMP_L = 32
COMP_STRIDE = 16
SEL_L = 64
SEL_N = 16
WINDOW = 512
MEM_HEADS = 4
MEM_HEAD = 128
POOL_WINDOWS = (2, 4, 8, 16)
N_EXPERTS = 8
NORM_EPS = 1e-6
RWKV_GN_EPS = 64e-5
RWKV_CHUNK = 64

VMEM_LIMIT_BYTES = 56 * 1024 * 1024

C_RWKV, C_NSAQ, C_CONV, C_POOL, C_MEM = 0, 2048, 2560, 4096, 4608
C_KC, C_VC, C_KS, C_VS, C_KW, C_VW, C_GL, C_VRES = 5120, 5248, 5376, 5504, 5632, 5760, 5888, 6016
N_SMALL = 6144


def _params(n_axes):
    return pltpu.CompilerParams(dimension_semantics=("arbitrary",) * n_axes,
                                vmem_limit_bytes=VMEM_LIMIT_BYTES)


def _split(a, n):
    pieces, r = [], a
    for i in range(n):
        p = r.astype(BF16)
        pieces.append(p)
        if i + 1 < n:
            r = r - p.astype(F32)
    return pieces


def _dg(a, b, nt):
    dims = (((1,), (1,)), ((), ())) if nt else (((1,), (0,)), ((), ()))
    return lax.dot_general(a, b, dims, preferred_element_type=F32)


def _mm(a, b, pa=1, pb=1, nt=False):
    sa = _split(a, pa) if a.dtype != BF16 else [a]
    sb = _split(b, pb) if b.dtype != BF16 else [b]
    order = max(len(sa), len(sb))
    acc = None
    for i, x in enumerate(sa):
        for j, y in enumerate(sb):
            if i + j < order:
                t = _dg(x, y, nt)
                acc = t if acc is None else acc + t
    return acc


def _head_sum_matrix(width, head):
    r = lax.broadcasted_iota(jnp.int32, (width, width), 0) // head
    c = lax.broadcasted_iota(jnp.int32, (width, width), 1) // head
    return jnp.where(r == c, 1.0, 0.0).astype(BF16)


def _group_mean(x, head):
    hs = _head_sum_matrix(x.shape[-1], head)
    return _mm(x, hs, pa=3) * (1.0 / head)


def _rmsnorm_kernel(x_ref, g_ref, o_ref):
    x = x_ref[...]
    y = x * lax.rsqrt(jnp.mean(x * x, axis=-1, keepdims=True) + NORM_EPS)
    o_ref[...] = (y * g_ref[...]).astype(o_ref.dtype)


def _rmsnorm(x, g, tm=512):
    m, d = x.shape
    tm = min(tm, m)
    return pl.pallas_call(
        _rmsnorm_kernel,
        grid=(m // tm,),
        in_specs=[pl.BlockSpec((tm, d), lambda i: (i, 0)), pl.BlockSpec((1, d), lambda i: (0, 0))],
        out_specs=pl.BlockSpec((tm, d), lambda i: (i, 0)),
        out_shape=jax.ShapeDtypeStruct((m, d), BF16),
        compiler_params=_params(1),
        name="rmsnorm",
    )(x, g.reshape(1, d))


def _rmsnorm_router_kernel(x_ref, g_ref, wr_ref, o_ref, comb_ref, assign_ref, count_ref, carry_ref):
    tm = x_ref.shape[0]

    @pl.when(pl.program_id(0) == 0)
    def _():
        carry_ref[...] = jnp.zeros_like(carry_ref)

    x = x_ref[...]
    y = x * lax.rsqrt(jnp.mean(x * x, axis=-1, keepdims=True) + NORM_EPS) * g_ref[...]
    o_ref[...] = y
    logits = _mm(y, wr_ref[...], pa=3, pb=3)
    lane = lax.broadcasted_iota(jnp.int32, logits.shape, 1)
    neg = jnp.float32(-3e38)
    lg = jnp.where(lane < N_EXPERTS, logits, neg)
    m1 = jnp.max(lg, axis=-1, keepdims=True)
    i1 = jnp.min(jnp.where(lg == m1, lane, 1 << 20), axis=-1, keepdims=True)
    lg2 = jnp.where(lane == i1, neg, lg)
    m2 = jnp.max(lg2, axis=-1, keepdims=True)
    i2 = jnp.min(jnp.where(lg2 == m2, lane, 1 << 20), axis=-1, keepdims=True)
    e2 = jnp.exp(m2 - m1)
    w1 = 1.0 / (1.0 + e2)
    w2 = e2 / (1.0 + e2)
    comb_ref[...] = jnp.where(lane == i1, w1, 0.0) + jnp.where(lane == i2, w2, 0.0)
    assign = jnp.where((lane == i1) | (lane == i2), 1.0, 0.0)
    assign_ref[...] = assign
    row = lax.broadcasted_iota(jnp.int32, (tm, tm), 0)
    col = lax.broadcasted_iota(jnp.int32, (tm, tm), 1)
    tril = jnp.where(row >= col, 1.0, 0.0).astype(BF16)
    count = jnp.dot(tril, assign.astype(BF16), preferred_element_type=F32) + carry_ref[0:1, :]
    count_ref[...] = count
    carry_ref[0:1, :] = count[tm - 1:tm, :]


def _rmsnorm_router(x, g, router, tm=512):
    m, d = x.shape
    tm = min(tm, m)
    wr = jnp.pad(router, ((0, 0), (0, 128 - router.shape[1])))
    lanes = pl.BlockSpec((tm, 128), lambda i: (i, 0))
    return pl.pallas_call(
        _rmsnorm_router_kernel,
        grid=(m // tm,),
        in_specs=[pl.BlockSpec((tm, d), lambda i: (i, 0)), pl.BlockSpec((1, d), lambda i: (0, 0)),
                  pl.BlockSpec((d, 128), lambda i: (0, 0))],
        out_specs=[pl.BlockSpec((tm, d), lambda i: (i, 0)), lanes, lanes, lanes],
        out_shape=[jax.ShapeDtypeStruct((m, d), F32)] + [jax.ShapeDtypeStruct((m, 128), F32)] * 3,
        scratch_shapes=[pltpu.VMEM((8, 128), F32)],
        compiler_params=_params(1),
        name="rmsnorm_router",
    )(x, g.reshape(1, d), wr)


MOE_TILE = 256


def _row_copy(src_hbm, row, dst, slot, sem):
    return pltpu.make_async_copy(src_hbm.at[pl.ds(row, 1), :], dst.at[pl.ds(slot, 1), :], sem)


def _gather_rows_kernel(idx_ref, src_hbm, o_ref, buf_ref, sem):
    tm = o_ref.shape[0]

    def issue(r, carry):
        _row_copy(src_hbm, idx_ref[0, 0, r], buf_ref, r, sem).start()
        return carry

    def wait(r, carry):
        _row_copy(src_hbm, 0, buf_ref, r, sem).wait()
        return carry

    lax.fori_loop(0, tm, issue, 0)
    lax.fori_loop(0, tm, wait, 0)
    o_ref[...] = buf_ref[...].astype(o_ref.dtype)


def _gather_rows(src, idx, tm=MOE_TILE):
    r = idx.shape[0]
    d = src.shape[1]
    return pl.pallas_call(
        _gather_rows_kernel,
        grid=(r // tm,),
        in_specs=[pl.BlockSpec((1, 1, tm), lambda i: (i, 0, 0), memory_space=pltpu.SMEM),
                  pl.BlockSpec(memory_space=pl.ANY)],
        out_specs=pl.BlockSpec((tm, d), lambda i: (i, 0)),
        out_shape=jax.ShapeDtypeStruct((r, d), BF16),
        scratch_shapes=[pltpu.VMEM((tm, d), F32), pltpu.SemaphoreType.DMA(())],
        compiler_params=_params(1),
        name="moe_gather",
    )(idx.reshape(r // tm, 1, tm), src)


def _expert_changed(te_ref, i):
    return (i == 0) | (te_ref[i] != te_ref[jnp.maximum(i - 1, 0)])


def _moe_up_kernel(te_ref, na_ref, x_ref, w1_ref, w3_ref, o_ref, w1c_ref, w3c_ref):
    i = pl.program_id(1)
    active = i < na_ref[0]

    @pl.when(active & _expert_changed(te_ref, i))
    def _():
        w1c_ref[...] = w1_ref[0].astype(BF16)
        w3c_ref[...] = w3_ref[0].astype(BF16)

    @pl.when(active)
    def _():
        x = x_ref[...]
        a = jnp.dot(x, w1c_ref[...], preferred_element_type=F32)
        b = jnp.dot(x, w3c_ref[...], preferred_element_type=F32)
        o_ref[...] = (a * jax.nn.sigmoid(a) * b).astype(o_ref.dtype)

    @pl.when(jnp.logical_not(active))
    def _():
        o_ref[...] = jnp.zeros_like(o_ref)


def _moe_down_kernel(te_ref, na_ref, a_ref, w_ref, o_ref, wc_ref):
    i = pl.program_id(1)
    active = i < na_ref[0]

    @pl.when(active & _expert_changed(te_ref, i))
    def _():
        wc_ref[...] = w_ref[0].astype(BF16)

    @pl.when(active)
    def _():
        o_ref[...] = jnp.dot(a_ref[...], wc_ref[...], preferred_element_type=F32)

    @pl.when(jnp.logical_not(active))
    def _():
        o_ref[...] = jnp.zeros_like(o_ref)


def _moe_experts(x_sorted, tile_expert, n_active, w1, w3, w2, tm=MOE_TILE):
    rows, d = x_sorted.shape
    ff = w1.shape[2]
    n_tiles = rows // tm
    tn_up, tn_down = ff // 2, d // 2
    once = pl.Buffered(1)
    act = pl.pallas_call(
        _moe_up_kernel,
        grid_spec=pltpu.PrefetchScalarGridSpec(
            num_scalar_prefetch=2,
            grid=(ff // tn_up, n_tiles),
            in_specs=[pl.BlockSpec((tm, d), lambda j, i, te, na: (i, 0)),
                      pl.BlockSpec((1, d, tn_up), lambda j, i, te, na: (te[i], 0, j), pipeline_mode=once),
                      pl.BlockSpec((1, d, tn_up), lambda j, i, te, na: (te[i], 0, j), pipeline_mode=once)],
            out_specs=pl.BlockSpec((tm, tn_up), lambda j, i, te, na: (i, j)),
            scratch_shapes=[pltpu.VMEM((d, tn_up), BF16)] * 2),
        out_shape=jax.ShapeDtypeStruct((rows, ff), BF16),
        compiler_params=_params(2),
        name="moe_up",
    )(tile_expert, n_active, x_sorted, w1, w3)
    return pl.pallas_call(
        _moe_down_kernel,
        grid_spec=pltpu.PrefetchScalarGridSpec(
            num_scalar_prefetch=2,
            grid=(d // tn_down, n_tiles),
            in_specs=[pl.BlockSpec((tm, ff), lambda j, i, te, na: (i, 0)),
                      pl.BlockSpec((1, ff, tn_down), lambda j, i, te, na: (te[i], 0, j), pipeline_mode=once)],
            out_specs=pl.BlockSpec((tm, tn_down), lambda j, i, te, na: (i, j)),
            scratch_shapes=[pltpu.VMEM((ff, tn_down), BF16)]),
        out_shape=jax.ShapeDtypeStruct((rows, d), F32),
        compiler_params=_params(2),
        name="moe_down",
    )(tile_expert, n_active, act, w2)


def _moe_combine_kernel(pa_ref, pb_ref, y_hbm, x_ref, w_ref, o_ref, buf_ref, sem):
    tm = x_ref.shape[0]

    def issue(r, carry):
        _row_copy(y_hbm, pa_ref[0, 0, r], buf_ref.at[0], r, sem).start()
        _row_copy(y_hbm, pb_ref[0, 0, r], buf_ref.at[1], r, sem).start()
        return carry

    def wait(r, carry):
        _row_copy(y_hbm, 0, buf_ref.at[0], r, sem).wait()
        _row_copy(y_hbm, 0, buf_ref.at[1], r, sem).wait()
        return carry

    lax.fori_loop(0, tm, issue, 0)
    lax.fori_loop(0, tm, wait, 0)
    w = w_ref[...]
    o_ref[...] = x_ref[...] + w[:, 0:1] * buf_ref[0] + w[:, 1:2] * buf_ref[1]


def _moe_combine(x, y_sorted, pos_a, pos_b, weights, tm=MOE_TILE):
    n, d = x.shape
    idx_spec = pl.BlockSpec((1, 1, tm), lambda i: (i, 0, 0), memory_space=pltpu.SMEM)
    return pl.pallas_call(
        _moe_combine_kernel,
        grid=(n // tm,),
        in_specs=[idx_spec, idx_spec, pl.BlockSpec(memory_space=pl.ANY),
                  pl.BlockSpec((tm, d), lambda i: (i, 0)), pl.BlockSpec((tm, 128), lambda i: (i, 0))],
        out_specs=pl.BlockSpec((tm, d), lambda i: (i, 0)),
        out_shape=jax.ShapeDtypeStruct((n, d), F32),
        scratch_shapes=[pltpu.VMEM((2, tm, d), F32), pltpu.SemaphoreType.DMA(())],
        compiler_params=_params(1),
        name="moe_combine",
    )(pos_a.reshape(n // tm, 1, tm), pos_b.reshape(n // tm, 1, tm), y_sorted, x, weights)


def _moe(x, norm_g, router, w1, w3, w2):
    n, d = x.shape
    tm = min(MOE_TILE, n)
    h2, comb, assign, count = _rmsnorm_router(x, norm_g, router)
    assigned = assign[:, :N_EXPERTS] > 0.5
    count = count[:, :N_EXPERTS].astype(jnp.int32)
    total = count[-1]
    padded = (total + tm - 1) // tm * tm
    ends = jnp.cumsum(padded)
    starts = ends - padded
    dest = starts[None, :] + count - 1
    rows = (2 * n // tm + N_EXPERTS) * tm
    tile_start = jnp.arange(rows // tm, dtype=jnp.int32) * tm
    tile_expert = jnp.minimum(jnp.sum(tile_start[:, None] >= ends[None, :], axis=1), N_EXPERTS - 1).astype(jnp.int32)
    n_active = (ends[-1:] // tm).astype(jnp.int32)
    offset = tile_start[:, None] + jnp.arange(tm, dtype=jnp.int32)[None, :] - starts[tile_expert][:, None]
    src = jnp.sum(count.T[tile_expert][:, None, :] <= offset[:, :, None], axis=-1, dtype=jnp.int32)
    src = jnp.minimum(src, n - 1).reshape(rows)
    first = jnp.argmax(assigned, axis=1)
    second = N_EXPERTS - 1 - jnp.argmax(assigned[:, ::-1], axis=1)
    take = lambda a, i: jnp.take_along_axis(a, i[:, None], axis=1)[:, 0]
    pos_a, pos_b = take(dest, first), take(dest, second)
    weights = jnp.pad(jnp.stack([take(comb, first), take(comb, second)], axis=1), ((0, 0), (0, 126)))

    x_sorted = _gather_rows(h2, src, tm)
    y_sorted = _moe_experts(x_sorted, tile_expert, n_active, w1, w3, w2, tm)
    return _moe_combine(x, y_sorted, pos_a, pos_b, weights, tm)


def _matmul_kernel(*refs, nk, has_res):
    a_ref, w_ref = refs[0], refs[1]
    res_ref = refs[2] if has_res else None
    o_ref, acc_ref = refs[-2], refs[-1]
    k = pl.program_id(2)
    part = jnp.dot(a_ref[...], w_ref[...].astype(BF16), preferred_element_type=F32)

    @pl.when(k == 0)
    def _():
        acc_ref[...] = part

    @pl.when(k > 0)
    def _():
        acc_ref[...] += part

    @pl.when(k == nk - 1)
    def _():
        out = acc_ref[...]
        if has_res:
            out = out + res_ref[...]
        o_ref[...] = out.astype(o_ref.dtype)


def _matmul(a, w, *, tm, tn, tk=None, out_dtype=F32, res=None, name="matmul"):
    m, kdim = a.shape
    n = w.shape[1]
    tk = kdim if tk is None else tk
    tm, tn = min(tm, m), min(tn, n)
    nk = kdim // tk
    assert m % tm == 0 and n % tn == 0 and kdim % tk == 0
    in_specs = [pl.BlockSpec((tm, tk), lambda j, i, k: (i, k)), pl.BlockSpec((tk, tn), lambda j, i, k: (k, j))]
    args = [a, w]
    if res is not None:
        in_specs.append(pl.BlockSpec((tm, tn), lambda j, i, k: (i, j)))
        args.append(res)
    return pl.pallas_call(
        functools.partial(_matmul_kernel, nk=nk, has_res=res is not None),
        grid=(n // tn, m // tm, nk),
        in_specs=in_specs,
        out_specs=pl.BlockSpec((tm, tn), lambda j, i, k: (i, j)),
        out_shape=jax.ShapeDtypeStruct((m, n), out_dtype),
        scratch_shapes=[pltpu.VMEM((tm, tn), F32)],
        compiler_params=_params(3),
        name=name,
    )(*args)


def _swiglu_up_kernel(h_ref, w1_ref, w3_ref, o_ref):
    h = h_ref[...]
    a = jnp.dot(h, w1_ref[...].astype(BF16), preferred_element_type=F32)
    b = jnp.dot(h, w3_ref[...].astype(BF16), preferred_element_type=F32)
    o_ref[...] = (a * jax.nn.sigmoid(a) * b).astype(o_ref.dtype)


def _swiglu_up(h, w1, w3, tm=1024, tn=512):
    m, kdim = h.shape
    n = w1.shape[1]
    tm, tn = min(tm, m), min(tn, n)
    assert m % tm == 0 and n % tn == 0
    return pl.pallas_call(
        _swiglu_up_kernel,
        grid=(n // tn, m // tm),
        in_specs=[pl.BlockSpec((tm, kdim), lambda j, i: (i, 0)),
                  pl.BlockSpec((kdim, tn), lambda j, i: (0, j)),
                  pl.BlockSpec((kdim, tn), lambda j, i: (0, j))],
        out_specs=pl.BlockSpec((tm, tn), lambda j, i: (i, j)),
        out_shape=jax.ShapeDtypeStruct((m, n), BF16),
        compiler_params=_params(2),
        name="swiglu_up",
    )(h, w1, w3)


def _shift_rows(x, carry_row):
    rolled = pltpu.roll(x, 1, axis=0)
    row = lax.broadcasted_iota(jnp.int32, x.shape, 0)
    return jnp.where(row == 0, carry_row, rolled)


def _rwkv_prep_kernel(*refs, has_vres):
    (u_ref, mu_ref, w0_ref, w2_ref, a0_ref, a2_ref, g2_ref, kkw_ref, kaw_ref) = refs[:9]
    pos = 9
    if has_vres:
        vd_ref, vmu_ref, v0_ref, vup_ref, vfirst_ref = refs[pos:pos + 5]
        pos += 5
    r_ref, lw_ref, k_ref, v_ref, kk_ref, b_ref, g_ref = refs[pos:pos + 7]
    pos += 7
    cu_ref = refs[pos]
    cv_ref = refs[pos + 1] if has_vres else None
    tt = u_ref.shape[0]

    @pl.when(pl.program_id(1) == 0)
    def _():
        cu_ref[...] = jnp.zeros_like(cu_ref)
        if has_vres:
            cv_ref[...] = jnp.zeros_like(cv_ref)

    u = u_ref[...]
    prev = _shift_rows(u, cu_ref[0:1, :])
    cu_ref[0:1, :] = u[tt - 1:tt, :]
    uf = u + (prev - u) * mu_ref[...]
    r, k, v = uf[:, 0:512], uf[:, 512:1024], uf[:, 1024:1536]
    wd, ad, gd = uf[:, 1536:1664], uf[:, 1664:1792], uf[:, 1792:2048]

    x = w0_ref[...] + _mm(jnp.tanh(wd), w2_ref[...], 2, 2)
    softplus = jnp.maximum(-x, 0.0) + jnp.log(1.0 + jnp.exp(-jnp.abs(x)))
    lw_ref[...] = -jnp.exp(-softplus - 0.5)
    a = jax.nn.sigmoid(a0_ref[...] + _mm(ad, a2_ref[...], 2, 2))
    g_ref[...] = _mm(jax.nn.sigmoid(gd), g2_ref[...], 2, 2)
    if has_vres:
        vd = vd_ref[...]
        vprev = _shift_rows(vd, cv_ref[0:1, :])
        cv_ref[0:1, :] = vd[tt - 1:tt, :]
        vdf = vd + (vprev - vd) * vmu_ref[...]
        v = v + (vfirst_ref[...] - v) * jax.nn.sigmoid(v0_ref[...] + _mm(vdf, vup_ref[...], 2, 2))
    kk = k * kkw_ref[...]
    ss = _group_mean(kk * kk, HEAD) * HEAD
    kk = kk * lax.rsqrt(jnp.maximum(ss, 1e-24))
    r_ref[...] = r
    k_ref[...] = k * (1.0 + (a - 1.0) * kaw_ref[...])
    v_ref[...] = v
    kk_ref[...] = kk
    b_ref[...] = kk * a


def _rwkv_prep(p_small, bsz, seq, mu, w0, w2, a0, a2, g2, kkw, kaw, vres, tt=256):
    tt = min(tt, seq)
    nt = seq // tt
    n = bsz * seq
    row = lambda b, t: (b * nt + t, 0)
    const = lambda b, t: (0, 0)
    pad_rows = lambda w: jnp.pad(w, ((0, 128 - w.shape[0]), (0, 0)))
    mu_p = jnp.concatenate([mu[:1536], jnp.pad(mu[1536:1632], (0, 32)), jnp.pad(mu[1632:1728], (0, 32)), mu[1728:]])
    vec = lambda a: a.reshape(1, -1)
    args = [p_small, vec(mu_p), vec(w0), pad_rows(w2), vec(a0), pad_rows(a2), g2, vec(kkw), vec(kaw)]
    in_specs = [pl.BlockSpec((tt, 2048), row), pl.BlockSpec((1, 2048), const), pl.BlockSpec((1, 512), const),
                pl.BlockSpec((128, 512), const), pl.BlockSpec((1, 512), const), pl.BlockSpec((128, 512), const),
                pl.BlockSpec((256, 512), const), pl.BlockSpec((1, 512), const), pl.BlockSpec((1, 512), const)]
    scratch = [pltpu.VMEM((8, 2048), F32)]
    if vres is not None:
        vmu, v0, vup, vfirst = vres
        args += [p_small, vec(jnp.pad(vmu, (0, 64))), vec(v0), pad_rows(vup), vfirst]
        in_specs += [pl.BlockSpec((tt, 128), lambda b, t: (b * nt + t, C_VRES // 128)), pl.BlockSpec((1, 128), const),
                     pl.BlockSpec((1, 512), const), pl.BlockSpec((128, 512), const), pl.BlockSpec((tt, 512), row)]
        scratch.append(pltpu.VMEM((8, 128), F32))
    return pl.pallas_call(
        functools.partial(_rwkv_prep_kernel, has_vres=vres is not None),
        grid=(bsz, nt),
        in_specs=in_specs,
        out_specs=[pl.BlockSpec((tt, 512), row)] * 7,
        out_shape=[jax.ShapeDtypeStruct((n, 512), F32)] * 7,
        scratch_shapes=scratch,
        compiler_params=_params(2),
        name="rwkv_prep",
    )(*args)


def _rwkv_chunk_kernel(r_ref, lw_ref, k_ref, v_ref, kk_ref, b_ref, g_ref, rk_ref, lnw_ref, lnb_ref, o_ref, s_ref,
                       *, pw):
    nb, c, _ = r_ref.shape

    @pl.when(pl.program_id(0) == 0)
    def _():
        s_ref[...] = jnp.zeros_like(s_ref)

    row = lax.broadcasted_iota(jnp.int32, (c, c), 0)
    col = lax.broadcasted_iota(jnp.int32, (c, c), 1)
    tril_incl = jnp.where(row >= col, 1.0, 0.0).astype(BF16)
    strict = row > col
    incl = row >= col
    eye = jnp.where(row == col, 1.0, 0.0)
    hrow = lax.broadcasted_iota(jnp.int32, (HEAD, HEAD), 0)
    hcol = lax.broadcasted_iota(jnp.int32, (HEAD, HEAD), 1)
    eye_h = jnp.where(hrow == hcol, 1.0, 0.0).astype(BF16)
    mm = functools.partial(_mm, pa=pw, pb=pw)
    transpose = lambda a, pieces: _mm(eye_h, a, pb=pieces, nt=True)

    heads = [(bi, h) for bi in range(nb) for h in range(RWKV_HEADS)]
    per_head = lambda full: [full[bi][:, h * HEAD:(h + 1) * HEAD] for bi, h in heads]
    each = lambda f, *lists: [f(*vals) for vals in zip(*lists)]

    r, lw, k, v, kk, b = ([ref[bi] for bi in range(nb)] for ref in (r_ref, lw_ref, k_ref, v_ref, kk_ref, b_ref))
    cum = [_mm(tril_incl, x, pb=3) for x in lw]
    g_inv = [jnp.exp(-x) for x in cum]
    a_t = per_head(each(lambda kk_, c_, lw_: kk_ * jnp.exp(c_ - lw_), kk, cum, lw))
    b_t = per_head(each(lambda b_, gi: b_ * gi, b, g_inv))
    k_t = per_head(each(lambda k_, gi: k_ * gi, k, g_inv))
    r_t = per_head(each(lambda r_, c_: r_ * jnp.exp(c_), r, cum))
    vh = per_head(v)
    tail = [jnp.exp(x[c - 1:c, :] - x) for x in cum]
    bh = per_head(each(lambda b_, t_: b_ * t_, b, tail))
    kh = per_head(each(lambda k_, t_: k_ * t_, k, tail))
    g_last = per_head([jnp.exp(x[c - 1:c, :]) for x in cum])

    a_ab = each(lambda x, y: jnp.where(strict, mm(x, y, nt=True), 0.0), a_t, b_t)
    a_ak = each(lambda x, y: jnp.where(strict, mm(x, y, nt=True), 0.0), a_t, k_t)
    a_rb = each(lambda x, y: jnp.where(incl, mm(x, y, nt=True), 0.0), r_t, b_t)
    a_rk = each(lambda x, y: jnp.where(incl, mm(x, y, nt=True), 0.0), r_t, k_t)
    x = [eye - n for n in a_ab]
    p = each(mm, a_ab, a_ab)
    steps = int(np.log2(c)) - 1
    for i in range(steps):
        x = each(lambda x_, p_: x_ + mm(x_, p_), x, p)
        if i + 1 < steps:
            p = each(mm, p, p)
    w1 = each(mm, x, a_t)
    akv = each(mm, a_ak, vh)
    w2 = each(mm, x, akv)
    y_in = each(lambda ark, v_, arb, w2_: mm(ark, v_) - mm(arb, w2_), a_rk, vh, a_rb, w2)
    q_h = each(lambda r_, arb, w1_: r_ - mm(arb, w1_), r_t, a_rb, w1)
    bh_t = [transpose(x_, pw) for x_ in bh]
    kh_t = [transpose(x_, pw) for x_ in kh]
    decay = [transpose(jnp.broadcast_to(x_, (HEAD, HEAD)), 3) for x_ in g_last]
    g_s = each(lambda kt_, v_, bt_, w2_: mm(kt_, v_) - mm(bt_, w2_), kh_t, vh, bh_t, w2)
    s = [s_ref[i] for i in range(len(heads))]
    ys = each(lambda yi, q_, s_: yi + mm(q_, s_), y_in, q_h, s)
    w1s = each(mm, w1, s)
    for i, (d_, s_, bt_, ws_, gs_) in enumerate(zip(decay, s, bh_t, w1s, g_s)):
        s_ref[i] = d_ * s_ - mm(bt_, ws_) + gs_

    for bi in range(nb):
        y = jnp.concatenate(ys[bi * RWKV_HEADS:(bi + 1) * RWKV_HEADS], axis=1)
        mean = _group_mean(y, HEAD)
        d = y - mean
        var = _group_mean(d * d, HEAD)
        yn = d * lax.rsqrt(var + RWKV_GN_EPS) * lnw_ref[...] + lnb_ref[...]
        bonus = _group_mean(r[bi] * k[bi] * rk_ref[...], HEAD) * HEAD * v[bi]
        o_ref[bi] = ((yn + bonus) * g_ref[bi]).astype(o_ref.dtype)


def _rwkv_recurrence(r, lw, k, v, kk, b, g, rk, lnw, lnb, bsz, seq, pw=1):
    c = min(RWKV_CHUNK, seq)
    blk = pl.BlockSpec((bsz, c, BRANCH_W), lambda t: (0, t, 0))
    cst = pl.BlockSpec((1, BRANCH_W), lambda t: (0, 0))
    as3 = lambda a: a.reshape(bsz, seq, BRANCH_W)
    out = pl.pallas_call(
        functools.partial(_rwkv_chunk_kernel, pw=pw),
        grid=(seq // c,),
        in_specs=[blk] * 7 + [cst] * 3,
        out_specs=blk,
        out_shape=jax.ShapeDtypeStruct((bsz, seq, BRANCH_W), BF16),
        scratch_shapes=[pltpu.VMEM((bsz * RWKV_HEADS, HEAD, HEAD), F32)],
        compiler_params=_params(1),
        name="rwkv_chunk",
    )(as3(r), as3(lw), as3(k), as3(v), as3(kk), as3(b), as3(g), rk.reshape(1, -1), lnw.reshape(1, -1),
      lnb.reshape(1, -1))
    return out.reshape(bsz * seq, BRANCH_W)


def _eye(n):
    r = lax.broadcasted_iota(jnp.int32, (n, n), 0)
    c = lax.broadcasted_iota(jnp.int32, (n, n), 1)
    return jnp.where(r == c, 1.0, 0.0).astype(BF16)


def _transpose_bf16(x):
    return _dg(_eye(x.shape[1]), x.astype(BF16), nt=True).astype(BF16)


def _nsa_norm_kernel(q_ref, ks_ref, kw_ref, vs_ref, vw_ref, gq_ref, gs_ref, gw_ref,
                     qo_ref, kso_ref, kwo_ref, vso_ref, vwo_ref):
    for x_ref, g_ref, o_ref in ((q_ref, gq_ref, qo_ref), (ks_ref, gs_ref, kso_ref), (kw_ref, gw_ref, kwo_ref)):
        x = x_ref[...]
        ms = _group_mean(x * x, HEAD)
        o_ref[...] = (x * lax.rsqrt(ms + NORM_EPS) * g_ref[...]).astype(o_ref.dtype)
    vso_ref[...] = _transpose_bf16(vs_ref[...])
    vwo_ref[...] = _transpose_bf16(vw_ref[...])


def _nsa_norm(p_small, qk_gain, bsz, seq, tt=512):
    n = p_small.shape[0]
    tt = min(tt, seq)
    nt = seq // tt
    col = lambda c, w: pl.BlockSpec((tt, w), lambda i: (i, c // w))
    cst = lambda w: pl.BlockSpec((1, w), lambda i: (0, 0))
    out = lambda w: pl.BlockSpec((tt, w), lambda i: (i, 0))
    out_t = pl.BlockSpec((128, tt), lambda i: (i // nt, i % nt))
    gq = (jnp.tile(qk_gain[0], 8) * HEAD ** -0.5).reshape(1, 512)
    gs = jnp.tile(qk_gain[2], 2).reshape(1, 128)
    gw = jnp.tile(qk_gain[3], 2).reshape(1, 128)
    return pl.pallas_call(
        _nsa_norm_kernel,
        grid=(n // tt,),
        in_specs=[col(C_NSAQ, 512), col(C_KS, 128), col(C_KW, 128), col(C_VS, 128), col(C_VW, 128),
                  cst(512), cst(128), cst(128)],
        out_specs=[out(512), out(128), out(128), out_t, out_t],
        out_shape=[jax.ShapeDtypeStruct((n, 512), BF16)] + [jax.ShapeDtypeStruct((n, 128), BF16)] * 2
        + [jax.ShapeDtypeStruct((bsz * 128, seq), BF16)] * 2,
        compiler_params=_params(1),
        name="nsa_norm",
    )(p_small, p_small, p_small, p_small, p_small, gq, gs, gw)


def _gelu_tanh(x):
    return 0.5 * x * (1.0 + jnp.tanh(0.7978845608028654 * (x + 0.044715 * x * x * x)))


def _nsa_compress_kernel(x_ref, pos_ref, w1_ref, b1_ref, w2_ref, gain_ref, o_ref):
    half = x_ref.shape[-1]
    x = x_ref[0, 0, 0].astype(BF16)
    w1 = w1_ref[0].astype(BF16)
    z_lo = jnp.dot(x, w1[:half], preferred_element_type=F32)
    z_hi = jnp.dot(x, w1[half:], preferred_element_type=F32)
    nrow = z_hi.shape[0]
    pos = jnp.broadcast_to(pos_ref[0], (8, 2 * half))
    const = _mm(pos, w1, pa=2)[0:1] + b1_ref[0]
    pre = z_lo + pltpu.roll(z_hi, nrow - 1, axis=0) + const
    out = jnp.dot(_gelu_tanh(pre).astype(BF16), w2_ref[0].astype(BF16), preferred_element_type=F32)
    is_key = pl.program_id(0) == 0
    normed = out * lax.rsqrt(jnp.mean(out * out, axis=-1, keepdims=True) + NORM_EPS) * gain_ref[...]
    o_ref[0, 0, 0] = jnp.where(is_key, normed, out)


def _nsa_compress(kc_vc, cmp_pos, cmp_w1, cmp_b1, cmp_w2, gain):
    _, bsz, hkv, nc, wid = kc_vc.shape
    return pl.pallas_call(
        _nsa_compress_kernel,
        grid=(2, bsz, hkv),
        in_specs=[pl.BlockSpec((1, 1, 1, nc, wid), lambda i, b, h: (i, b, h, 0, 0)),
                  pl.BlockSpec((1, 1, 2 * wid), lambda i, b, h: (i, 0, 0)),
                  pl.BlockSpec((1, 2 * wid, 2 * HEAD), lambda i, b, h: (i, 0, 0)),
                  pl.BlockSpec((1, 1, 2 * HEAD), lambda i, b, h: (i, 0, 0)),
                  pl.BlockSpec((1, 2 * HEAD, HEAD), lambda i, b, h: (i, 0, 0)),
                  pl.BlockSpec((1, HEAD), lambda i, b, h: (0, 0))],
        out_specs=pl.BlockSpec((1, 1, 1, nc, HEAD), lambda i, b, h: (i, b, h, 0, 0)),
        out_shape=jax.ShapeDtypeStruct((2, bsz, hkv, nc, HEAD), F32),
        compiler_params=_params(3),
        name="nsa_compress",
    )(kc_vc, cmp_pos.reshape(2, 1, 2 * wid), cmp_w1, cmp_b1.reshape(2, 1, 2 * HEAD), cmp_w2, gain.reshape(1, HEAD))


def _nsa_cmp_kernel(q_ref, kc_ref, vc_ref, ov_ref, o_ref, sel_ref, *, tq):
    hkv, ncmp = kc_ref.shape[1], kc_ref.shape[2]
    nsel = ov_ref.shape[0]
    g = NSA_GROUP
    rows = g * tq
    start = pl.program_id(1) * tq
    q_all = q_ref[...]
    t_pos = start + (lax.broadcasted_iota(jnp.int32, (ncmp, rows), 1) & (tq - 1))
    c_end = lax.broadcasted_iota(jnp.int32, (ncmp, rows), 0) * COMP_STRIDE + (COMP_L - 1)
    mask = c_end <= t_pos
    sid = lax.broadcasted_iota(jnp.int32, (nsel, tq), 0)
    cur = (start + lax.broadcasted_iota(jnp.int32, (nsel, tq), 1)) // SEL_L
    forced = (sid == 0) | (sid == cur) | (sid == cur - 1)
    k_top = min(SEL_N, nsel)
    outs, sels = [], []
    for h in range(hkv):
        q = _group_rows(q_all, h, g)
        s = jnp.where(mask, _dg(kc_ref[0, h].astype(BF16), q, nt=True), -1e30)
        e = jnp.exp(s - jnp.max(s, axis=0, keepdims=True))
        p = jnp.where(mask, e / jnp.sum(e, axis=0, keepdims=True), 0.0)
        outs.append(jnp.dot(_transpose_bf16(vc_ref[0, h]), p.astype(BF16), preferred_element_type=F32))
        p_sum = p[:, 0:tq]
        for i in range(1, g):
            p_sum = p_sum + p[:, i * tq:(i + 1) * tq]
        imp = _mm(ov_ref[...], p_sum, pb=3)
        val = jnp.where(forced, 1e9, jnp.where(sid <= cur, imp, -1e9))
        rank = jnp.zeros(imp.shape, jnp.int32)
        for j in range(nsel):
            cj = val[j:j + 1, :]
            ahead = (cj > val) | ((cj == val) & (sid > j))
            rank = rank + ahead.astype(jnp.int32)
        sels.append(jnp.where((rank < k_top) & (val > -1e8), 1.0, 0.0))
    o_ref[...] = jnp.concatenate(outs, axis=0)
    sel_ref[...] = jnp.concatenate(sels, axis=0).astype(sel_ref.dtype)


def _group_rows(q_all, h, g):
    return jnp.concatenate([q_all[:, (h * g + i) * HEAD:(h * g + i + 1) * HEAD] for i in range(g)], axis=0)


def _nsa_cmp(q_n, k_cmp, v_cmp, overlap_t, bsz, seq, tq=128):
    hkv = k_cmp.shape[1]
    nsel, ncmp = overlap_t.shape
    nt = seq // tq
    rows = NSA_GROUP * tq
    return pl.pallas_call(
        functools.partial(_nsa_cmp_kernel, tq=tq),
        grid=(bsz, nt),
        in_specs=[pl.BlockSpec((tq, BRANCH_W), lambda b, t: (b * nt + t, 0)),
                  pl.BlockSpec((1, hkv, ncmp, HEAD), lambda b, t: (b, 0, 0, 0)),
                  pl.BlockSpec((1, hkv, ncmp, HEAD), lambda b, t: (b, 0, 0, 0)),
                  pl.BlockSpec((nsel, ncmp), lambda b, t: (0, 0))],
        out_specs=[pl.BlockSpec((hkv * HEAD, rows), lambda b, t: (b * nt + t, 0)),
                   pl.BlockSpec((hkv * nsel, tq), lambda b, t: (b * nt + t, 0))],
        out_shape=[jax.ShapeDtypeStruct((bsz * nt * hkv * HEAD, rows), F32),
                   jax.ShapeDtypeStruct((bsz * nt * hkv * nsel, tq), BF16)],
        compiler_params=_params(2),
        name="nsa_cmp",
    )(q_n, k_cmp, v_cmp, overlap_t)


SCORE_MASKED = -1e30
SCORE_FLOOR = -1e20


def _nsa_attn_kernel(q_ref, sel_ref, ks_ref, vs_ref, kw_ref, vw_ref, oc_ref, gl_ref, o_ref, *, tq, kt):
    g = NSA_GROUP
    hkv = ks_ref.shape[1] // HEAD
    seq = ks_ref.shape[0]
    nsel = sel_ref.shape[0] // hkv
    rows = g * tq
    start = pl.program_id(1) * tq
    q_all = q_ref[...]
    sel_all = sel_ref[...]
    qs = [_group_rows(q_all, h, g) for h in range(hkv)]
    sels = [sel_all[h * nsel:(h + 1) * nsel, :] for h in range(hkv)]
    head_cols = lambda x, h: x[:, h * HEAD:(h + 1) * HEAD]
    per_query = lambda x: jnp.concatenate([x] * g, axis=1)
    with_ones = lambda vt: jnp.concatenate([vt, jnp.ones((16, vt.shape[1]), BF16)], axis=0)

    key_pos = lax.broadcasted_iota(jnp.int32, (kt, tq), 0)
    t_pos = start + lax.broadcasted_iota(jnp.int32, (kt, tq), 1)
    blk_of_key = lax.broadcasted_iota(jnp.int32, (kt, nsel), 0) // SEL_L
    blk_id = lax.broadcasted_iota(jnp.int32, (kt, nsel), 1)

    def key_tile(j, carry, causal):
        base = pl.multiple_of(j * kt, kt)
        kb2 = ks_ref[pl.ds(base, kt), :]
        vt2 = vs_ref[:, pl.ds(base, kt)]
        expand = jnp.where(blk_of_key + j * (kt // SEL_L) == blk_id, 1.0, 0.0).astype(BF16)
        out = []
        for h in range(hkv):
            m, acc = carry[h]
            keep = jnp.dot(expand, sels[h], preferred_element_type=F32) > 0.5
            if causal:
                keep = keep & (key_pos + base <= t_pos)
            bias = per_query(jnp.where(keep, 0.0, SCORE_MASKED))
            s = _dg(head_cols(kb2, h), qs[h], nt=True) + bias
            m_new = jnp.maximum(m, jnp.max(s, axis=0, keepdims=True))
            p = jnp.exp(s - m_new).astype(BF16)
            vt = with_ones(vt2[h * HEAD:(h + 1) * HEAD, :])
            acc = jnp.exp(m - m_new) * acc + jnp.dot(vt, p, preferred_element_type=F32)
            out.append((m_new, acc))
        return tuple(out)

    init = tuple((jnp.full((1, rows), SCORE_FLOOR, F32), jnp.zeros((HEAD + 16, rows), F32)) for _ in range(hkv))
    n_full = start // kt
    carry = lax.fori_loop(0, n_full, lambda j, c: key_tile(j, c, False), init)
    carry = key_tile(n_full, carry, True)
    o_sel = [acc[0:HEAD] / acc[HEAD:HEAD + 1] for _, acc in carry]

    span = min(WINDOW + tq, seq)
    wbase = pl.multiple_of(jnp.maximum(start - WINDOW, 0), tq) if seq > span else 0
    kb2 = kw_ref[pl.ds(wbase, span), :]
    vt2 = vw_ref[:, pl.ds(wbase, span)]
    tw = start + lax.broadcasted_iota(jnp.int32, (span, tq), 1)
    wpos = wbase + lax.broadcasted_iota(jnp.int32, (span, tq), 0)
    wbias = per_query(jnp.where((wpos <= tw) & (wpos > tw - WINDOW), 0.0, SCORE_MASKED))
    o_win = []
    for h in range(hkv):
        s = _dg(head_cols(kb2, h), qs[h], nt=True) + wbias
        p = jnp.exp(s - jnp.max(s, axis=0, keepdims=True)).astype(BF16)
        acc = jnp.dot(with_ones(vt2[h * HEAD:(h + 1) * HEAD, :]), p, preferred_element_type=F32)
        o_win.append(acc[0:HEAD] / acc[HEAD:HEAD + 1])

    gate_t = _mm(_eye(128), jax.nn.sigmoid(gl_ref[...]), pb=3, nt=True)
    o_cmp = oc_ref[...]
    blocks = []
    for h in range(hkv):
        for i in range(g):
            cols = slice(i * tq, (i + 1) * tq)
            r = (h * g + i) * 3
            blocks.append(gate_t[r:r + 1] * o_cmp[h * HEAD:(h + 1) * HEAD, cols]
                          + gate_t[r + 1:r + 2] * o_sel[h][:, cols] + gate_t[r + 2:r + 3] * o_win[h][:, cols])
    y_t = jnp.concatenate(blocks, axis=0)
    o_ref[...] = _dg(_eye(tq), y_t.astype(BF16), nt=True).astype(o_ref.dtype)


def _nsa_attn(p_small, q_n, sel_t, ks, vs_t, kw, vw_t, o_cmp_t, bsz, seq, tq=128, kt=512):
    kt = min(kt, seq)
    nt = seq // tq
    hkv = NSA_KV_HEADS
    nsel = seq // SEL_L
    row = lambda w: pl.BlockSpec((tq, w), lambda b, t: (b * nt + t, 0))
    tile = lambda r, w: pl.BlockSpec((r, w), lambda b, t: (b * nt + t, 0))
    k_spec = pl.BlockSpec((seq, hkv * HEAD), lambda b, t: (b, 0))
    vt_spec = pl.BlockSpec((hkv * HEAD, seq), lambda b, t: (b, 0))
    return pl.pallas_call(
        functools.partial(_nsa_attn_kernel, tq=tq, kt=kt),
        grid=(bsz, nt),
        in_specs=[row(BRANCH_W), tile(hkv * nsel, tq), k_spec, vt_spec, k_spec, vt_spec,
                  tile(hkv * HEAD, NSA_GROUP * tq),
                  pl.BlockSpec((tq, 128), lambda b, t: (b * nt + t, C_GL // 128))],
        out_specs=row(BRANCH_W),
        out_shape=jax.ShapeDtypeStruct((bsz * seq, BRANCH_W), BF16),
        compiler_params=_params(2),
        name="nsa_attn",
    )(q_n, sel_t, ks, vs_t, kw, vw_t, o_cmp_t, p_small)


def _nsa(p_small, bsz, seq, qk_gain, cmp_pos, cmp_w1, cmp_b1, cmp_w2):
    hkv = NSA_KV_HEADS
    q_n, ks_n, kw_n, vs_t, vw_t = _nsa_norm(p_small, qk_gain, bsz, seq)
    ncmp = seq // COMP_STRIDE
    grouped = lambda c: p_small[:, c:c + 128].reshape(bsz, ncmp, COMP_STRIDE, hkv, HEAD).transpose(0, 3, 1, 2, 4)
    kc_vc = jnp.stack([grouped(C_KC), grouped(C_VC)]).reshape(2, bsz, hkv, ncmp, COMP_STRIDE * HEAD)
    cmp = _nsa_compress(kc_vc, cmp_pos, cmp_w1, cmp_b1, cmp_w2, qk_gain[1])
    nsel = seq // SEL_L
    c0 = np.arange(ncmp)[None, :] * COMP_STRIDE
    s0 = np.arange(nsel)[:, None] * SEL_L
    overlap_t = np.clip(np.minimum(c0 + COMP_L, s0 + SEL_L) - np.maximum(c0, s0), 0, None) / COMP_L
    o_cmp_t, sel_t = _nsa_cmp(q_n, cmp[0], cmp[1], jnp.asarray(overlap_t, BF16), bsz, seq)
    return _nsa_attn(p_small, q_n, sel_t, ks_n, vs_t, kw_n, vw_t, o_cmp_t, bsz, seq)


HALO = 16


def _conv_pool_kernel(bg_ref, cg_ref, xi_ref, pu_ref, cw_ref, conv_ref, pool_ref, cz_ref, cp_ref):
    tt = xi_ref.shape[0]
    ti = pl.program_id(1)

    @pl.when(ti == 0)
    def _():
        cz_ref[...] = jnp.zeros_like(cz_ref)
        cp_ref[...] = jnp.zeros_like(cp_ref)

    def history(x, carry_ref):
        ext = jnp.concatenate([carry_ref[...], x], axis=0)
        carry_ref[...] = x[tt - HALO:tt, :]
        return ext

    lag = lambda ext, s: pltpu.roll(ext, s, axis=0)
    body = lambda ext: ext[HALO:HALO + tt, :]

    z = cg_ref[...] * xi_ref[...]
    ze = history(z, cz_ref)
    cw = cw_ref[...]
    y = cw[2:3, :] * z + cw[1:2, :] * body(lag(ze, 1)) + cw[0:1, :] * body(lag(ze, 2))
    conv_ref[...] = (bg_ref[...] * y).astype(conv_ref.dtype)

    u = pu_ref[...]
    sums = [history(u, cp_ref)]
    for w in (1, 2, 4, 8):
        sums.append(sums[-1] + lag(sums[-1], w))
    count = (ti * tt + 1 + lax.broadcasted_iota(jnp.int32, (tt, 128), 0)).astype(F32)
    outs = []
    for gi, w in enumerate(POOL_WINDOWS):
        sl = slice(gi * 128, (gi + 1) * 128)
        outs.append(body(sums[gi + 1])[:, sl] / jnp.minimum(count, float(w)) - u[:, sl])
    pool_ref[...] = jnp.concatenate(outs, axis=1).astype(pool_ref.dtype)


def _conv_pool(p_small, bsz, seq, conv_w, tt=512):
    tt = min(tt, seq)
    nt = seq // tt
    n = bsz * seq
    col = lambda c: pl.BlockSpec((tt, 512), lambda b, t: (b * nt + t, c // 512))
    row = pl.BlockSpec((tt, 512), lambda b, t: (b * nt + t, 0))
    return pl.pallas_call(
        _conv_pool_kernel,
        grid=(bsz, nt),
        in_specs=[col(C_CONV), col(C_CONV + 512), col(C_CONV + 1024), col(C_POOL),
                  pl.BlockSpec((8, 512), lambda b, t: (0, 0))],
        out_specs=[row, row],
        out_shape=[jax.ShapeDtypeStruct((n, 512), BF16)] * 2,
        scratch_shapes=[pltpu.VMEM((HALO, 512), F32)] * 2,
        compiler_params=_params(2),
        name="conv_pool",
    )(p_small, p_small, p_small, p_small, jnp.pad(conv_w, ((0, 8 - conv_w.shape[0]), (0, 0))))


def _mem_attn_kernel(q_ref, kv_ref, gq_ref, gk_ref, o_ref, *, scale):
    outs = []
    for h in range(MEM_HEADS):
        sl = slice(h * MEM_HEAD, (h + 1) * MEM_HEAD)
        q = q_ref[:, sl]
        q = q * lax.rsqrt(jnp.mean(q * q, axis=-1, keepdims=True) + NORM_EPS) * gq_ref[...]
        k = kv_ref[0, :, sl]
        k = k * lax.rsqrt(jnp.mean(k * k, axis=-1, keepdims=True) + NORM_EPS) * gk_ref[...]
        v = kv_ref[0, :, BRANCH_W + h * MEM_HEAD:BRANCH_W + (h + 1) * MEM_HEAD]
        s = _dg(q.astype(BF16), k.astype(BF16), nt=True) * scale
        e = jnp.exp(s - jnp.max(s, axis=-1, keepdims=True))
        p = e / jnp.sum(e, axis=-1, keepdims=True)
        outs.append(jnp.dot(p.astype(BF16), v.astype(BF16), preferred_element_type=F32))
    o_ref[...] = jnp.concatenate(outs, axis=1).astype(o_ref.dtype)


def _mem_attn(p_small, kv, bsz, seq, qk_gain, tq=512):
    tq = min(tq, seq)
    nt = seq // tq
    mlen = kv.shape[1]
    return pl.pallas_call(
        functools.partial(_mem_attn_kernel, scale=MEM_HEAD ** -0.5),
        grid=(bsz, nt),
        in_specs=[pl.BlockSpec((tq, 512), lambda b, t: (b * nt + t, C_MEM // 512)),
                  pl.BlockSpec((1, mlen, 2 * BRANCH_W), lambda b, t: (b, 0, 0)),
                  pl.BlockSpec((1, MEM_HEAD), lambda b, t: (0, 0)),
                  pl.BlockSpec((1, MEM_HEAD), lambda b, t: (0, 0))],
        out_specs=pl.BlockSpec((tq, 512), lambda b, t: (b * nt + t, 0)),
        out_shape=jax.ShapeDtypeStruct((bsz * seq, 512), BF16),
        compiler_params=_params(2),
        name="mem_attn",
    )(p_small, kv, qk_gain[0].reshape(1, -1), qk_gain[1].reshape(1, -1))


def _merge_kernel(h_ref, *refs):
    wg_refs, y_refs = refs[0:5], refs[5:9]
    wb_ref, pooled_ref, wp_ref, ps_ref, o_ref = refs[9:]
    h = h_ref[...]
    gate = lambda i: jax.nn.sigmoid(jnp.dot(h, wg_refs[i][...], preferred_element_type=F32))
    z_pool = jnp.dot(pooled_ref[...], wp_ref[0].astype(BF16), preferred_element_type=F32) * ps_ref[...]
    acc = gate(4) * z_pool
    for i in range(4):
        acc = acc + gate(i) * jnp.dot(y_refs[i][...], wb_ref[i].astype(BF16), preferred_element_type=F32)
    o_ref[...] = acc.astype(o_ref.dtype)


def _merge(h, w_gate, ys, w_branch, pooled, pool_w, pool_scale, tm=512):
    n = h.shape[0]
    tm = min(tm, n)
    tn = 512
    nj = D_MODEL // tn
    gate_spec = lambda g: pl.BlockSpec((D_MODEL, tn), lambda j, i: (0, g * nj + j))
    y_spec = pl.BlockSpec((tm, BRANCH_W), lambda j, i: (i, 0))
    return pl.pallas_call(
        _merge_kernel,
        grid=(nj, n // tm),
        in_specs=[pl.BlockSpec((tm, D_MODEL), lambda j, i: (i, 0))] + [gate_spec(g) for g in range(5)]
        + [y_spec] * 4
        + [pl.BlockSpec((4, BRANCH_W, tn), lambda j, i: (0, 0, j)),
           pl.BlockSpec((tm, 128), lambda j, i: (i, j)),
           pl.BlockSpec((1, 128, tn), lambda j, i: (j, 0, 0)),
           pl.BlockSpec((1, tn), lambda j, i: (0, j))],
        out_specs=pl.BlockSpec((tm, tn), lambda j, i: (i, j)),
        out_shape=jax.ShapeDtypeStruct((n, D_MODEL), BF16),
        compiler_params=_params(2),
        name="merge",
    )(h, *([w_gate] * 5), *ys, w_branch, pooled, pool_w, pool_scale.reshape(1, -1))


def _pack_w_in(w, vres_w):
    pad = lambda a, width: jnp.pad(a, ((0, 0), (0, width - a.shape[1])))
    w = w.astype(BF16)
    nsa = w[:, 1984:3288]
    vres_cols = pad(vres_w.astype(BF16), 128) if vres_w is not None else jnp.zeros((D_MODEL, 128), BF16)
    small = jnp.concatenate([
        w[:, 0:1536], pad(w[:, 1536:1632], 128), pad(w[:, 1632:1728], 128), w[:, 1728:1984],
        nsa[:, 0:512], w[:, 3288:4824], w[:, 4824:5336], w[:, 5336:5848],
        nsa[:, 512:1280], pad(nsa[:, 1280:1304], 128), vres_cols], axis=1)
    return small, w[:, 5848:]


def kernel(x, mem, norm_mix, norm_ffn, norm_mem, w_in, rwkv_mu, rwkv_w0, rwkv_w2, rwkv_a0, rwkv_a2, rwkv_g2, rwkv_kk, rwkv_ka, rwkv_rk, rwkv_ln_w, rwkv_ln_b, vres_in, vres_mu, vres_v0, vres_up, nsa_qk_gain, nsa_cmp_pos, nsa_cmp_w1, nsa_cmp_b1, nsa_cmp_w2, conv_w, pool_w, pool_scale, mem_wkv, mem_qk_gain, w_branch, w_out, ffn_w1, ffn_w3, ffn_w2, moe_router, moe_w1, moe_w3, moe_w2):
    bsz, seq, d = x.shape
    n = bsz * seq
    depth = w_in.shape[0]
    mlen = mem.shape[1]
    xf = x.reshape(n, d)
    memf = mem.reshape(bsz * mlen, d)
    v_first = None
    for l in range(depth):
        h = _rmsnorm(xf, norm_mix[l])
        w_small, w_gate = _pack_w_in(w_in[l], vres_in[l - 1] if l > 0 else None)
        p_small = _matmul(h, w_small, tm=1024, tn=512, name="in_proj")
        vres = (vres_mu[l - 1], vres_v0[l - 1], vres_up[l - 1], v_first) if l > 0 else None
        r, lw, k, v, kk, b, g = _rwkv_prep(p_small, bsz, seq, rwkv_mu[l], rwkv_w0[l], rwkv_w2[l], rwkv_a0[l],
                                           rwkv_a2[l], rwkv_g2[l], rwkv_kk[l], rwkv_ka[l], vres)
        if l == 0:
            v_first = v
        y_rwkv = _rwkv_recurrence(r, lw, k, v, kk, b, g, rwkv_rk[l], rwkv_ln_w[l], rwkv_ln_b[l], bsz, seq)
        y_nsa = _nsa(p_small, bsz, seq, nsa_qk_gain[l], nsa_cmp_pos[l], nsa_cmp_w1[l], nsa_cmp_b1[l], nsa_cmp_w2[l])
        y_conv, pooled = _conv_pool(p_small, bsz, seq, conv_w[l])
        mem_n = _rmsnorm(memf, norm_mem[l])
        kv = _matmul(mem_n, mem_wkv[l], tm=512, tn=512, name="mem_kv").reshape(bsz, mlen, 2 * BRANCH_W)
        y_mem = _mem_attn(p_small, kv, bsz, seq, mem_qk_gain[l])
        merged = _merge(h, w_gate, (y_rwkv, y_nsa, y_conv, y_mem), w_branch[l], pooled, pool_w[l], pool_scale[l])
        xf = _matmul(merged, w_out[l], tm=1024, tn=512, res=xf, name="out_proj")

        if l % 2 == 0:
            h2 = _rmsnorm(xf, norm_ffn[l])
            e = l // 2
            act = _swiglu_up(h2, ffn_w1[e], ffn_w3[e])
            xf = _matmul(act, ffn_w2[e], tm=1024, tn=1024, tk=1408, res=xf, name="ffn_down")
        else:
            e = l // 2
            xf = _moe(xf, norm_ffn[l], moe_router[e], moe_w1[e], moe_w3[e], moe_w2[e])
    return xf.reshape(bsz, seq, d)
```

```python
import functools

import jax
import jax.numpy as jnp
import numpy as np
from jax import lax
from jax.experimental import pallas as pl
from jax.experimental.pallas import tpu as pltpu

F32 = jnp.float32
BF16 = jnp.bfloat16

D_MODEL = 2048
BRANCH_W = 512
HEAD = 64
RWKV_HEADS = 8
NSA_KV_HEADS = 2
NSA_GROUP = 4
COMP_L = 32
COMP_STRIDE = 16
SEL_L = 64
SEL_N = 16
WINDOW = 512
MEM_HEADS = 4
MEM_HEAD = 128
POOL_WINDOWS = (2, 4, 8, 16)
N_EXPERTS = 8
NORM_EPS = 1e-6
RWKV_GN_EPS = 64e-5
RWKV_CHUNK = 64

VMEM_LIMIT_BYTES = 56 * 1024 * 1024

C_RWKV, C_NSAQ, C_CONV, C_POOL, C_MEM = 0, 2048, 2560, 4096, 4608
C_KC, C_VC, C_KS, C_VS, C_KW, C_VW, C_GL, C_VRES = 5120, 5248, 5376, 5504, 5632, 5760, 5888, 6016
N_SMALL = 6144


def _params(n_axes):
    return pltpu.CompilerParams(dimension_semantics=("arbitrary",) * n_axes,
                                vmem_limit_bytes=VMEM_LIMIT_BYTES)


def _split(a, n):
    pieces, r = [], a
    for i in range(n):
        p = r.astype(BF16)
        pieces.append(p)
        if i + 1 < n:
            r = r - p.astype(F32)
    return pieces


def _dg(a, b, nt):
    dims = (((1,), (1,)), ((), ())) if nt else (((1,), (0,)), ((), ()))
    return lax.dot_general(a, b, dims, preferred_element_type=F32)


def _mm(a, b, pa=1, pb=1, nt=False):
    sa = _split(a, pa) if a.dtype != BF16 else [a]
    sb = _split(b, pb) if b.dtype != BF16 else [b]
    order = max(len(sa), len(sb))
    acc = None
    for i, x in enumerate(sa):
        for j, y in enumerate(sb):
            if i + j < order:
                t = _dg(x, y, nt)
                acc = t if acc is None else acc + t
    return acc


def _head_sum_matrix(width, head):
    r = lax.broadcasted_iota(jnp.int32, (width, width), 0) // head
    c = lax.broadcasted_iota(jnp.int32, (width, width), 1) // head
    return jnp.where(r == c, 1.0, 0.0).astype(BF16)


def _group_mean(x, head):
    hs = _head_sum_matrix(x.shape[-1], head)
    return _mm(x, hs, pa=3) * (1.0 / head)


def _rmsnorm_kernel(x_ref, g_ref, o_ref):
    x = x_ref[...]
    y = x * lax.rsqrt(jnp.mean(x * x, axis=-1, keepdims=True) + NORM_EPS)
    o_ref[...] = (y * g_ref[...]).astype(o_ref.dtype)


def _rmsnorm(x, g, tm=512):
    m, d = x.shape
    tm = min(tm, m)
    return pl.pallas_call(
        _rmsnorm_kernel,
        grid=(m // tm,),
        in_specs=[pl.BlockSpec((tm, d), lambda i: (i, 0)), pl.BlockSpec((1, d), lambda i: (0, 0))],
        out_specs=pl.BlockSpec((tm, d), lambda i: (i, 0)),
        out_shape=jax.ShapeDtypeStruct((m, d), BF16),
        compiler_params=_params(1),
        name="rmsnorm",
    )(x, g.reshape(1, d))


def _rmsnorm_router_kernel(x_ref, g_ref, wr_ref, o_ref, comb_ref, assign_ref, count_ref, carry_ref):
    tm = x_ref.shape[0]

    @pl.when(pl.program_id(0) == 0)
    def _():
        carry_ref[...] = jnp.zeros_like(carry_ref)

    x = x_ref[...]
    y = x * lax.rsqrt(jnp.mean(x * x, axis=-1, keepdims=True) + NORM_EPS) * g_ref[...]
    o_ref[...] = y
    logits = _mm(y, wr_ref[...], pa=3, pb=3)
    lane = lax.broadcasted_iota(jnp.int32, logits.shape, 1)
    neg = jnp.float32(-3e38)
    lg = jnp.where(lane < N_EXPERTS, logits, neg)
    m1 = jnp.max(lg, axis=-1, keepdims=True)
    i1 = jnp.min(jnp.where(lg == m1, lane, 1 << 20), axis=-1, keepdims=True)
    lg2 = jnp.where(lane == i1, neg, lg)
    m2 = jnp.max(lg2, axis=-1, keepdims=True)
    i2 = jnp.min(jnp.where(lg2 == m2, lane, 1 << 20), axis=-1, keepdims=True)
    e2 = jnp.exp(m2 - m1)
    w1 = 1.0 / (1.0 + e2)
    w2 = e2 / (1.0 + e2)
    comb_ref[...] = jnp.where(lane == i1, w1, 0.0) + jnp.where(lane == i2, w2, 0.0)
    assign = jnp.where((lane == i1) | (lane == i2), 1.0, 0.0)
    assign_ref[...] = assign
    row = lax.broadcasted_iota(jnp.int32, (tm, tm), 0)
    col = lax.broadcasted_iota(jnp.int32, (tm, tm), 1)
    tril = jnp.where(row >= col, 1.0, 0.0).astype(BF16)
    count = jnp.dot(tril, assign.astype(BF16), preferred_element_type=F32) + carry_ref[0:1, :]
    count_ref[...] = count
    carry_ref[0:1, :] = count[tm - 1:tm, :]


def _rmsnorm_router(x, g, router, tm=512):
    m, d = x.shape
    tm = min(tm, m)
    wr = jnp.pad(router, ((0, 0), (0, 128 - router.shape[1])))
    lanes = pl.BlockSpec((tm, 128), lambda i: (i, 0))
    return pl.pallas_call(
        _rmsnorm_router_kernel,
        grid=(m // tm,),
        in_specs=[pl.BlockSpec((tm, d), lambda i: (i, 0)), pl.BlockSpec((1, d), lambda i: (0, 0)),
                  pl.BlockSpec((d, 128), lambda i: (0, 0))],
        out_specs=[pl.BlockSpec((tm, d), lambda i: (i, 0)), lanes, lanes, lanes],
        out_shape=[jax.ShapeDtypeStruct((m, d), F32)] + [jax.ShapeDtypeStruct((m, 128), F32)] * 3,
        scratch_shapes=[pltpu.VMEM((8, 128), F32)],
        compiler_params=_params(1),
        name="rmsnorm_router",
    )(x, g.reshape(1, d), wr)


MOE_TILE = 256


def _row_copy(src_hbm, row, dst, slot, sem):
    return pltpu.make_async_copy(src_hbm.at[pl.ds(row, 1), :], dst.at[pl.ds(slot, 1), :], sem)


def _gather_rows_kernel(idx_ref, nxt_ref, src_hbm, o_ref, buf_ref, sem):
    tm = o_ref.shape[0]

    i = pl.program_id(0)
    slot = i % 2

    def issue(ids_ref, to_slot):
        def body(r, carry):
            _row_copy(src_hbm, ids_ref[0, 0, r], buf_ref.at[to_slot], r, sem.at[to_slot]).start()
            return carry
        lax.fori_loop(0, tm, body, 0)

    @pl.when(i == 0)
    def _():
        issue(idx_ref, 0)

    @pl.when(i + 1 < pl.num_programs(0))
    def _():
        issue(nxt_ref, 1 - slot)

    def wait(r, carry):
        _row_copy(src_hbm, 0, buf_ref.at[slot], r, sem.at[slot]).wait()
        return carry

    lax.fori_loop(0, tm, wait, 0)
    o_ref[...] = buf_ref[slot].astype(o_ref.dtype)


def _gather_rows(src, idx, tm=MOE_TILE):
    r = idx.shape[0]
    d = src.shape[1]
    n_tiles = r // tm
    ids = idx.reshape(n_tiles, 1, tm)
    return pl.pallas_call(
        _gather_rows_kernel,
        grid=(n_tiles,),
        in_specs=[pl.BlockSpec((1, 1, tm), lambda i: (i, 0, 0), memory_space=pltpu.SMEM),
                  pl.BlockSpec((1, 1, tm), lambda i: (jnp.minimum(i + 1, n_tiles - 1), 0, 0),
                               memory_space=pltpu.SMEM),
                  pl.BlockSpec(memory_space=pl.ANY)],
        out_specs=pl.BlockSpec((tm, d), lambda i: (i, 0)),
        out_shape=jax.ShapeDtypeStruct((r, d), BF16),
        scratch_shapes=[pltpu.VMEM((2, tm, d), F32), pltpu.SemaphoreType.DMA((2,))],
        compiler_params=_params(1),
        name="moe_gather",
    )(ids, ids, src)


def _expert_changed(te_ref, i):
    return (i == 0) | (te_ref[i] != te_ref[jnp.maximum(i - 1, 0)])


def _moe_up_kernel(te_ref, na_ref, x_ref, w1_ref, w3_ref, o_ref, w1c_ref, w3c_ref):
    i = pl.program_id(1)
    active = i < na_ref[0]

    @pl.when(active & _expert_changed(te_ref, i))
    def _():
        w1c_ref[...] = w1_ref[0].astype(BF16)
        w3c_ref[...] = w3_ref[0].astype(BF16)

    @pl.when(active)
    def _():
        x = x_ref[...]
        a = jnp.dot(x, w1c_ref[...], preferred_element_type=F32)
        b = jnp.dot(x, w3c_ref[...], preferred_element_type=F32)
        o_ref[...] = (a * jax.nn.sigmoid(a) * b).astype(o_ref.dtype)

    @pl.when(jnp.logical_not(active))
    def _():
        o_ref[...] = jnp.zeros_like(o_ref)


def _moe_down_kernel(te_ref, na_ref, a_ref, w_ref, o_ref, wc_ref):
    i = pl.program_id(1)
    active = i < na_ref[0]

    @pl.when(active & _expert_changed(te_ref, i))
    def _():
        wc_ref[...] = w_ref[0].astype(BF16)

    @pl.when(active)
    def _():
        o_ref[...] = jnp.dot(a_ref[...], wc_ref[...], preferred_element_type=F32)

    @pl.when(jnp.logical_not(active))
    def _():
        o_ref[...] = jnp.zeros_like(o_ref)


def _moe_experts(x_sorted, tile_expert, n_active, w1, w3, w2, tm=MOE_TILE):
    rows, d = x_sorted.shape
    ff = w1.shape[2]
    n_tiles = rows // tm
    tn_up, tn_down = ff // 2, d // 2
    once = pl.Buffered(1)
    act = pl.pallas_call(
        _moe_up_kernel,
        grid_spec=pltpu.PrefetchScalarGridSpec(
            num_scalar_prefetch=2,
            grid=(ff // tn_up, n_tiles),
            in_specs=[pl.BlockSpec((tm, d), lambda j, i, te, na: (i, 0)),
                      pl.BlockSpec((1, d, tn_up), lambda j, i, te, na: (te[i], 0, j), pipeline_mode=once),
                      pl.BlockSpec((1, d, tn_up), lambda j, i, te, na: (te[i], 0, j), pipeline_mode=once)],
            out_specs=pl.BlockSpec((tm, tn_up), lambda j, i, te, na: (i, j)),
            scratch_shapes=[pltpu.VMEM((d, tn_up), BF16)] * 2),
        out_shape=jax.ShapeDtypeStruct((rows, ff), BF16),
        compiler_params=_params(2),
        name="moe_up",
    )(tile_expert, n_active, x_sorted, w1, w3)
    return pl.pallas_call(
        _moe_down_kernel,
        grid_spec=pltpu.PrefetchScalarGridSpec(
            num_scalar_prefetch=2,
            grid=(d // tn_down, n_tiles),
            in_specs=[pl.BlockSpec((tm, ff), lambda j, i, te, na: (i, 0)),
                      pl.BlockSpec((1, ff, tn_down), lambda j, i, te, na: (te[i], 0, j), pipeline_mode=once)],
            out_specs=pl.BlockSpec((tm, tn_down), lambda j, i, te, na: (i, j)),
            scratch_shapes=[pltpu.VMEM((ff, tn_down), BF16)]),
        out_shape=jax.ShapeDtypeStruct((rows, d), F32),
        compiler_params=_params(2),
        name="moe_down",
    )(tile_expert, n_active, act, w2)


def _moe_combine_kernel(pa_ref, pb_ref, na_ref, nb_ref, y_hbm, x_ref, w_ref, o_ref, buf_ref, sem):
    tm = x_ref.shape[0]
    i = pl.program_id(0)
    slot = i % 2

    def issue(a_ref, b_ref, to_slot):
        def body(r, carry):
            _row_copy(y_hbm, a_ref[0, 0, r], buf_ref.at[to_slot, 0], r, sem.at[to_slot]).start()
            _row_copy(y_hbm, b_ref[0, 0, r], buf_ref.at[to_slot, 1], r, sem.at[to_slot]).start()
            return carry
        lax.fori_loop(0, tm, body, 0)

    @pl.when(i == 0)
    def _():
        issue(pa_ref, pb_ref, 0)

    @pl.when(i + 1 < pl.num_programs(0))
    def _():
        issue(na_ref, nb_ref, 1 - slot)

    def wait(r, carry):
        _row_copy(y_hbm, 0, buf_ref.at[slot, 0], r, sem.at[slot]).wait()
        _row_copy(y_hbm, 0, buf_ref.at[slot, 1], r, sem.at[slot]).wait()
        return carry

    lax.fori_loop(0, tm, wait, 0)
    w = w_ref[...]
    o_ref[...] = x_ref[...] + w[:, 0:1] * buf_ref[slot, 0] + w[:, 1:2] * buf_ref[slot, 1]


def _moe_combine(x, y_sorted, pos_a, pos_b, weights, tm=MOE_TILE):
    n, d = x.shape
    n_tiles = n // tm
    cur = pl.BlockSpec((1, 1, tm), lambda i: (i, 0, 0), memory_space=pltpu.SMEM)
    nxt = pl.BlockSpec((1, 1, tm), lambda i: (jnp.minimum(i + 1, n_tiles - 1), 0, 0), memory_space=pltpu.SMEM)
    pa, pb = pos_a.reshape(n_tiles, 1, tm), pos_b.reshape(n_tiles, 1, tm)
    return pl.pallas_call(
        _moe_combine_kernel,
        grid=(n_tiles,),
        in_specs=[cur, cur, nxt, nxt, pl.BlockSpec(memory_space=pl.ANY),
                  pl.BlockSpec((tm, d), lambda i: (i, 0)), pl.BlockSpec((tm, 128), lambda i: (i, 0))],
        out_specs=pl.BlockSpec((tm, d), lambda i: (i, 0)),
        out_shape=jax.ShapeDtypeStruct((n, d), F32),
        scratch_shapes=[pltpu.VMEM((2, 2, tm, d), F32), pltpu.SemaphoreType.DMA((2,))],
        compiler_params=_params(1),
        name="moe_combine",
    )(pa, pb, pa, pb, y_sorted, x, weights)


def _moe(x, norm_g, router, w1, w3, w2):
    n, d = x.shape
    tm = min(MOE_TILE, n)
    h2, comb, assign, count = _rmsnorm_router(x, norm_g, router)
    assigned = assign[:, :N_EXPERTS] > 0.5
    count = count[:, :N_EXPERTS].astype(jnp.int32)
    total = count[-1]
    padded = (total + tm - 1) // tm * tm
    ends = jnp.cumsum(padded)
    starts = ends - padded
    dest = starts[None, :] + count - 1
    rows = (2 * n // tm + N_EXPERTS) * tm
    tile_start = jnp.arange(rows // tm, dtype=jnp.int32) * tm
    tile_expert = jnp.minimum(jnp.sum(tile_start[:, None] >= ends[None, :], axis=1), N_EXPERTS - 1).astype(jnp.int32)
    n_active = (ends[-1:] // tm).astype(jnp.int32)
    offset = tile_start[:, None] + jnp.arange(tm, dtype=jnp.int32)[None, :] - starts[tile_expert][:, None]
    src = jnp.sum(count.T[tile_expert][:, None, :] <= offset[:, :, None], axis=-1, dtype=jnp.int32)
    src = jnp.minimum(src, n - 1).reshape(rows)
    first = jnp.argmax(assigned, axis=1)
    second = N_EXPERTS - 1 - jnp.argmax(assigned[:, ::-1], axis=1)
    take = lambda a, i: jnp.take_along_axis(a, i[:, None], axis=1)[:, 0]
    pos_a, pos_b = take(dest, first), take(dest, second)
    weights = jnp.pad(jnp.stack([take(comb, first), take(comb, second)], axis=1), ((0, 0), (0, 126)))

    x_sorted = _gather_rows(h2, src, tm)
    y_sorted = _moe_experts(x_sorted, tile_expert, n_active, w1, w3, w2, tm)
    return _moe_combine(x, y_sorted, pos_a, pos_b, weights, tm)


def _matmul_kernel(*refs, has_res, cached):
    a_ref, w_ref = refs[0], refs[1]
    res_ref = refs[2] if has_res else None
    o_ref = refs[3] if has_res else refs[2]
    if cached:
        wc_ref = refs[-1]

        @pl.when(pl.program_id(1) == 0)
        def _():
            wc_ref[...] = w_ref[...].astype(BF16)

        w = wc_ref[...]
    else:
        w = w_ref[...]
    out = jnp.dot(a_ref[...], w, preferred_element_type=F32)
    if has_res:
        out = out + res_ref[...]
    o_ref[...] = out.astype(o_ref.dtype)


def _matmul(a, w, *, tm, tn, out_dtype=F32, res=None, name="matmul"):
    m, kdim = a.shape
    n = w.shape[1]
    tm, tn = min(tm, m), min(tn, n)
    assert m % tm == 0 and n % tn == 0
    cached = w.dtype != BF16
    w_mode = dict(pipeline_mode=pl.Buffered(1)) if cached else {}
    in_specs = [pl.BlockSpec((tm, kdim), lambda j, i: (i, 0)), pl.BlockSpec((kdim, tn), lambda j, i: (0, j), **w_mode)]
    args = [a, w]
    if res is not None:
        in_specs.append(pl.BlockSpec((tm, tn), lambda j, i: (i, j)))
        args.append(res)
    return pl.pallas_call(
        functools.partial(_matmul_kernel, has_res=res is not None, cached=cached),
        grid=(n // tn, m // tm),
        in_specs=in_specs,
        out_specs=pl.BlockSpec((tm, tn), lambda j, i: (i, j)),
        out_shape=jax.ShapeDtypeStruct((m, n), out_dtype),
        scratch_shapes=[pltpu.VMEM((kdim, tn), BF16)] if cached else [],
        compiler_params=_params(2),
        name=name,
    )(*args)


def _swiglu_up_kernel(h_ref, w1_ref, w3_ref, o_ref):
    h = h_ref[...]
    a = jnp.dot(h, w1_ref[...].astype(BF16), preferred_element_type=F32)
    b = jnp.dot(h, w3_ref[...].astype(BF16), preferred_element_type=F32)
    o_ref[...] = (a * jax.nn.sigmoid(a) * b).astype(o_ref.dtype)


def _swiglu_up(h, w1, w3, tm=1024, tn=512):
    m, kdim = h.shape
    n = w1.shape[1]
    tm, tn = min(tm, m), min(tn, n)
    assert m % tm == 0 and n % tn == 0
    return pl.pallas_call(
        _swiglu_up_kernel,
        grid=(n // tn, m // tm),
        in_specs=[pl.BlockSpec((tm, kdim), lambda j, i: (i, 0)),
                  pl.BlockSpec((kdim, tn), lambda j, i: (0, j)),
                  pl.BlockSpec((kdim, tn), lambda j, i: (0, j))],
        out_specs=pl.BlockSpec((tm, tn), lambda j, i: (i, j)),
        out_shape=jax.ShapeDtypeStruct((m, n), BF16),
        compiler_params=_params(2),
        name="swiglu_up",
    )(h, w1, w3)


def _shift_rows(x, carry_row):
    rolled = pltpu.roll(x, 1, axis=0)
    row = lax.broadcasted_iota(jnp.int32, x.shape, 0)
    return jnp.where(row == 0, carry_row, rolled)


def _rwkv_prep_kernel(*refs, has_vres):
    (u_ref, mu_ref, w0_ref, w2_ref, a0_ref, a2_ref, g2_ref, kkw_ref, kaw_ref) = refs[:9]
    pos = 9
    if has_vres:
        vd_ref, vmu_ref, v0_ref, vup_ref, vfirst_ref = refs[pos:pos + 5]
        pos += 5
    r_ref, lw_ref, k_ref, v_ref, kk_ref, b_ref, g_ref = refs[pos:pos + 7]
    pos += 7
    cu_ref = refs[pos]
    cv_ref = refs[pos + 1] if has_vres else None
    tt = u_ref.shape[0]

    @pl.when(pl.program_id(1) == 0)
    def _():
        cu_ref[...] = jnp.zeros_like(cu_ref)
        if has_vres:
            cv_ref[...] = jnp.zeros_like(cv_ref)

    u = u_ref[...]
    prev = _shift_rows(u, cu_ref[0:1, :])
    cu_ref[0:1, :] = u[tt - 1:tt, :]
    uf = u + (prev - u) * mu_ref[...]
    r, k, v = uf[:, 0:512], uf[:, 512:1024], uf[:, 1024:1536]
    wd, ad, gd = uf[:, 1536:1664], uf[:, 1664:1792], uf[:, 1792:2048]

    x = w0_ref[...] + _mm(jnp.tanh(wd), w2_ref[...], 2, 2)
    softplus = jnp.maximum(-x, 0.0) + jnp.log(1.0 + jnp.exp(-jnp.abs(x)))
    lw_ref[...] = -jnp.exp(-softplus - 0.5)
    a = jax.nn.sigmoid(a0_ref[...] + _mm(ad, a2_ref[...], 2, 2))
    g_ref[...] = _mm(jax.nn.sigmoid(gd), g2_ref[...], 2, 2)
    if has_vres:
        vd = vd_ref[...]
        vprev = _shift_rows(vd, cv_ref[0:1, :])
        cv_ref[0:1, :] = vd[tt - 1:tt, :]
        vdf = vd + (vprev - vd) * vmu_ref[...]
        v = v + (vfirst_ref[...] - v) * jax.nn.sigmoid(v0_ref[...] + _mm(vdf, vup_ref[...], 2, 2))
    kk = k * kkw_ref[...]
    ss = _group_mean(kk * kk, HEAD) * HEAD
    kk = kk * lax.rsqrt(jnp.maximum(ss, 1e-24))
    r_ref[...] = r
    k_ref[...] = k * (1.0 + (a - 1.0) * kaw_ref[...])
    v_ref[...] = v
    kk_ref[...] = kk
    b_ref[...] = kk * a


def _rwkv_prep(p_small, bsz, seq, mu, w0, w2, a0, a2, g2, kkw, kaw, vres, tt=256):
    tt = min(tt, seq)
    nt = seq // tt
    n = bsz * seq
    row = lambda b, t: (b * nt + t, 0)
    const = lambda b, t: (0, 0)
    pad_rows = lambda w: jnp.pad(w, ((0, 128 - w.shape[0]), (0, 0)))
    mu_p = jnp.concatenate([mu[:1536], jnp.pad(mu[1536:1632], (0, 32)), jnp.pad(mu[1632:1728], (0, 32)), mu[1728:]])
    vec = lambda a: a.reshape(1, -1)
    args = [p_small, vec(mu_p), vec(w0), pad_rows(w2), vec(a0), pad_rows(a2), g2, vec(kkw), vec(kaw)]
    in_specs = [pl.BlockSpec((tt, 2048), row), pl.BlockSpec((1, 2048), const), pl.BlockSpec((1, 512), const),
                pl.BlockSpec((128, 512), const), pl.BlockSpec((1, 512), const), pl.BlockSpec((128, 512), const),
                pl.BlockSpec((256, 512), const), pl.BlockSpec((1, 512), const), pl.BlockSpec((1, 512), const)]
    scratch = [pltpu.VMEM((8, 2048), F32)]
    if vres is not None:
        vmu, v0, vup, vfirst = vres
        args += [p_small, vec(jnp.pad(vmu, (0, 64))), vec(v0), pad_rows(vup), vfirst]
        in_specs += [pl.BlockSpec((tt, 128), lambda b, t: (b * nt + t, C_VRES // 128)), pl.BlockSpec((1, 128), const),
                     pl.BlockSpec((1, 512), const), pl.BlockSpec((128, 512), const), pl.BlockSpec((tt, 512), row)]
        scratch.append(pltpu.VMEM((8, 128), F32))
    return pl.pallas_call(
        functools.partial(_rwkv_prep_kernel, has_vres=vres is not None),
        grid=(bsz, nt),
        in_specs=in_specs,
        out_specs=[pl.BlockSpec((tt, 512), row)] * 7,
        out_shape=[jax.ShapeDtypeStruct((n, 512), F32)] * 7,
        scratch_shapes=scratch,
        compiler_params=_params(2),
        name="rwkv_prep",
    )(*args)


def _rwkv_chunk_kernel(r_ref, lw_ref, k_ref, v_ref, kk_ref, b_ref, g_ref, rk_ref, lnw_ref, lnb_ref, o_ref, s_ref,
                       *, pw):
    nb, c, _ = r_ref.shape

    @pl.when(pl.program_id(0) == 0)
    def _():
        s_ref[...] = jnp.zeros_like(s_ref)

    row = lax.broadcasted_iota(jnp.int32, (c, c), 0)
    col = lax.broadcasted_iota(jnp.int32, (c, c), 1)
    tril_incl = jnp.where(row >= col, 1.0, 0.0).astype(BF16)
    strict = row > col
    incl = row >= col
    eye = jnp.where(row == col, 1.0, 0.0)
    hrow = lax.broadcasted_iota(jnp.int32, (HEAD, HEAD), 0)
    hcol = lax.broadcasted_iota(jnp.int32, (HEAD, HEAD), 1)
    eye_h = jnp.where(hrow == hcol, 1.0, 0.0).astype(BF16)
    mm = functools.partial(_mm, pa=pw, pb=pw)
    transpose = lambda a, pieces: _mm(eye_h, a, pb=pieces, nt=True)

    heads = [(bi, h) for bi in range(nb) for h in range(RWKV_HEADS)]
    per_head = lambda full: [full[bi][:, h * HEAD:(h + 1) * HEAD] for bi, h in heads]
    each = lambda f, *lists: [f(*vals) for vals in zip(*lists)]

    r, lw, k, v, kk, b = ([ref[bi] for bi in range(nb)] for ref in (r_ref, lw_ref, k_ref, v_ref, kk_ref, b_ref))
    cum = [_mm(tril_incl, x, pb=3) for x in lw]
    g_inv = [jnp.exp(-x) for x in cum]
    a_t = per_head(each(lambda kk_, c_, lw_: kk_ * jnp.exp(c_ - lw_), kk, cum, lw))
    b_t = per_head(each(lambda b_, gi: b_ * gi, b, g_inv))
    k_t = per_head(each(lambda k_, gi: k_ * gi, k, g_inv))
    r_t = per_head(each(lambda r_, c_: r_ * jnp.exp(c_), r, cum))
    vh = per_head(v)
    tail = [jnp.exp(x[c - 1:c, :] - x) for x in cum]
    bh = per_head(each(lambda b_, t_: b_ * t_, b, tail))
    kh = per_head(each(lambda k_, t_: k_ * t_, k, tail))
    g_last = per_head([jnp.exp(x[c - 1:c, :]) for x in cum])

    a_ab = each(lambda x, y: jnp.where(strict, mm(x, y, nt=True), 0.0), a_t, b_t)
    a_ak = each(lambda x, y: jnp.where(strict, mm(x, y, nt=True), 0.0), a_t, k_t)
    a_rb = each(lambda x, y: jnp.where(incl, mm(x, y, nt=True), 0.0), r_t, b_t)
    a_rk = each(lambda x, y: jnp.where(incl, mm(x, y, nt=True), 0.0), r_t, k_t)
    x = [eye - n for n in a_ab]
    p = each(mm, a_ab, a_ab)
    steps = int(np.log2(c)) - 1
    for i in range(steps):
        x = each(lambda x_, p_: x_ + mm(x_, p_), x, p)
        if i + 1 < steps:
            p = each(mm, p, p)
    w1 = each(mm, x, a_t)
    akv = each(mm, a_ak, vh)
    w2 = each(mm, x, akv)
    y_in = each(lambda ark, v_, arb, w2_: mm(ark, v_) - mm(arb, w2_), a_rk, vh, a_rb, w2)
    q_h = each(lambda r_, arb, w1_: r_ - mm(arb, w1_), r_t, a_rb, w1)
    bh_t = [transpose(x_, pw) for x_ in bh]
    kh_t = [transpose(x_, pw) for x_ in kh]
    decay = [transpose(jnp.broadcast_to(x_, (HEAD, HEAD)), 3) for x_ in g_last]
    g_s = each(lambda kt_, v_, bt_, w2_: mm(kt_, v_) - mm(bt_, w2_), kh_t, vh, bh_t, w2)
    s = [s_ref[i] for i in range(len(heads))]
    ys = each(lambda yi, q_, s_: yi + mm(q_, s_), y_in, q_h, s)
    w1s = each(mm, w1, s)
    for i, (d_, s_, bt_, ws_, gs_) in enumerate(zip(decay, s, bh_t, w1s, g_s)):
        s_ref[i] = d_ * s_ - mm(bt_, ws_) + gs_

    for bi in range(nb):
        y = jnp.concatenate(ys[bi * RWKV_HEADS:(bi + 1) * RWKV_HEADS], axis=1)
        mean = _group_mean(y, HEAD)
        d = y - mean
        var = _group_mean(d * d, HEAD)
        yn = d * lax.rsqrt(var + RWKV_GN_EPS) * lnw_ref[...] + lnb_ref[...]
        bonus = _group_mean(r[bi] * k[bi] * rk_ref[...], HEAD) * HEAD * v[bi]
        o_ref[bi] = ((yn + bonus) * g_ref[bi]).astype(o_ref.dtype)


def _rwkv_recurrence(r, lw, k, v, kk, b, g, rk, lnw, lnb, bsz, seq, pw=1):
    c = min(RWKV_CHUNK, seq)
    blk = pl.BlockSpec((bsz, c, BRANCH_W), lambda t: (0, t, 0))
    cst = pl.BlockSpec((1, BRANCH_W), lambda t: (0, 0))
    as3 = lambda a: a.reshape(bsz, seq, BRANCH_W)
    out = pl.pallas_call(
        functools.partial(_rwkv_chunk_kernel, pw=pw),
        grid=(seq // c,),
        in_specs=[blk] * 7 + [cst] * 3,
        out_specs=blk,
        out_shape=jax.ShapeDtypeStruct((bsz, seq, BRANCH_W), BF16),
        scratch_shapes=[pltpu.VMEM((bsz * RWKV_HEADS, HEAD, HEAD), F32)],
        compiler_params=_params(1),
        name="rwkv_chunk",
    )(as3(r), as3(lw), as3(k), as3(v), as3(kk), as3(b), as3(g), rk.reshape(1, -1), lnw.reshape(1, -1),
      lnb.reshape(1, -1))
    return out.reshape(bsz * seq, BRANCH_W)


def _eye(n):
    r = lax.broadcasted_iota(jnp.int32, (n, n), 0)
    c = lax.broadcasted_iota(jnp.int32, (n, n), 1)
    return jnp.where(r == c, 1.0, 0.0).astype(BF16)


def _transpose_bf16(x):
    return _dg(_eye(x.shape[1]), x.astype(BF16), nt=True).astype(BF16)


def _nsa_norm_kernel(q_ref, ks_ref, kw_ref, vs_ref, vw_ref, gq_ref, gs_ref, gw_ref,
                     qo_ref, kso_ref, kwo_ref, vso_ref, vwo_ref):
    for x_ref, g_ref, o_ref in ((q_ref, gq_ref, qo_ref), (ks_ref, gs_ref, kso_ref), (kw_ref, gw_ref, kwo_ref)):
        x = x_ref[...]
        ms = _group_mean(x * x, HEAD)
        o_ref[...] = (x * lax.rsqrt(ms + NORM_EPS) * g_ref[...]).astype(o_ref.dtype)
    vso_ref[...] = _transpose_bf16(vs_ref[...])
    vwo_ref[...] = _transpose_bf16(vw_ref[...])


def _nsa_norm(p_small, qk_gain, bsz, seq, tt=512):
    n = p_small.shape[0]
    tt = min(tt, seq)
    nt = seq // tt
    col = lambda c, w: pl.BlockSpec((tt, w), lambda i: (i, c // w))
    cst = lambda w: pl.BlockSpec((1, w), lambda i: (0, 0))
    out = lambda w: pl.BlockSpec((tt, w), lambda i: (i, 0))
    out_t = pl.BlockSpec((128, tt), lambda i: (i // nt, i % nt))
    gq = (jnp.tile(qk_gain[0], 8) * HEAD ** -0.5).reshape(1, 512)
    gs = jnp.tile(qk_gain[2], 2).reshape(1, 128)
    gw = jnp.tile(qk_gain[3], 2).reshape(1, 128)
    return pl.pallas_call(
        _nsa_norm_kernel,
        grid=(n // tt,),
        in_specs=[col(C_NSAQ, 512), col(C_KS, 128), col(C_KW, 128), col(C_VS, 128), col(C_VW, 128),
                  cst(512), cst(128), cst(128)],
        out_specs=[out(512), out(128), out(128), out_t, out_t],
        out_shape=[jax.ShapeDtypeStruct((n, 512), BF16)] + [jax.ShapeDtypeStruct((n, 128), BF16)] * 2
        + [jax.ShapeDtypeStruct((bsz * 128, seq), BF16)] * 2,
        compiler_params=_params(1),
        name="nsa_norm",
    )(p_small, p_small, p_small, p_small, p_small, gq, gs, gw)


def _gelu_tanh(x):
    return 0.5 * x * (1.0 + jnp.tanh(0.7978845608028654 * (x + 0.044715 * x * x * x)))


def _nsa_compress_kernel(x_ref, pos_ref, w1_ref, b1_ref, w2_ref, gain_ref, o_ref):
    half = x_ref.shape[-1]
    x = x_ref[0, 0, 0].astype(BF16)
    w1 = w1_ref[0].astype(BF16)
    z_lo = jnp.dot(x, w1[:half], preferred_element_type=F32)
    z_hi = jnp.dot(x, w1[half:], preferred_element_type=F32)
    nrow = z_hi.shape[0]
    pos = jnp.broadcast_to(pos_ref[0], (8, 2 * half))
    const = _mm(pos, w1, pa=2)[0:1] + b1_ref[0]
    pre = z_lo + pltpu.roll(z_hi, nrow - 1, axis=0) + const
    out = jnp.dot(_gelu_tanh(pre).astype(BF16), w2_ref[0].astype(BF16), preferred_element_type=F32)
    is_key = pl.program_id(0) == 0
    normed = out * lax.rsqrt(jnp.mean(out * out, axis=-1, keepdims=True) + NORM_EPS) * gain_ref[...]
    o_ref[0, 0, 0] = jnp.where(is_key, normed, out)


def _nsa_compress(kc_vc, cmp_pos, cmp_w1, cmp_b1, cmp_w2, gain):
    _, bsz, hkv, nc, wid = kc_vc.shape
    return pl.pallas_call(
        _nsa_compress_kernel,
        grid=(2, bsz, hkv),
        in_specs=[pl.BlockSpec((1, 1, 1, nc, wid), lambda i, b, h: (i, b, h, 0, 0)),
                  pl.BlockSpec((1, 1, 2 * wid), lambda i, b, h: (i, 0, 0)),
                  pl.BlockSpec((1, 2 * wid, 2 * HEAD), lambda i, b, h: (i, 0, 0)),
                  pl.BlockSpec((1, 1, 2 * HEAD), lambda i, b, h: (i, 0, 0)),
                  pl.BlockSpec((1, 2 * HEAD, HEAD), lambda i, b, h: (i, 0, 0)),
                  pl.BlockSpec((1, HEAD), lambda i, b, h: (0, 0))],
        out_specs=pl.BlockSpec((1, 1, 1, nc, HEAD), lambda i, b, h: (i, b, h, 0, 0)),
        out_shape=jax.ShapeDtypeStruct((2, bsz, hkv, nc, HEAD), F32),
        compiler_params=_params(3),
        name="nsa_compress",
    )(kc_vc, cmp_pos.reshape(2, 1, 2 * wid), cmp_w1, cmp_b1.reshape(2, 1, 2 * HEAD), cmp_w2, gain.reshape(1, HEAD))


def _nsa_cmp_kernel(q_ref, kc_ref, vc_ref, ov_ref, o_ref, sel_ref, *, tq):
    hkv, ncmp = kc_ref.shape[1], kc_ref.shape[2]
    nsel = ov_ref.shape[0]
    g = NSA_GROUP
    rows = g * tq
    start = pl.program_id(1) * tq
    q_all = q_ref[...]
    t_pos = start + (lax.broadcasted_iota(jnp.int32, (ncmp, rows), 1) & (tq - 1))
    c_end = lax.broadcasted_iota(jnp.int32, (ncmp, rows), 0) * COMP_STRIDE + (COMP_L - 1)
    mask = c_end <= t_pos
    sid = lax.broadcasted_iota(jnp.int32, (nsel, tq), 0)
    cur = (start + lax.broadcasted_iota(jnp.int32, (nsel, tq), 1)) // SEL_L
    forced = (sid == 0) | (sid == cur) | (sid == cur - 1)
    k_top = min(SEL_N, nsel)
    outs, sels = [], []
    for h in range(hkv):
        q = _group_rows(q_all, h, g)
        s = jnp.where(mask, _dg(kc_ref[0, h].astype(BF16), q, nt=True), -1e30)
        e = jnp.exp(s - jnp.max(s, axis=0, keepdims=True))
        p = jnp.where(mask, e / jnp.sum(e, axis=0, keepdims=True), 0.0)
        outs.append(jnp.dot(_transpose_bf16(vc_ref[0, h]), p.astype(BF16), preferred_element_type=F32))
        p_sum = p[:, 0:tq]
        for i in range(1, g):
            p_sum = p_sum + p[:, i * tq:(i + 1) * tq]
        imp = _mm(ov_ref[...], p_sum, pb=3)
        val = jnp.where(forced, 1e9, jnp.where(sid <= cur, imp, -1e9))
        rank = jnp.zeros(imp.shape, jnp.int32)
        for j in range(nsel):
            cj = val[j:j + 1, :]
            ahead = (cj > val) | ((cj == val) & (sid > j))
            rank = rank + ahead.astype(jnp.int32)
        sels.append(jnp.where((rank < k_top) & (val > -1e8), 1.0, 0.0))
    o_ref[...] = jnp.concatenate(outs, axis=0)
    sel_ref[...] = jnp.concatenate(sels, axis=0).astype(sel_ref.dtype)


def _group_rows(q_all, h, g):
    return jnp.concatenate([q_all[:, (h * g + i) * HEAD:(h * g + i + 1) * HEAD] for i in range(g)], axis=0)


NSA_Q_TILE = 256


def _nsa_cmp(q_n, k_cmp, v_cmp, overlap_t, bsz, seq, tq=NSA_Q_TILE):
    hkv = k_cmp.shape[1]
    nsel, ncmp = overlap_t.shape
    nt = seq // tq
    rows = NSA_GROUP * tq
    return pl.pallas_call(
        functools.partial(_nsa_cmp_kernel, tq=tq),
        grid=(bsz, nt),
        in_specs=[pl.BlockSpec((tq, BRANCH_W), lambda b, t: (b * nt + t, 0)),
                  pl.BlockSpec((1, hkv, ncmp, HEAD), lambda b, t: (b, 0, 0, 0)),
                  pl.BlockSpec((1, hkv, ncmp, HEAD), lambda b, t: (b, 0, 0, 0)),
                  pl.BlockSpec((nsel, ncmp), lambda b, t: (0, 0))],
        out_specs=[pl.BlockSpec((hkv * HEAD, rows), lambda b, t: (b * nt + t, 0)),
                   pl.BlockSpec((hkv * nsel, tq), lambda b, t: (b * nt + t, 0))],
        out_shape=[jax.ShapeDtypeStruct((bsz * nt * hkv * HEAD, rows), F32),
                   jax.ShapeDtypeStruct((bsz * nt * hkv * nsel, tq), BF16)],
        compiler_params=_params(2),
        name="nsa_cmp",
    )(q_n, k_cmp, v_cmp, overlap_t)


SCORE_MASKED = -1e30
SCORE_FLOOR = -1e20


def _nsa_attn_kernel(q_ref, sel_ref, ks_ref, vs_ref, kw_ref, vw_ref, oc_ref, gl_ref, o_ref, *, tq, kt):
    g = NSA_GROUP
    hkv = ks_ref.shape[1] // HEAD
    seq = ks_ref.shape[0]
    nsel = sel_ref.shape[0] // hkv
    rows = g * tq
    start = pl.program_id(1) * tq
    q_all = q_ref[...]
    sel_all = sel_ref[...]
    qs = [_group_rows(q_all, h, g) for h in range(hkv)]
    sels = [sel_all[h * nsel:(h + 1) * nsel, :] for h in range(hkv)]
    head_cols = lambda x, h: x[:, h * HEAD:(h + 1) * HEAD]
    per_query = lambda x: jnp.concatenate([x] * g, axis=1)
    with_ones = lambda vt: jnp.concatenate([vt, jnp.ones((16, vt.shape[1]), BF16)], axis=0)

    key_pos = lax.broadcasted_iota(jnp.int32, (kt, tq), 0)
    t_pos = start + lax.broadcasted_iota(jnp.int32, (kt, tq), 1)
    blk_of_key = lax.broadcasted_iota(jnp.int32, (kt, nsel), 0) // SEL_L
    blk_id = lax.broadcasted_iota(jnp.int32, (kt, nsel), 1)

    def key_tile(j, carry, causal):
        base = pl.multiple_of(j * kt, kt)
        kb2 = ks_ref[pl.ds(base, kt), :]
        vt2 = vs_ref[:, pl.ds(base, kt)]
        expand = jnp.where(blk_of_key + j * (kt // SEL_L) == blk_id, 1.0, 0.0).astype(BF16)
        out = []
        for h in range(hkv):
            m, acc = carry[h]
            keep = jnp.dot(expand, sels[h], preferred_element_type=F32) > 0.5
            if causal:
                keep = keep & (key_pos + base <= t_pos)
            bias = per_query(jnp.where(keep, 0.0, SCORE_MASKED))
            s = _dg(head_cols(kb2, h), qs[h], nt=True) + bias
            m_new = jnp.maximum(m, jnp.max(s, axis=0, keepdims=True))
            p = jnp.exp(s - m_new).astype(BF16)
            vt = with_ones(vt2[h * HEAD:(h + 1) * HEAD, :])
            acc = jnp.exp(m - m_new) * acc + jnp.dot(vt, p, preferred_element_type=F32)
            out.append((m_new, acc))
        return tuple(out)

    init = tuple((jnp.full((1, rows), SCORE_FLOOR, F32), jnp.zeros((HEAD + 16, rows), F32)) for _ in range(hkv))
    n_full = start // kt
    carry = lax.fori_loop(0, n_full, lambda j, c: key_tile(j, c, False), init)
    carry = key_tile(n_full, carry, True)
    o_sel = [acc[0:HEAD] / acc[HEAD:HEAD + 1] for _, acc in carry]

    span = min(WINDOW + tq, seq)
    wbase = pl.multiple_of(jnp.maximum(start - WINDOW, 0), tq) if seq > span else 0
    kb2 = kw_ref[pl.ds(wbase, span), :]
    vt2 = vw_ref[:, pl.ds(wbase, span)]
    tw = start + lax.broadcasted_iota(jnp.int32, (span, tq), 1)
    wpos = wbase + lax.broadcasted_iota(jnp.int32, (span, tq), 0)
    wbias = per_query(jnp.where((wpos <= tw) & (wpos > tw - WINDOW), 0.0, SCORE_MASKED))
    o_win = []
    for h in range(hkv):
        s = _dg(head_cols(kb2, h), qs[h], nt=True) + wbias
        p = jnp.exp(s - jnp.max(s, axis=0, keepdims=True)).astype(BF16)
        acc = jnp.dot(with_ones(vt2[h * HEAD:(h + 1) * HEAD, :]), p, preferred_element_type=F32)
        o_win.append(acc[0:HEAD] / acc[HEAD:HEAD + 1])

    gate_t = _mm(_eye(128), jax.nn.sigmoid(gl_ref[...]), pb=3, nt=True)
    o_cmp = oc_ref[...]
    blocks = []
    for h in range(hkv):
        for i in range(g):
            cols = slice(i * tq, (i + 1) * tq)
            r = (h * g + i) * 3
            blocks.append(gate_t[r:r + 1] * o_cmp[h * HEAD:(h + 1) * HEAD, cols]
                          + gate_t[r + 1:r + 2] * o_sel[h][:, cols] + gate_t[r + 2:r + 3] * o_win[h][:, cols])
    y_t = jnp.concatenate(blocks, axis=0)
    o_ref[...] = _dg(_eye(tq), y_t.astype(BF16), nt=True).astype(o_ref.dtype)


def _nsa_attn(p_small, q_n, sel_t, ks, vs_t, kw, vw_t, o_cmp_t, bsz, seq, tq=NSA_Q_TILE, kt=512):
    kt = min(kt, seq)
    nt = seq // tq
    hkv = NSA_KV_HEADS
    nsel = seq // SEL_L
    row = lambda w: pl.BlockSpec((tq, w), lambda b, t: (b * nt + t, 0))
    tile = lambda r, w: pl.BlockSpec((r, w), lambda b, t: (b * nt + t, 0))
    k_spec = pl.BlockSpec((seq, hkv * HEAD), lambda b, t: (b, 0))
    vt_spec = pl.BlockSpec((hkv * HEAD, seq), lambda b, t: (b, 0))
    return pl.pallas_call(
        functools.partial(_nsa_attn_kernel, tq=tq, kt=kt),
        grid=(bsz, nt),
        in_specs=[row(BRANCH_W), tile(hkv * nsel, tq), k_spec, vt_spec, k_spec, vt_spec,
                  tile(hkv * HEAD, NSA_GROUP * tq),
                  pl.BlockSpec((tq, 128), lambda b, t: (b * nt + t, C_GL // 128))],
        out_specs=row(BRANCH_W),
        out_shape=jax.ShapeDtypeStruct((bsz * seq, BRANCH_W), BF16),
        compiler_params=_params(2),
        name="nsa_attn",
    )(q_n, sel_t, ks, vs_t, kw, vw_t, o_cmp_t, p_small)


def _nsa(p_small, bsz, seq, qk_gain, cmp_pos, cmp_w1, cmp_b1, cmp_w2):
    hkv = NSA_KV_HEADS
    q_n, ks_n, kw_n, vs_t, vw_t = _nsa_norm(p_small, qk_gain, bsz, seq)
    ncmp = seq // COMP_STRIDE
    grouped = lambda c: p_small[:, c:c + 128].reshape(bsz, ncmp, COMP_STRIDE, hkv, HEAD).transpose(0, 3, 1, 2, 4)
    kc_vc = jnp.stack([grouped(C_KC), grouped(C_VC)]).reshape(2, bsz, hkv, ncmp, COMP_STRIDE * HEAD)
    cmp = _nsa_compress(kc_vc, cmp_pos, cmp_w1, cmp_b1, cmp_w2, qk_gain[1])
    nsel = seq // SEL_L
    c0 = np.arange(ncmp)[None, :] * COMP_STRIDE
    s0 = np.arange(nsel)[:, None] * SEL_L
    overlap_t = np.clip(np.minimum(c0 + COMP_L, s0 + SEL_L) - np.maximum(c0, s0), 0, None) / COMP_L
    o_cmp_t, sel_t = _nsa_cmp(q_n, cmp[0], cmp[1], jnp.asarray(overlap_t, BF16), bsz, seq)
    return _nsa_attn(p_small, q_n, sel_t, ks_n, vs_t, kw_n, vw_t, o_cmp_t, bsz, seq)


HALO = 16


def _conv_pool_kernel(bg_ref, cg_ref, xi_ref, pu_ref, cw_ref, conv_ref, pool_ref, cz_ref, cp_ref):
    tt = xi_ref.shape[0]
    ti = pl.program_id(1)

    @pl.when(ti == 0)
    def _():
        cz_ref[...] = jnp.zeros_like(cz_ref)
        cp_ref[...] = jnp.zeros_like(cp_ref)

    def history(x, carry_ref):
        ext = jnp.concatenate([carry_ref[...], x], axis=0)
        carry_ref[...] = x[tt - HALO:tt, :]
        return ext

    lag = lambda ext, s: pltpu.roll(ext, s, axis=0)
    body = lambda ext: ext[HALO:HALO + tt, :]

    z = cg_ref[...] * xi_ref[...]
    ze = history(z, cz_ref)
    cw = cw_ref[...]
    y = cw[2:3, :] * z + cw[1:2, :] * body(lag(ze, 1)) + cw[0:1, :] * body(lag(ze, 2))
    conv_ref[...] = (bg_ref[...] * y).astype(conv_ref.dtype)

    u = pu_ref[...]
    sums = [history(u, cp_ref)]
    for w in (1, 2, 4, 8):
        sums.append(sums[-1] + lag(sums[-1], w))
    count = (ti * tt + 1 + lax.broadcasted_iota(jnp.int32, (tt, 128), 0)).astype(F32)
    outs = []
    for gi, w in enumerate(POOL_WINDOWS):
        sl = slice(gi * 128, (gi + 1) * 128)
        outs.append(body(sums[gi + 1])[:, sl] / jnp.minimum(count, float(w)) - u[:, sl])
    pool_ref[...] = jnp.concatenate(outs, axis=1).astype(pool_ref.dtype)


def _conv_pool(p_small, bsz, seq, conv_w, tt=512):
    tt = min(tt, seq)
    nt = seq // tt
    n = bsz * seq
    col = lambda c: pl.BlockSpec((tt, 512), lambda b, t: (b * nt + t, c // 512))
    row = pl.BlockSpec((tt, 512), lambda b, t: (b * nt + t, 0))
    return pl.pallas_call(
        _conv_pool_kernel,
        grid=(bsz, nt),
        in_specs=[col(C_CONV), col(C_CONV + 512), col(C_CONV + 1024), col(C_POOL),
                  pl.BlockSpec((8, 512), lambda b, t: (0, 0))],
        out_specs=[row, row],
        out_shape=[jax.ShapeDtypeStruct((n, 512), BF16)] * 2,
        scratch_shapes=[pltpu.VMEM((HALO, 512), F32)] * 2,
        compiler_params=_params(2),
        name="conv_pool",
    )(p_small, p_small, p_small, p_small, jnp.pad(conv_w, ((0, 8 - conv_w.shape[0]), (0, 0))))


def _mem_attn_kernel(q_ref, kv_ref, gq_ref, gk_ref, o_ref, *, scale):
    outs = []
    for h in range(MEM_HEADS):
        sl = slice(h * MEM_HEAD, (h + 1) * MEM_HEAD)
        q = q_ref[:, sl]
        q = q * lax.rsqrt(jnp.mean(q * q, axis=-1, keepdims=True) + NORM_EPS) * gq_ref[...]
        k = kv_ref[0, :, sl]
        k = k * lax.rsqrt(jnp.mean(k * k, axis=-1, keepdims=True) + NORM_EPS) * gk_ref[...]
        v = kv_ref[0, :, BRANCH_W + h * MEM_HEAD:BRANCH_W + (h + 1) * MEM_HEAD]
        s = _dg(q.astype(BF16), k.astype(BF16), nt=True) * scale
        e = jnp.exp(s - jnp.max(s, axis=-1, keepdims=True))
        p = e / jnp.sum(e, axis=-1, keepdims=True)
        outs.append(jnp.dot(p.astype(BF16), v.astype(BF16), preferred_element_type=F32))
    o_ref[...] = jnp.concatenate(outs, axis=1).astype(o_ref.dtype)


def _mem_attn(p_small, kv, bsz, seq, qk_gain, tq=512):
    tq = min(tq, seq)
    nt = seq // tq
    mlen = kv.shape[1]
    return pl.pallas_call(
        functools.partial(_mem_attn_kernel, scale=MEM_HEAD ** -0.5),
        grid=(bsz, nt),
        in_specs=[pl.BlockSpec((tq, 512), lambda b, t: (b * nt + t, C_MEM // 512)),
                  pl.BlockSpec((1, mlen, 2 * BRANCH_W), lambda b, t: (b, 0, 0)),
                  pl.BlockSpec((1, MEM_HEAD), lambda b, t: (0, 0)),
                  pl.BlockSpec((1, MEM_HEAD), lambda b, t: (0, 0))],
        out_specs=pl.BlockSpec((tq, 512), lambda b, t: (b * nt + t, 0)),
        out_shape=jax.ShapeDtypeStruct((bsz * seq, 512), BF16),
        compiler_params=_params(2),
        name="mem_attn",
    )(p_small, kv, qk_gain[0].reshape(1, -1), qk_gain[1].reshape(1, -1))


def _merge_kernel(h_ref, *refs):
    wg_refs, y_refs = refs[0:5], refs[5:9]
    wb_ref, pooled_ref, wp_ref, ps_ref, o_ref, wgc_ref, wbc_ref = refs[9:]

    @pl.when(pl.program_id(1) == 0)
    def _():
        for i in range(5):
            wgc_ref[i] = wg_refs[i][...].astype(BF16)
        wbc_ref[...] = wb_ref[...].astype(BF16)

    h = h_ref[...]
    gate = lambda i: jax.nn.sigmoid(jnp.dot(h, wgc_ref[i], preferred_element_type=F32))
    z_pool = jnp.dot(pooled_ref[...], wp_ref[0].astype(BF16), preferred_element_type=F32) * ps_ref[...]
    acc = gate(4) * z_pool
    for i in range(4):
        acc = acc + gate(i) * jnp.dot(y_refs[i][...], wbc_ref[i], preferred_element_type=F32)
    o_ref[...] = acc.astype(o_ref.dtype)


def _merge(h, w_gate, ys, w_branch, pooled, pool_w, pool_scale, tm=512):
    n = h.shape[0]
    tm = min(tm, n)
    tn = 512
    nj = D_MODEL // tn
    once = pl.Buffered(1)
    gate_spec = lambda g: pl.BlockSpec((D_MODEL, tn), lambda j, i: (0, g * nj + j), pipeline_mode=once)
    y_spec = pl.BlockSpec((tm, BRANCH_W), lambda j, i: (i, 0))
    return pl.pallas_call(
        _merge_kernel,
        grid=(nj, n // tm),
        in_specs=[pl.BlockSpec((tm, D_MODEL), lambda j, i: (i, 0))] + [gate_spec(g) for g in range(5)]
        + [y_spec] * 4
        + [pl.BlockSpec((4, BRANCH_W, tn), lambda j, i: (0, 0, j), pipeline_mode=once),
           pl.BlockSpec((tm, 128), lambda j, i: (i, j)),
           pl.BlockSpec((1, 128, tn), lambda j, i: (j, 0, 0)),
           pl.BlockSpec((1, tn), lambda j, i: (0, j))],
        out_specs=pl.BlockSpec((tm, tn), lambda j, i: (i, j)),
        out_shape=jax.ShapeDtypeStruct((n, D_MODEL), BF16),
        scratch_shapes=[pltpu.VMEM((5, D_MODEL, tn), BF16), pltpu.VMEM((4, BRANCH_W, tn), BF16)],
        compiler_params=_params(2),
        name="merge",
    )(h, *([w_gate] * 5), *ys, w_branch, pooled, pool_w, pool_scale.reshape(1, -1))


def _pack_w_in(w, vres_w):
    pad = lambda a, width: jnp.pad(a, ((0, 0), (0, width - a.shape[1])))
    nsa = w[:, 1984:3288]
    vres_cols = pad(vres_w, 128) if vres_w is not None else jnp.zeros((D_MODEL, 128), w.dtype)
    small = jnp.concatenate([
        w[:, 0:1536], pad(w[:, 1536:1632], 128), pad(w[:, 1632:1728], 128), w[:, 1728:1984],
        nsa[:, 0:512], w[:, 3288:4824], w[:, 4824:5336], w[:, 5336:5848],
        nsa[:, 512:1280], pad(nsa[:, 1280:1304], 128), vres_cols], axis=1)
    return small, w[:, 5848:]


def kernel(x, mem, norm_mix, norm_ffn, norm_mem, w_in, rwkv_mu, rwkv_w0, rwkv_w2, rwkv_a0, rwkv_a2, rwkv_g2, rwkv_kk, rwkv_ka, rwkv_rk, rwkv_ln_w, rwkv_ln_b, vres_in, vres_mu, vres_v0, vres_up, nsa_qk_gain, nsa_cmp_pos, nsa_cmp_w1, nsa_cmp_b1, nsa_cmp_w2, conv_w, pool_w, pool_scale, mem_wkv, mem_qk_gain, w_branch, w_out, ffn_w1, ffn_w3, ffn_w2, moe_router, moe_w1, moe_w3, moe_w2):
    bsz, seq, d = x.shape
    n = bsz * seq
    depth = w_in.shape[0]
    mlen = mem.shape[1]
    xf = x.reshape(n, d)
    memf = mem.reshape(bsz * mlen, d)
    v_first = None
    for l in range(depth):
        h = _rmsnorm(xf, norm_mix[l])
        w_small, w_gate = _pack_w_in(w_in[l], vres_in[l - 1] if l > 0 else None)
        p_small = _matmul(h, w_small, tm=1024, tn=1024, name="in_proj")
        vres = (vres_mu[l - 1], vres_v0[l - 1], vres_up[l - 1], v_first) if l > 0 else None
        r, lw, k, v, kk, b, g = _rwkv_prep(p_small, bsz, seq, rwkv_mu[l], rwkv_w0[l], rwkv_w2[l], rwkv_a0[l],
                                           rwkv_a2[l], rwkv_g2[l], rwkv_kk[l], rwkv_ka[l], vres)
        if l == 0:
            v_first = v
        y_rwkv = _rwkv_recurrence(r, lw, k, v, kk, b, g, rwkv_rk[l], rwkv_ln_w[l], rwkv_ln_b[l], bsz, seq)
        y_nsa = _nsa(p_small, bsz, seq, nsa_qk_gain[l], nsa_cmp_pos[l], nsa_cmp_w1[l], nsa_cmp_b1[l], nsa_cmp_w2[l])
        y_conv, pooled = _conv_pool(p_small, bsz, seq, conv_w[l])
        mem_n = _rmsnorm(memf, norm_mem[l])
        kv = _matmul(mem_n, mem_wkv[l], tm=512, tn=512, name="mem_kv").reshape(bsz, mlen, 2 * BRANCH_W)
        y_mem = _mem_attn(p_small, kv, bsz, seq, mem_qk_gain[l])
        merged = _merge(h, w_gate, (y_rwkv, y_nsa, y_conv, y_mem), w_branch[l], pooled, pool_w[l], pool_scale[l])
        xf = _matmul(merged, w_out[l], tm=1024, tn=1024, res=xf, name="out_proj")

        if l % 2 == 0:
            h2 = _rmsnorm(xf, norm_ffn[l])
            e = l // 2
            act = _swiglu_up(h2, ffn_w1[e], ffn_w3[e])
            xf = _matmul(act, ffn_w2[e], tm=512, tn=512, res=xf, name="ffn_down")
        else:
            e = l // 2
            xf = _moe(xf, norm_ffn[l], moe_router[e], moe_w1[e], moe_w3[e], moe_w2[e])
    return xf.reshape(bsz, seq, d)
```

```python
import functools

import jax
import jax.numpy as jnp
import numpy as np
from jax import lax
from jax.experimental import pallas as pl
from jax.experimental.pallas import tpu as pltpu

F32 = jnp.float32
BF16 = jnp.bfloat16

D_MODEL = 2048
BRANCH_W = 512
HEAD = 64
RWKV_HEADS = 8
NSA_KV_HEADS = 2
NSA_GROUP = 4
COMP_L = 32
COMP_STRIDE = 16
SEL_L = 64
SEL_N = 16
WINDOW = 512
MEM_HEADS = 4
MEM_HEAD = 128
POOL_WINDOWS = (2, 4, 8, 16)
N_EXPERTS = 8
NORM_EPS = 1e-6
RWKV_GN_EPS = 64e-5
RWKV_CHUNK = 64

VMEM_LIMIT_BYTES = 56 * 1024 * 1024

C_RWKV, C_NSAQ, C_CONV, C_POOL, C_MEM = 0, 2048, 2560, 4096, 4608
C_KC, C_VC, C_KS, C_VS, C_KW, C_VW, C_GL, C_VRES = 5120, 5248, 5376, 5504, 5632, 5760, 5888, 6016
N_SMALL = 6144


def _params(n_axes):
    return pltpu.CompilerParams(dimension_semantics=("arbitrary",) * n_axes,
                                vmem_limit_bytes=VMEM_LIMIT_BYTES)


def _split(a, n):
    pieces, r = [], a
    for i in range(n):
        p = r.astype(BF16)
        pieces.append(p)
        if i + 1 < n:
            r = r - p.astype(F32)
    return pieces


def _dg(a, b, nt):
    dims = (((1,), (1,)), ((), ())) if nt else (((1,), (0,)), ((), ()))
    return lax.dot_general(a, b, dims, preferred_element_type=F32)


def _mm(a, b, pa=1, pb=1, nt=False):
    sa = _split(a, pa) if a.dtype != BF16 else [a]
    sb = _split(b, pb) if b.dtype != BF16 else [b]
    order = max(len(sa), len(sb))
    acc = None
    for i, x in enumerate(sa):
        for j, y in enumerate(sb):
            if i + j < order:
                t = _dg(x, y, nt)
                acc = t if acc is None else acc + t
    return acc


def _head_sum_matrix(width, head):
    r = lax.broadcasted_iota(jnp.int32, (width, width), 0) // head
    c = lax.broadcasted_iota(jnp.int32, (width, width), 1) // head
    return jnp.where(r == c, 1.0, 0.0).astype(BF16)


def _group_mean(x, head):
    hs = _head_sum_matrix(x.shape[-1], head)
    return _mm(x, hs, pa=3) * (1.0 / head)


def _rmsnorm_kernel(x_ref, g_ref, o_ref):
    x = x_ref[...]
    y = x * lax.rsqrt(jnp.mean(x * x, axis=-1, keepdims=True) + NORM_EPS)
    o_ref[...] = (y * g_ref[...]).astype(o_ref.dtype)


def _rmsnorm(x, g, tm=512):
    m, d = x.shape
    tm = min(tm, m)
    return pl.pallas_call(
        _rmsnorm_kernel,
        grid=(m // tm,),
        in_specs=[pl.BlockSpec((tm, d), lambda i: (i, 0)), pl.BlockSpec((1, d), lambda i: (0, 0))],
        out_specs=pl.BlockSpec((tm, d), lambda i: (i, 0)),
        out_shape=jax.ShapeDtypeStruct((m, d), BF16),
        compiler_params=_params(1),
        name="rmsnorm",
    )(x, g.reshape(1, d))


def _rmsnorm_router_kernel(x_ref, g_ref, wr_ref, o_ref, comb_ref, assign_ref, count_ref, carry_ref):
    tm = x_ref.shape[0]

    @pl.when(pl.program_id(0) == 0)
    def _():
        carry_ref[...] = jnp.zeros_like(carry_ref)

    x = x_ref[...]
    y = x * lax.rsqrt(jnp.mean(x * x, axis=-1, keepdims=True) + NORM_EPS) * g_ref[...]
    _store_slabs(o_ref, y)
    logits = _mm(y, wr_ref[...], pa=3, pb=3)
    lane = lax.broadcasted_iota(jnp.int32, logits.shape, 1)
    neg = jnp.float32(-3e38)
    lg = jnp.where(lane < N_EXPERTS, logits, neg)
    m1 = jnp.max(lg, axis=-1, keepdims=True)
    i1 = jnp.min(jnp.where(lg == m1, lane, 1 << 20), axis=-1, keepdims=True)
    lg2 = jnp.where(lane == i1, neg, lg)
    m2 = jnp.max(lg2, axis=-1, keepdims=True)
    i2 = jnp.min(jnp.where(lg2 == m2, lane, 1 << 20), axis=-1, keepdims=True)
    e2 = jnp.exp(m2 - m1)
    w1 = 1.0 / (1.0 + e2)
    w2 = e2 / (1.0 + e2)
    comb_ref[...] = jnp.where(lane == i1, w1, 0.0) + jnp.where(lane == i2, w2, 0.0)
    assign = jnp.where((lane == i1) | (lane == i2), 1.0, 0.0)
    assign_ref[...] = assign
    row = lax.broadcasted_iota(jnp.int32, (tm, tm), 0)
    col = lax.broadcasted_iota(jnp.int32, (tm, tm), 1)
    tril = jnp.where(row >= col, 1.0, 0.0).astype(BF16)
    count = jnp.dot(tril, assign.astype(BF16), preferred_element_type=F32) + carry_ref[0:1, :]
    count_ref[...] = count
    carry_ref[0:1, :] = count[tm - 1:tm, :]


LANES = 128


def _store_slabs(o_ref, y):
    for s in range(o_ref.shape[1]):
        o_ref[:, s, :] = y[:, s * LANES:(s + 1) * LANES].astype(o_ref.dtype)


def _rmsnorm_router(x, g, router, tm=512):
    m, d = x.shape
    tm = min(tm, m)
    wr = jnp.pad(router, ((0, 0), (0, 128 - router.shape[1])))
    lanes = pl.BlockSpec((tm, 128), lambda i: (i, 0))
    return pl.pallas_call(
        _rmsnorm_router_kernel,
        grid=(m // tm,),
        in_specs=[pl.BlockSpec((tm, d), lambda i: (i, 0)), pl.BlockSpec((1, d), lambda i: (0, 0)),
                  pl.BlockSpec((d, 128), lambda i: (0, 0))],
        out_specs=[pl.BlockSpec((tm, d // LANES, LANES), lambda i: (i, 0, 0)), lanes, lanes, lanes],
        out_shape=[jax.ShapeDtypeStruct((m, d // LANES, LANES), F32)] + [jax.ShapeDtypeStruct((m, 128), F32)] * 3,
        scratch_shapes=[pltpu.VMEM((8, 128), F32)],
        compiler_params=_params(1),
        name="rmsnorm_router",
    )(x, g.reshape(1, d), wr)


MOE_TILE = 256


def _row_copy(src_hbm, row, dst, slot, sem):
    return pltpu.make_async_copy(src_hbm.at[row], dst.at[slot], sem)


def _load_slabs(buf, dtype=None):
    cols = [buf[:, s, :] for s in range(buf.shape[1])]
    return cols if dtype is None else [c.astype(dtype) for c in cols]


def _gather_rows_kernel(idx_ref, nxt_ref, src_hbm, o_ref, buf_ref, sem):
    tm = o_ref.shape[0]

    i = pl.program_id(0)
    slot = i % 2

    def issue(ids_ref, to_slot):
        def body(r, carry):
            _row_copy(src_hbm, ids_ref[0, 0, r], buf_ref.at[to_slot], r, sem.at[to_slot]).start()
            return carry
        lax.fori_loop(0, tm, body, 0)

    @pl.when(i == 0)
    def _():
        issue(idx_ref, 0)

    @pl.when(i + 1 < pl.num_programs(0))
    def _():
        issue(nxt_ref, 1 - slot)

    def wait(r, carry):
        _row_copy(src_hbm, 0, buf_ref.at[slot], r, sem.at[slot]).wait()
        return carry

    lax.fori_loop(0, tm, wait, 0)
    o_ref[...] = jnp.concatenate(_load_slabs(buf_ref.at[slot], o_ref.dtype), axis=1)


def _gather_rows(src, idx, tm=MOE_TILE):
    r = idx.shape[0]
    slab = src.shape[1:]
    d = slab[0] * slab[1]
    n_tiles = r // tm
    ids = idx.reshape(n_tiles, 1, tm)
    return pl.pallas_call(
        _gather_rows_kernel,
        grid=(n_tiles,),
        in_specs=[pl.BlockSpec((1, 1, tm), lambda i: (i, 0, 0), memory_space=pltpu.SMEM),
                  pl.BlockSpec((1, 1, tm), lambda i: (jnp.minimum(i + 1, n_tiles - 1), 0, 0),
                               memory_space=pltpu.SMEM),
                  pl.BlockSpec(memory_space=pl.ANY)],
        out_specs=pl.BlockSpec((tm, d), lambda i: (i, 0)),
        out_shape=jax.ShapeDtypeStruct((r, d), BF16),
        scratch_shapes=[pltpu.VMEM((2, tm) + slab, F32), pltpu.SemaphoreType.DMA((2,))],
        compiler_params=_params(1),
        name="moe_gather",
    )(ids, ids, src)


def _expert_changed(te_ref, i):
    return (i == 0) | (te_ref[i] != te_ref[jnp.maximum(i - 1, 0)])


def _moe_up_kernel(te_ref, na_ref, x_ref, w1_ref, w3_ref, o_ref, w1c_ref, w3c_ref):
    i = pl.program_id(1)
    active = i < na_ref[0]

    @pl.when(active & _expert_changed(te_ref, i))
    def _():
        w1c_ref[...] = w1_ref[0].astype(BF16)
        w3c_ref[...] = w3_ref[0].astype(BF16)

    @pl.when(active)
    def _():
        x = x_ref[...]
        a = jnp.dot(x, w1c_ref[...], preferred_element_type=F32)
        b = jnp.dot(x, w3c_ref[...], preferred_element_type=F32)
        o_ref[...] = (a * jax.nn.sigmoid(a) * b).astype(o_ref.dtype)

    @pl.when(jnp.logical_not(active))
    def _():
        o_ref[...] = jnp.zeros_like(o_ref)


def _moe_down_kernel(te_ref, na_ref, a_ref, w_ref, o_ref, wc_ref):
    i = pl.program_id(1)
    active = i < na_ref[0]

    @pl.when(active & _expert_changed(te_ref, i))
    def _():
        wc_ref[...] = w_ref[0].astype(BF16)

    @pl.when(active)
    def _():
        _store_slabs(o_ref, jnp.dot(a_ref[...], wc_ref[...], preferred_element_type=F32))

    @pl.when(jnp.logical_not(active))
    def _():
        o_ref[...] = jnp.zeros_like(o_ref)


def _moe_experts(x_sorted, tile_expert, n_active, w1, w3, w2, tm=MOE_TILE):
    rows, d = x_sorted.shape
    ff = w1.shape[2]
    n_tiles = rows // tm
    tn_up, tn_down = ff // 2, d // 2
    once = pl.Buffered(1)
    act = pl.pallas_call(
        _moe_up_kernel,
        grid_spec=pltpu.PrefetchScalarGridSpec(
            num_scalar_prefetch=2,
            grid=(ff // tn_up, n_tiles),
            in_specs=[pl.BlockSpec((tm, d), lambda j, i, te, na: (i, 0)),
                      pl.BlockSpec((1, d, tn_up), lambda j, i, te, na: (te[i], 0, j), pipeline_mode=once),
                      pl.BlockSpec((1, d, tn_up), lambda j, i, te, na: (te[i], 0, j), pipeline_mode=once)],
            out_specs=pl.BlockSpec((tm, tn_up), lambda j, i, te, na: (i, j)),
            scratch_shapes=[pltpu.VMEM((d, tn_up), BF16)] * 2),
        out_shape=jax.ShapeDtypeStruct((rows, ff), BF16),
        compiler_params=_params(2),
        name="moe_up",
    )(tile_expert, n_active, x_sorted, w1, w3)
    return pl.pallas_call(
        _moe_down_kernel,
        grid_spec=pltpu.PrefetchScalarGridSpec(
            num_scalar_prefetch=2,
            grid=(d // tn_down, n_tiles),
            in_specs=[pl.BlockSpec((tm, ff), lambda j, i, te, na: (i, 0)),
                      pl.BlockSpec((1, ff, tn_down), lambda j, i, te, na: (te[i], 0, j), pipeline_mode=once)],
            out_specs=pl.BlockSpec((tm, tn_down // LANES, LANES), lambda j, i, te, na: (i, j, 0)),
            scratch_shapes=[pltpu.VMEM((ff, tn_down), BF16)]),
        out_shape=jax.ShapeDtypeStruct((rows, d // LANES, LANES), F32),
        compiler_params=_params(2),
        name="moe_down",
    )(tile_expert, n_active, act, w2)


def _moe_combine_kernel(pa_ref, pb_ref, na_ref, nb_ref, y_hbm, x_ref, w_ref, o_ref, buf_ref, sem):
    tm = x_ref.shape[0]
    i = pl.program_id(0)
    slot = i % 2

    def issue(a_ref, b_ref, to_slot):
        def body(r, carry):
            _row_copy(y_hbm, a_ref[0, 0, r], buf_ref.at[to_slot, 0], r, sem.at[to_slot]).start()
            _row_copy(y_hbm, b_ref[0, 0, r], buf_ref.at[to_slot, 1], r, sem.at[to_slot]).start()
            return carry
        lax.fori_loop(0, tm, body, 0)

    @pl.when(i == 0)
    def _():
        issue(pa_ref, pb_ref, 0)

    @pl.when(i + 1 < pl.num_programs(0))
    def _():
        issue(na_ref, nb_ref, 1 - slot)

    def wait(r, carry):
        _row_copy(y_hbm, 0, buf_ref.at[slot, 0], r, sem.at[slot]).wait()
        _row_copy(y_hbm, 0, buf_ref.at[slot, 1], r, sem.at[slot]).wait()
        return carry

    lax.fori_loop(0, tm, wait, 0)
    w = w_ref[...]
    mixed = [w[:, 0:1] * a + w[:, 1:2] * b
             for a, b in zip(_load_slabs(buf_ref.at[slot, 0]), _load_slabs(buf_ref.at[slot, 1]))]
    o_ref[...] = x_ref[...] + jnp.concatenate(mixed, axis=1)


def _moe_combine(x, y_sorted, pos_a, pos_b, weights, tm=MOE_TILE):
    n, d = x.shape
    n_tiles = n // tm
    cur = pl.BlockSpec((1, 1, tm), lambda i: (i, 0, 0), memory_space=pltpu.SMEM)
    nxt = pl.BlockSpec((1, 1, tm), lambda i: (jnp.minimum(i + 1, n_tiles - 1), 0, 0), memory_space=pltpu.SMEM)
    pa, pb = pos_a.reshape(n_tiles, 1, tm), pos_b.reshape(n_tiles, 1, tm)
    return pl.pallas_call(
        _moe_combine_kernel,
        grid=(n_tiles,),
        in_specs=[cur, cur, nxt, nxt, pl.BlockSpec(memory_space=pl.ANY),
                  pl.BlockSpec((tm, d), lambda i: (i, 0)), pl.BlockSpec((tm, 128), lambda i: (i, 0))],
        out_specs=pl.BlockSpec((tm, d), lambda i: (i, 0)),
        out_shape=jax.ShapeDtypeStruct((n, d), F32),
        scratch_shapes=[pltpu.VMEM((2, 2, tm) + y_sorted.shape[1:], F32), pltpu.SemaphoreType.DMA((2,))],
        compiler_params=_params(1),
        name="moe_combine",
    )(pa, pb, pa, pb, y_sorted, x, weights)


def _moe(x, norm_g, router, w1, w3, w2):
    n, d = x.shape
    tm = min(MOE_TILE, n)
    h2, comb, assign, count = _rmsnorm_router(x, norm_g, router)
    assigned = assign[:, :N_EXPERTS] > 0.5
    count = count[:, :N_EXPERTS].astype(jnp.int32)
    total = count[-1]
    padded = (total + tm - 1) // tm * tm
    ends = jnp.cumsum(padded)
    starts = ends - padded
    dest = starts[None, :] + count - 1
    rows = (2 * n // tm + N_EXPERTS) * tm
    tile_start = jnp.arange(rows // tm, dtype=jnp.int32) * tm
    tile_expert = jnp.minimum(jnp.sum(tile_start[:, None] >= ends[None, :], axis=1), N_EXPERTS - 1).astype(jnp.int32)
    n_active = (ends[-1:] // tm).astype(jnp.int32)
    offset = tile_start[:, None] + jnp.arange(tm, dtype=jnp.int32)[None, :] - starts[tile_expert][:, None]
    src = jnp.sum(count.T[tile_expert][:, None, :] <= offset[:, :, None], axis=-1, dtype=jnp.int32)
    src = jnp.minimum(src, n - 1).reshape(rows)
    first = jnp.argmax(assigned, axis=1)
    second = N_EXPERTS - 1 - jnp.argmax(assigned[:, ::-1], axis=1)
    take = lambda a, i: jnp.take_along_axis(a, i[:, None], axis=1)[:, 0]
    pos_a, pos_b = take(dest, first), take(dest, second)
    weights = jnp.pad(jnp.stack([take(comb, first), take(comb, second)], axis=1), ((0, 0), (0, 126)))

    x_sorted = _gather_rows(h2, src, tm)
    y_sorted = _moe_experts(x_sorted, tile_expert, n_active, w1, w3, w2, tm)
    return _moe_combine(x, y_sorted, pos_a, pos_b, weights, tm)


def _matmul_kernel(*refs, has_res, cached):
    a_ref, w_ref = refs[0], refs[1]
    res_ref = refs[2] if has_res else None
    o_ref = refs[3] if has_res else refs[2]
    if cached:
        wc_ref = refs[-1]

        @pl.when(pl.program_id(1) == 0)
        def _():
            wc_ref[...] = w_ref[...].astype(BF16)

        w = wc_ref[...]
    else:
        w = w_ref[...]
    out = jnp.dot(a_ref[...], w, preferred_element_type=F32)
    if has_res:
        out = out + res_ref[...]
    o_ref[...] = out.astype(o_ref.dtype)


def _matmul(a, w, *, tm, tn, out_dtype=F32, res=None, name="matmul"):
    m, kdim = a.shape
    n = w.shape[1]
    tm, tn = min(tm, m), min(tn, n)
    assert m % tm == 0 and n % tn == 0
    cached = w.dtype != BF16
    w_mode = dict(pipeline_mode=pl.Buffered(1)) if cached else {}
    in_specs = [pl.BlockSpec((tm, kdim), lambda j, i: (i, 0)), pl.BlockSpec((kdim, tn), lambda j, i: (0, j), **w_mode)]
    args = [a, w]
    if res is not None:
        in_specs.append(pl.BlockSpec((tm, tn), lambda j, i: (i, j)))
        args.append(res)
    return pl.pallas_call(
        functools.partial(_matmul_kernel, has_res=res is not None, cached=cached),
        grid=(n // tn, m // tm),
        in_specs=in_specs,
        out_specs=pl.BlockSpec((tm, tn), lambda j, i: (i, j)),
        out_shape=jax.ShapeDtypeStruct((m, n), out_dtype),
        scratch_shapes=[pltpu.VMEM((kdim, tn), BF16)] if cached else [],
        compiler_params=_params(2),
        name=name,
    )(*args)


def _swiglu_up_kernel(h_ref, w1_ref, w3_ref, o_ref):
    h = h_ref[...]
    a = jnp.dot(h, w1_ref[...].astype(BF16), preferred_element_type=F32)
    b = jnp.dot(h, w3_ref[...].astype(BF16), preferred_element_type=F32)
    o_ref[...] = (a * jax.nn.sigmoid(a) * b).astype(o_ref.dtype)


def _swiglu_up(h, w1, w3, tm=1024, tn=512):
    m, kdim = h.shape
    n = w1.shape[1]
    tm, tn = min(tm, m), min(tn, n)
    assert m % tm == 0 and n % tn == 0
    return pl.pallas_call(
        _swiglu_up_kernel,
        grid=(n // tn, m // tm),
        in_specs=[pl.BlockSpec((tm, kdim), lambda j, i: (i, 0)),
                  pl.BlockSpec((kdim, tn), lambda j, i: (0, j)),
                  pl.BlockSpec((kdim, tn), lambda j, i: (0, j))],
        out_specs=pl.BlockSpec((tm, tn), lambda j, i: (i, j)),
        out_shape=jax.ShapeDtypeStruct((m, n), BF16),
        compiler_params=_params(2),
        name="swiglu_up",
    )(h, w1, w3)


def _shift_rows(x, carry_row):
    rolled = pltpu.roll(x, 1, axis=0)
    row = lax.broadcasted_iota(jnp.int32, x.shape, 0)
    return jnp.where(row == 0, carry_row, rolled)


def _rwkv_prep_kernel(*refs, has_vres):
    (u_ref, mu_ref, w0_ref, w2_ref, a0_ref, a2_ref, g2_ref, kkw_ref, kaw_ref) = refs[:9]
    pos = 9
    if has_vres:
        vd_ref, vmu_ref, v0_ref, vup_ref, vfirst_ref = refs[pos:pos + 5]
        pos += 5
    r_ref, lw_ref, k_ref, v_ref, kk_ref, b_ref, g_ref = refs[pos:pos + 7]
    pos += 7
    cu_ref = refs[pos]
    cv_ref = refs[pos + 1] if has_vres else None
    tt = u_ref.shape[0]

    @pl.when(pl.program_id(1) == 0)
    def _():
        cu_ref[...] = jnp.zeros_like(cu_ref)
        if has_vres:
            cv_ref[...] = jnp.zeros_like(cv_ref)

    u = u_ref[...]
    prev = _shift_rows(u, cu_ref[0:1, :])
    cu_ref[0:1, :] = u[tt - 1:tt, :]
    uf = u + (prev - u) * mu_ref[...]
    r, k, v = uf[:, 0:512], uf[:, 512:1024], uf[:, 1024:1536]
    wd, ad, gd = uf[:, 1536:1664], uf[:, 1664:1792], uf[:, 1792:2048]

    x = w0_ref[...] + _mm(jnp.tanh(wd), w2_ref[...], 2, 2)
    softplus = jnp.maximum(-x, 0.0) + jnp.log(1.0 + jnp.exp(-jnp.abs(x)))
    lw_ref[...] = -jnp.exp(-softplus - 0.5)
    a = jax.nn.sigmoid(a0_ref[...] + _mm(ad, a2_ref[...], 2, 2))
    g_ref[...] = _mm(jax.nn.sigmoid(gd), g2_ref[...], 2, 2)
    if has_vres:
        vd = vd_ref[...]
        vprev = _shift_rows(vd, cv_ref[0:1, :])
        cv_ref[0:1, :] = vd[tt - 1:tt, :]
        vdf = vd + (vprev - vd) * vmu_ref[...]
        v = v + (vfirst_ref[...] - v) * jax.nn.sigmoid(v0_ref[...] + _mm(vdf, vup_ref[...], 2, 2))
    kk = k * kkw_ref[...]
    ss = _group_mean(kk * kk, HEAD) * HEAD
    kk = kk * lax.rsqrt(jnp.maximum(ss, 1e-24))
    r_ref[...] = r
    k_ref[...] = k * (1.0 + (a - 1.0) * kaw_ref[...])
    v_ref[...] = v
    kk_ref[...] = kk
    b_ref[...] = kk * a


def _rwkv_prep(p_small, bsz, seq, mu, w0, w2, a0, a2, g2, kkw, kaw, vres, tt=256):
    tt = min(tt, seq)
    nt = seq // tt
    n = bsz * seq
    row = lambda b, t: (b * nt + t, 0)
    const = lambda b, t: (0, 0)
    pad_rows = lambda w: jnp.pad(w, ((0, 128 - w.shape[0]), (0, 0)))
    mu_p = jnp.concatenate([mu[:1536], jnp.pad(mu[1536:1632], (0, 32)), jnp.pad(mu[1632:1728], (0, 32)), mu[1728:]])
    vec = lambda a: a.reshape(1, -1)
    args = [p_small, vec(mu_p), vec(w0), pad_rows(w2), vec(a0), pad_rows(a2), g2, vec(kkw), vec(kaw)]
    in_specs = [pl.BlockSpec((tt, 2048), row), pl.BlockSpec((1, 2048), const), pl.BlockSpec((1, 512), const),
                pl.BlockSpec((128, 512), const), pl.BlockSpec((1, 512), const), pl.BlockSpec((128, 512), const),
                pl.BlockSpec((256, 512), const), pl.BlockSpec((1, 512), const), pl.BlockSpec((1, 512), const)]
    scratch = [pltpu.VMEM((8, 2048), F32)]
    if vres is not None:
        vmu, v0, vup, vfirst = vres
        args += [p_small, vec(jnp.pad(vmu, (0, 64))), vec(v0), pad_rows(vup), vfirst]
        in_specs += [pl.BlockSpec((tt, 128), lambda b, t: (b * nt + t, C_VRES // 128)), pl.BlockSpec((1, 128), const),
                     pl.BlockSpec((1, 512), const), pl.BlockSpec((128, 512), const), pl.BlockSpec((tt, 512), row)]
        scratch.append(pltpu.VMEM((8, 128), F32))
    return pl.pallas_call(
        functools.partial(_rwkv_prep_kernel, has_vres=vres is not None),
        grid=(bsz, nt),
        in_specs=in_specs,
        out_specs=[pl.BlockSpec((tt, 512), row)] * 7,
        out_shape=[jax.ShapeDtypeStruct((n, 512), F32)] * 7,
        scratch_shapes=scratch,
        compiler_params=_params(2),
        name="rwkv_prep",
    )(*args)


def _rwkv_chunk_kernel(r_ref, lw_ref, k_ref, v_ref, kk_ref, b_ref, g_ref, rk_ref, lnw_ref, lnb_ref, o_ref, s_ref,
                       *, pw):
    nb, c, _ = r_ref.shape

    @pl.when(pl.program_id(0) == 0)
    def _():
        s_ref[...] = jnp.zeros_like(s_ref)

    row = lax.broadcasted_iota(jnp.int32, (c, c), 0)
    col = lax.broadcasted_iota(jnp.int32, (c, c), 1)
    tril_incl = jnp.where(row >= col, 1.0, 0.0).astype(BF16)
    strict = row > col
    incl = row >= col
    eye = jnp.where(row == col, 1.0, 0.0)
    hrow = lax.broadcasted_iota(jnp.int32, (HEAD, HEAD), 0)
    hcol = lax.broadcasted_iota(jnp.int32, (HEAD, HEAD), 1)
    eye_h = jnp.where(hrow == hcol, 1.0, 0.0).astype(BF16)
    mm = functools.partial(_mm, pa=pw, pb=pw)
    transpose = lambda a, pieces: _mm(eye_h, a, pb=pieces, nt=True)

    heads = [(bi, h) for bi in range(nb) for h in range(RWKV_HEADS)]
    per_head = lambda full: [full[bi][:, h * HEAD:(h + 1) * HEAD] for bi, h in heads]
    each = lambda f, *lists: [f(*vals) for vals in zip(*lists)]

    r, lw, k, v, kk, b = ([ref[bi] for bi in range(nb)] for ref in (r_ref, lw_ref, k_ref, v_ref, kk_ref, b_ref))
    cum = [_mm(tril_incl, x, pb=3) for x in lw]
    g_inv = [jnp.exp(-x) for x in cum]
    a_t = per_head(each(lambda kk_, c_, lw_: kk_ * jnp.exp(c_ - lw_), kk, cum, lw))
    b_t = per_head(each(lambda b_, gi: b_ * gi, b, g_inv))
    k_t = per_head(each(lambda k_, gi: k_ * gi, k, g_inv))
    r_t = per_head(each(lambda r_, c_: r_ * jnp.exp(c_), r, cum))
    vh = per_head(v)
    tail = [jnp.exp(x[c - 1:c, :] - x) for x in cum]
    bh = per_head(each(lambda b_, t_: b_ * t_, b, tail))
    kh = per_head(each(lambda k_, t_: k_ * t_, k, tail))
    g_last = per_head([jnp.exp(x[c - 1:c, :]) for x in cum])

    a_ab = each(lambda x, y: jnp.where(strict, mm(x, y, nt=True), 0.0), a_t, b_t)
    a_ak = each(lambda x, y: jnp.where(strict, mm(x, y, nt=True), 0.0), a_t, k_t)
    a_rb = each(lambda x, y: jnp.where(incl, mm(x, y, nt=True), 0.0), r_t, b_t)
    a_rk = each(lambda x, y: jnp.where(incl, mm(x, y, nt=True), 0.0), r_t, k_t)
    x = [eye - n for n in a_ab]
    p = each(mm, a_ab, a_ab)
    steps = int(np.log2(c)) - 1
    for i in range(steps):
        x = each(lambda x_, p_: x_ + mm(x_, p_), x, p)
        if i + 1 < steps:
            p = each(mm, p, p)
    w1 = each(mm, x, a_t)
    akv = each(mm, a_ak, vh)
    w2 = each(mm, x, akv)
    y_in = each(lambda ark, v_, arb, w2_: mm(ark, v_) - mm(arb, w2_), a_rk, vh, a_rb, w2)
    q_h = each(lambda r_, arb, w1_: r_ - mm(arb, w1_), r_t, a_rb, w1)
    bh_t = [transpose(x_, pw) for x_ in bh]
    kh_t = [transpose(x_, pw) for x_ in kh]
    decay = [transpose(jnp.broadcast_to(x_, (HEAD, HEAD)), 3) for x_ in g_last]
    g_s = each(lambda kt_, v_, bt_, w2_: mm(kt_, v_) - mm(bt_, w2_), kh_t, vh, bh_t, w2)
    s = [s_ref[i] for i in range(len(heads))]
    ys = each(lambda yi, q_, s_: yi + mm(q_, s_), y_in, q_h, s)
    w1s = each(mm, w1, s)
    for i, (d_, s_, bt_, ws_, gs_) in enumerate(zip(decay, s, bh_t, w1s, g_s)):
        s_ref[i] = d_ * s_ - mm(bt_, ws_) + gs_

    for bi in range(nb):
        y = jnp.concatenate(ys[bi * RWKV_HEADS:(bi + 1) * RWKV_HEADS], axis=1)
        mean = _group_mean(y, HEAD)
        d = y - mean
        var = _group_mean(d * d, HEAD)
        yn = d * lax.rsqrt(var + RWKV_GN_EPS) * lnw_ref[...] + lnb_ref[...]
        bonus = _group_mean(r[bi] * k[bi] * rk_ref[...], HEAD) * HEAD * v[bi]
        o_ref[bi] = ((yn + bonus) * g_ref[bi]).astype(o_ref.dtype)


def _rwkv_recurrence(r, lw, k, v, kk, b, g, rk, lnw, lnb, bsz, seq, pw=1):
    c = min(RWKV_CHUNK, seq)
    blk = pl.BlockSpec((bsz, c, BRANCH_W), lambda t: (0, t, 0))
    cst = pl.BlockSpec((1, BRANCH_W), lambda t: (0, 0))
    as3 = lambda a: a.reshape(bsz, seq, BRANCH_W)
    out = pl.pallas_call(
        functools.partial(_rwkv_chunk_kernel, pw=pw),
        grid=(seq // c,),
        in_specs=[blk] * 7 + [cst] * 3,
        out_specs=blk,
        out_shape=jax.ShapeDtypeStruct((bsz, seq, BRANCH_W), BF16),
        scratch_shapes=[pltpu.VMEM((bsz * RWKV_HEADS, HEAD, HEAD), F32)],
        compiler_params=_params(1),
        name="rwkv_chunk",
    )(as3(r), as3(lw), as3(k), as3(v), as3(kk), as3(b), as3(g), rk.reshape(1, -1), lnw.reshape(1, -1),
      lnb.reshape(1, -1))
    return out.reshape(bsz * seq, BRANCH_W)


def _eye(n):
    r = lax.broadcasted_iota(jnp.int32, (n, n), 0)
    c = lax.broadcasted_iota(jnp.int32, (n, n), 1)
    return jnp.where(r == c, 1.0, 0.0).astype(BF16)


def _transpose_bf16(x):
    return _dg(_eye(x.shape[1]), x.astype(BF16), nt=True).astype(BF16)


def _nsa_norm_kernel(q_ref, ks_ref, kw_ref, vs_ref, vw_ref, gq_ref, gs_ref, gw_ref,
                     qo_ref, kso_ref, kwo_ref, vso_ref, vwo_ref):
    for x_ref, g_ref, o_ref in ((q_ref, gq_ref, qo_ref), (ks_ref, gs_ref, kso_ref), (kw_ref, gw_ref, kwo_ref)):
        x = x_ref[...]
        ms = _group_mean(x * x, HEAD)
        o_ref[...] = (x * lax.rsqrt(ms + NORM_EPS) * g_ref[...]).astype(o_ref.dtype)
    vso_ref[...] = _transpose_bf16(vs_ref[...])
    vwo_ref[...] = _transpose_bf16(vw_ref[...])


def _nsa_norm(p_small, qk_gain, bsz, seq, tt=512):
    n = p_small.shape[0]
    tt = min(tt, seq)
    nt = seq // tt
    col = lambda c, w: pl.BlockSpec((tt, w), lambda i: (i, c // w))
    cst = lambda w: pl.BlockSpec((1, w), lambda i: (0, 0))
    out = lambda w: pl.BlockSpec((tt, w), lambda i: (i, 0))
    out_t = pl.BlockSpec((128, tt), lambda i: (i // nt, i % nt))
    gq = (jnp.tile(qk_gain[0], 8) * HEAD ** -0.5).reshape(1, 512)
    gs = jnp.tile(qk_gain[2], 2).reshape(1, 128)
    gw = jnp.tile(qk_gain[3], 2).reshape(1, 128)
    return pl.pallas_call(
        _nsa_norm_kernel,
        grid=(n // tt,),
        in_specs=[col(C_NSAQ, 512), col(C_KS, 128), col(C_KW, 128), col(C_VS, 128), col(C_VW, 128),
                  cst(512), cst(128), cst(128)],
        out_specs=[out(512), out(128), out(128), out_t, out_t],
        out_shape=[jax.ShapeDtypeStruct((n, 512), BF16)] + [jax.ShapeDtypeStruct((n, 128), BF16)] * 2
        + [jax.ShapeDtypeStruct((bsz * 128, seq), BF16)] * 2,
        compiler_params=_params(1),
        name="nsa_norm",
    )(p_small, p_small, p_small, p_small, p_small, gq, gs, gw)


def _gelu_tanh(x):
    return 0.5 * x * (1.0 + jnp.tanh(0.7978845608028654 * (x + 0.044715 * x * x * x)))


def _nsa_compress_kernel(x_ref, pos_ref, w1_ref, b1_ref, w2_ref, gain_ref, o_ref):
    half = x_ref.shape[-1]
    x = x_ref[0, 0, 0].astype(BF16)
    w1 = w1_ref[0].astype(BF16)
    z_lo = jnp.dot(x, w1[:half], preferred_element_type=F32)
    z_hi = jnp.dot(x, w1[half:], preferred_element_type=F32)
    nrow = z_hi.shape[0]
    pos = jnp.broadcast_to(pos_ref[0], (8, 2 * half))
    const = _mm(pos, w1, pa=2)[0:1] + b1_ref[0]
    pre = z_lo + pltpu.roll(z_hi, nrow - 1, axis=0) + const
    out = jnp.dot(_gelu_tanh(pre).astype(BF16), w2_ref[0].astype(BF16), preferred_element_type=F32)
    is_key = pl.program_id(0) == 0
    normed = out * lax.rsqrt(jnp.mean(out * out, axis=-1, keepdims=True) + NORM_EPS) * gain_ref[...]
    o_ref[0, 0, 0] = jnp.where(is_key, normed, out)


def _nsa_compress(kc_vc, cmp_pos, cmp_w1, cmp_b1, cmp_w2, gain):
    _, bsz, hkv, nc, wid = kc_vc.shape
    return pl.pallas_call(
        _nsa_compress_kernel,
        grid=(2, bsz, hkv),
        in_specs=[pl.BlockSpec((1, 1, 1, nc, wid), lambda i, b, h: (i, b, h, 0, 0)),
                  pl.BlockSpec((1, 1, 2 * wid), lambda i, b, h: (i, 0, 0)),
                  pl.BlockSpec((1, 2 * wid, 2 * HEAD), lambda i, b, h: (i, 0, 0)),
                  pl.BlockSpec((1, 1, 2 * HEAD), lambda i, b, h: (i, 0, 0)),
                  pl.BlockSpec((1, 2 * HEAD, HEAD), lambda i, b, h: (i, 0, 0)),
                  pl.BlockSpec((1, HEAD), lambda i, b, h: (0, 0))],
        out_specs=pl.BlockSpec((1, 1, 1, nc, HEAD), lambda i, b, h: (i, b, h, 0, 0)),
        out_shape=jax.ShapeDtypeStruct((2, bsz, hkv, nc, HEAD), F32),
        compiler_params=_params(3),
        name="nsa_compress",
    )(kc_vc, cmp_pos.reshape(2, 1, 2 * wid), cmp_w1, cmp_b1.reshape(2, 1, 2 * HEAD), cmp_w2, gain.reshape(1, HEAD))


def _nsa_cmp_kernel(q_ref, kc_ref, vc_ref, ov_ref, o_ref, sel_ref, *, tq):
    hkv, ncmp = kc_ref.shape[1], kc_ref.shape[2]
    nsel = ov_ref.shape[0]
    g = NSA_GROUP
    rows = g * tq
    start = pl.program_id(1) * tq
    q_all = q_ref[...]
    t_pos = start + (lax.broadcasted_iota(jnp.int32, (ncmp, rows), 1) & (tq - 1))
    c_end = lax.broadcasted_iota(jnp.int32, (ncmp, rows), 0) * COMP_STRIDE + (COMP_L - 1)
    mask = c_end <= t_pos
    sid = lax.broadcasted_iota(jnp.int32, (nsel, tq), 0)
    cur = (start + lax.broadcasted_iota(jnp.int32, (nsel, tq), 1)) // SEL_L
    forced = (sid == 0) | (sid == cur) | (sid == cur - 1)
    k_top = min(SEL_N, nsel)
    outs, sels = [], []
    for h in range(hkv):
        q = _group_rows(q_all, h, g)
        s = jnp.where(mask, _dg(kc_ref[0, h].astype(BF16), q, nt=True), -1e30)
        e = jnp.exp(s - jnp.max(s, axis=0, keepdims=True))
        p = jnp.where(mask, e / jnp.sum(e, axis=0, keepdims=True), 0.0)
        outs.append(jnp.dot(_transpose_bf16(vc_ref[0, h]), p.astype(BF16), preferred_element_type=F32))
        p_sum = p[:, 0:tq]
        for i in range(1, g):
            p_sum = p_sum + p[:, i * tq:(i + 1) * tq]
        imp = _mm(ov_ref[...], p_sum, pb=3)
        val = jnp.where(forced, 1e9, jnp.where(sid <= cur, imp, -1e9))
        rank = jnp.zeros(imp.shape, jnp.int32)
        for j in range(nsel):
            cj = val[j:j + 1, :]
            ahead = (cj > val) | ((cj == val) & (sid > j))
            rank = rank + ahead.astype(jnp.int32)
        sels.append(jnp.where((rank < k_top) & (val > -1e8), 1.0, 0.0))
    o_ref[...] = jnp.concatenate(outs, axis=0)
    sel_ref[...] = jnp.concatenate(sels, axis=0).astype(sel_ref.dtype)


def _group_rows(q_all, h, g):
    return jnp.concatenate([q_all[:, (h * g + i) * HEAD:(h * g + i + 1) * HEAD] for i in range(g)], axis=0)


NSA_Q_TILE = 256


def _nsa_cmp(q_n, k_cmp, v_cmp, overlap_t, bsz, seq, tq=NSA_Q_TILE):
    hkv = k_cmp.shape[1]
    nsel, ncmp = overlap_t.shape
    nt = seq // tq
    rows = NSA_GROUP * tq
    return pl.pallas_call(
        functools.partial(_nsa_cmp_kernel, tq=tq),
        grid=(bsz, nt),
        in_specs=[pl.BlockSpec((tq, BRANCH_W), lambda b, t: (b * nt + t, 0)),
                  pl.BlockSpec((1, hkv, ncmp, HEAD), lambda b, t: (b, 0, 0, 0)),
                  pl.BlockSpec((1, hkv, ncmp, HEAD), lambda b, t: (b, 0, 0, 0)),
                  pl.BlockSpec((nsel, ncmp), lambda b, t: (0, 0))],
        out_specs=[pl.BlockSpec((hkv * HEAD, rows), lambda b, t: (b * nt + t, 0)),
                   pl.BlockSpec((hkv * nsel, tq), lambda b, t: (b * nt + t, 0))],
        out_shape=[jax.ShapeDtypeStruct((bsz * nt * hkv * HEAD, rows), F32),
                   jax.ShapeDtypeStruct((bsz * nt * hkv * nsel, tq), BF16)],
        compiler_params=_params(2),
        name="nsa_cmp",
    )(q_n, k_cmp, v_cmp, overlap_t)


SCORE_MASKED = -1e30
SCORE_FLOOR = -1e20


def _nsa_attn_kernel(q_ref, sel_ref, ks_ref, vs_ref, kw_ref, vw_ref, oc_ref, gl_ref, o_ref, *, tq, kt):
    g = NSA_GROUP
    hkv = ks_ref.shape[1] // HEAD
    seq = ks_ref.shape[0]
    nsel = sel_ref.shape[0] // hkv
    rows = g * tq
    start = pl.program_id(1) * tq
    q_all = q_ref[...]
    sel_all = sel_ref[...]
    qs = [_group_rows(q_all, h, g) for h in range(hkv)]
    sels = [sel_all[h * nsel:(h + 1) * nsel, :] for h in range(hkv)]
    head_cols = lambda x, h: x[:, h * HEAD:(h + 1) * HEAD]
    per_query = lambda x: jnp.concatenate([x] * g, axis=1)
    with_ones = lambda vt: jnp.concatenate([vt, jnp.ones((16, vt.shape[1]), BF16)], axis=0)

    key_pos = lax.broadcasted_iota(jnp.int32, (kt, tq), 0)
    t_pos = start + lax.broadcasted_iota(jnp.int32, (kt, tq), 1)
    blk_of_key = lax.broadcasted_iota(jnp.int32, (kt, nsel), 0) // SEL_L
    blk_id = lax.broadcasted_iota(jnp.int32, (kt, nsel), 1)

    def key_tile(j, carry, causal):
        base = pl.multiple_of(j * kt, kt)
        kb2 = ks_ref[pl.ds(base, kt), :]
        vt2 = vs_ref[:, pl.ds(base, kt)]
        expand = jnp.where(blk_of_key + j * (kt // SEL_L) == blk_id, 1.0, 0.0).astype(BF16)
        out = []
        for h in range(hkv):
            m, acc = carry[h]
            keep = jnp.dot(expand, sels[h], preferred_element_type=F32) > 0.5
            if causal:
                keep = keep & (key_pos + base <= t_pos)
            bias = per_query(jnp.where(keep, 0.0, SCORE_MASKED))
            s = _dg(head_cols(kb2, h), qs[h], nt=True) + bias
            m_new = jnp.maximum(m, jnp.max(s, axis=0, keepdims=True))
            p = jnp.exp(s - m_new).astype(BF16)
            vt = with_ones(vt2[h * HEAD:(h + 1) * HEAD, :])
            acc = jnp.exp(m - m_new) * acc + jnp.dot(vt, p, preferred_element_type=F32)
            out.append((m_new, acc))
        return tuple(out)

    init = tuple((jnp.full((1, rows), SCORE_FLOOR, F32), jnp.zeros((HEAD + 16, rows), F32)) for _ in range(hkv))
    n_full = start // kt
    carry = lax.fori_loop(0, n_full, lambda j, c: key_tile(j, c, False), init)
    carry = key_tile(n_full, carry, True)
    o_sel = [acc[0:HEAD] / acc[HEAD:HEAD + 1] for _, acc in carry]

    span = min(WINDOW + tq, seq)
    wbase = pl.multiple_of(jnp.maximum(start - WINDOW, 0), tq) if seq > span else 0
    kb2 = kw_ref[pl.ds(wbase, span), :]
    vt2 = vw_ref[:, pl.ds(wbase, span)]
    tw = start + lax.broadcasted_iota(jnp.int32, (span, tq), 1)
    wpos = wbase + lax.broadcasted_iota(jnp.int32, (span, tq), 0)
    wbias = per_query(jnp.where((wpos <= tw) & (wpos > tw - WINDOW), 0.0, SCORE_MASKED))
    o_win = []
    for h in range(hkv):
        s = _dg(head_cols(kb2, h), qs[h], nt=True) + wbias
        p = jnp.exp(s - jnp.max(s, axis=0, keepdims=True)).astype(BF16)
        acc = jnp.dot(with_ones(vt2[h * HEAD:(h + 1) * HEAD, :]), p, preferred_element_type=F32)
        o_win.append(acc[0:HEAD] / acc[HEAD:HEAD + 1])

    gate_t = _mm(_eye(128), jax.nn.sigmoid(gl_ref[...]), pb=3, nt=True)
    o_cmp = oc_ref[...]
    blocks = []
    for h in range(hkv):
        for i in range(g):
            cols = slice(i * tq, (i + 1) * tq)
            r = (h * g + i) * 3
            blocks.append(gate_t[r:r + 1] * o_cmp[h * HEAD:(h + 1) * HEAD, cols]
                          + gate_t[r + 1:r + 2] * o_sel[h][:, cols] + gate_t[r + 2:r + 3] * o_win[h][:, cols])
    y_t = jnp.concatenate(blocks, axis=0)
    o_ref[...] = _dg(_eye(tq), y_t.astype(BF16), nt=True).astype(o_ref.dtype)


def _nsa_attn(p_small, q_n, sel_t, ks, vs_t, kw, vw_t, o_cmp_t, bsz, seq, tq=NSA_Q_TILE, kt=512):
    kt = min(kt, seq)
    nt = seq // tq
    hkv = NSA_KV_HEADS
    nsel = seq // SEL_L
    row = lambda w: pl.BlockSpec((tq, w), lambda b, t: (b * nt + t, 0))
    tile = lambda r, w: pl.BlockSpec((r, w), lambda b, t: (b * nt + t, 0))
    k_spec = pl.BlockSpec((seq, hkv * HEAD), lambda b, t: (b, 0))
    vt_spec = pl.BlockSpec((hkv * HEAD, seq), lambda b, t: (b, 0))
    return pl.pallas_call(
        functools.partial(_nsa_attn_kernel, tq=tq, kt=kt),
        grid=(bsz, nt),
        in_specs=[row(BRANCH_W), tile(hkv * nsel, tq), k_spec, vt_spec, k_spec, vt_spec,
                  tile(hkv * HEAD, NSA_GROUP * tq),
                  pl.BlockSpec((tq, 128), lambda b, t: (b * nt + t, C_GL // 128))],
        out_specs=row(BRANCH_W),
        out_shape=jax.ShapeDtypeStruct((bsz * seq, BRANCH_W), BF16),
        compiler_params=_params(2),
        name="nsa_attn",
    )(q_n, sel_t, ks, vs_t, kw, vw_t, o_cmp_t, p_small)


def _nsa(p_small, bsz, seq, qk_gain, cmp_pos, cmp_w1, cmp_b1, cmp_w2):
    hkv = NSA_KV_HEADS
    q_n, ks_n, kw_n, vs_t, vw_t = _nsa_norm(p_small, qk_gain, bsz, seq)
    ncmp = seq // COMP_STRIDE
    grouped = lambda c: p_small[:, c:c + 128].reshape(bsz, ncmp, COMP_STRIDE, hkv, HEAD).transpose(0, 3, 1, 2, 4)
    kc_vc = jnp.stack([grouped(C_KC), grouped(C_VC)]).reshape(2, bsz, hkv, ncmp, COMP_STRIDE * HEAD)
    cmp = _nsa_compress(kc_vc, cmp_pos, cmp_w1, cmp_b1, cmp_w2, qk_gain[1])
    nsel = seq // SEL_L
    c0 = np.arange(ncmp)[None, :] * COMP_STRIDE
    s0 = np.arange(nsel)[:, None] * SEL_L
    overlap_t = np.clip(np.minimum(c0 + COMP_L, s0 + SEL_L) - np.maximum(c0, s0), 0, None) / COMP_L
    o_cmp_t, sel_t = _nsa_cmp(q_n, cmp[0], cmp[1], jnp.asarray(overlap_t, BF16), bsz, seq)
    return _nsa_attn(p_small, q_n, sel_t, ks_n, vs_t, kw_n, vw_t, o_cmp_t, bsz, seq)


HALO = 16


def _conv_pool_kernel(bg_ref, cg_ref, xi_ref, pu_ref, cw_ref, conv_ref, pool_ref, cz_ref, cp_ref):
    tt = xi_ref.shape[0]
    ti = pl.program_id(1)

    @pl.when(ti == 0)
    def _():
        cz_ref[...] = jnp.zeros_like(cz_ref)
        cp_ref[...] = jnp.zeros_like(cp_ref)

    def history(x, carry_ref):
        ext = jnp.concatenate([carry_ref[...], x], axis=0)
        carry_ref[...] = x[tt - HALO:tt, :]
        return ext

    lag = lambda ext, s: pltpu.roll(ext, s, axis=0)
    body = lambda ext: ext[HALO:HALO + tt, :]

    z = cg_ref[...] * xi_ref[...]
    ze = history(z, cz_ref)
    cw = cw_ref[...]
    y = cw[2:3, :] * z + cw[1:2, :] * body(lag(ze, 1)) + cw[0:1, :] * body(lag(ze, 2))
    conv_ref[...] = (bg_ref[...] * y).astype(conv_ref.dtype)

    u = pu_ref[...]
    sums = [history(u, cp_ref)]
    for w in (1, 2, 4, 8):
        sums.append(sums[-1] + lag(sums[-1], w))
    count = (ti * tt + 1 + lax.broadcasted_iota(jnp.int32, (tt, 128), 0)).astype(F32)
    outs = []
    for gi, w in enumerate(POOL_WINDOWS):
        sl = slice(gi * 128, (gi + 1) * 128)
        outs.append(body(sums[gi + 1])[:, sl] / jnp.minimum(count, float(w)) - u[:, sl])
    pool_ref[...] = jnp.concatenate(outs, axis=1).astype(pool_ref.dtype)


def _conv_pool(p_small, bsz, seq, conv_w, tt=512):
    tt = min(tt, seq)
    nt = seq // tt
    n = bsz * seq
    col = lambda c: pl.BlockSpec((tt, 512), lambda b, t: (b * nt + t, c // 512))
    row = pl.BlockSpec((tt, 512), lambda b, t: (b * nt + t, 0))
    return pl.pallas_call(
        _conv_pool_kernel,
        grid=(bsz, nt),
        in_specs=[col(C_CONV), col(C_CONV + 512), col(C_CONV + 1024), col(C_POOL),
                  pl.BlockSpec((8, 512), lambda b, t: (0, 0))],
        out_specs=[row, row],
        out_shape=[jax.ShapeDtypeStruct((n, 512), BF16)] * 2,
        scratch_shapes=[pltpu.VMEM((HALO, 512), F32)] * 2,
        compiler_params=_params(2),
        name="conv_pool",
    )(p_small, p_small, p_small, p_small, jnp.pad(conv_w, ((0, 8 - conv_w.shape[0]), (0, 0))))


def _mem_attn_kernel(q_ref, kv_ref, gq_ref, gk_ref, o_ref, *, scale):
    outs = []
    for h in range(MEM_HEADS):
        sl = slice(h * MEM_HEAD, (h + 1) * MEM_HEAD)
        q = q_ref[:, sl]
        q = q * lax.rsqrt(jnp.mean(q * q, axis=-1, keepdims=True) + NORM_EPS) * gq_ref[...]
        k = kv_ref[0, :, sl]
        k = k * lax.rsqrt(jnp.mean(k * k, axis=-1, keepdims=True) + NORM_EPS) * gk_ref[...]
        v = kv_ref[0, :, BRANCH_W + h * MEM_HEAD:BRANCH_W + (h + 1) * MEM_HEAD]
        s = _dg(q.astype(BF16), k.astype(BF16), nt=True) * scale
        e = jnp.exp(s - jnp.max(s, axis=-1, keepdims=True))
        p = e / jnp.sum(e, axis=-1, keepdims=True)
        outs.append(jnp.dot(p.astype(BF16), v.astype(BF16), preferred_element_type=F32))
    o_ref[...] = jnp.concatenate(outs, axis=1).astype(o_ref.dtype)


def _mem_attn(p_small, kv, bsz, seq, qk_gain, tq=512):
    tq = min(tq, seq)
    nt = seq // tq
    mlen = kv.shape[1]
    return pl.pallas_call(
        functools.partial(_mem_attn_kernel, scale=MEM_HEAD ** -0.5),
        grid=(bsz, nt),
        in_specs=[pl.BlockSpec((tq, 512), lambda b, t: (b * nt + t, C_MEM // 512)),
                  pl.BlockSpec((1, mlen, 2 * BRANCH_W), lambda b, t: (b, 0, 0)),
                  pl.BlockSpec((1, MEM_HEAD), lambda b, t: (0, 0)),
                  pl.BlockSpec((1, MEM_HEAD), lambda b, t: (0, 0))],
        out_specs=pl.BlockSpec((tq, 512), lambda b, t: (b * nt + t, 0)),
        out_shape=jax.ShapeDtypeStruct((bsz * seq, 512), BF16),
        compiler_params=_params(2),
        name="mem_attn",
    )(p_small, kv, qk_gain[0].reshape(1, -1), qk_gain[1].reshape(1, -1))


def _merge_kernel(h_ref, *refs):
    wg_refs, y_refs = refs[0:5], refs[5:9]
    wb_ref, pooled_ref, wp_ref, ps_ref, o_ref, wgc_ref, wbc_ref = refs[9:]

    @pl.when(pl.program_id(1) == 0)
    def _():
        for i in range(5):
            wgc_ref[i] = wg_refs[i][...].astype(BF16)
        wbc_ref[...] = wb_ref[...].astype(BF16)

    h = h_ref[...]
    gate = lambda i: jax.nn.sigmoid(jnp.dot(h, wgc_ref[i], preferred_element_type=F32))
    z_pool = jnp.dot(pooled_ref[...], wp_ref[0].astype(BF16), preferred_element_type=F32) * ps_ref[...]
    acc = gate(4) * z_pool
    for i in range(4):
        acc = acc + gate(i) * jnp.dot(y_refs[i][...], wbc_ref[i], preferred_element_type=F32)
    o_ref[...] = acc.astype(o_ref.dtype)


def _merge(h, w_gate, ys, w_branch, pooled, pool_w, pool_scale, tm=512):
    n = h.shape[0]
    tm = min(tm, n)
    tn = 512
    nj = D_MODEL // tn
    once = pl.Buffered(1)
    gate_spec = lambda g: pl.BlockSpec((D_MODEL, tn), lambda j, i: (0, g * nj + j), pipeline_mode=once)
    y_spec = pl.BlockSpec((tm, BRANCH_W), lambda j, i: (i, 0))
    return pl.pallas_call(
        _merge_kernel,
        grid=(nj, n // tm),
        in_specs=[pl.BlockSpec((tm, D_MODEL), lambda j, i: (i, 0))] + [gate_spec(g) for g in range(5)]
        + [y_spec] * 4
        + [pl.BlockSpec((4, BRANCH_W, tn), lambda j, i: (0, 0, j), pipeline_mode=once),
           pl.BlockSpec((tm, 128), lambda j, i: (i, j)),
           pl.BlockSpec((1, 128, tn), lambda j, i: (j, 0, 0)),
           pl.BlockSpec((1, tn), lambda j, i: (0, j))],
        out_specs=pl.BlockSpec((tm, tn), lambda j, i: (i, j)),
        out_shape=jax.ShapeDtypeStruct((n, D_MODEL), BF16),
        scratch_shapes=[pltpu.VMEM((5, D_MODEL, tn), BF16), pltpu.VMEM((4, BRANCH_W, tn), BF16)],
        compiler_params=_params(2),
        name="merge",
    )(h, *([w_gate] * 5), *ys, w_branch, pooled, pool_w, pool_scale.reshape(1, -1))


def _pack_w_in(w, vres_w):
    pad = lambda a, width: jnp.pad(a, ((0, 0), (0, width - a.shape[1])))
    nsa = w[:, 1984:3288]
    vres_cols = pad(vres_w, 128) if vres_w is not None else jnp.zeros((D_MODEL, 128), w.dtype)
    small = jnp.concatenate([
        w[:, 0:1536], pad(w[:, 1536:1632], 128), pad(w[:, 1632:1728], 128), w[:, 1728:1984],
        nsa[:, 0:512], w[:, 3288:4824], w[:, 4824:5336], w[:, 5336:5848],
        nsa[:, 512:1280], pad(nsa[:, 1280:1304], 128), vres_cols], axis=1)
    return small, w[:, 5848:]


def kernel(x, mem, norm_mix, norm_ffn, norm_mem, w_in, rwkv_mu, rwkv_w0, rwkv_w2, rwkv_a0, rwkv_a2, rwkv_g2, rwkv_kk, rwkv_ka, rwkv_rk, rwkv_ln_w, rwkv_ln_b, vres_in, vres_mu, vres_v0, vres_up, nsa_qk_gain, nsa_cmp_pos, nsa_cmp_w1, nsa_cmp_b1, nsa_cmp_w2, conv_w, pool_w, pool_scale, mem_wkv, mem_qk_gain, w_branch, w_out, ffn_w1, ffn_w3, ffn_w2, moe_router, moe_w1, moe_w3, moe_w2):
    bsz, seq, d = x.shape
    n = bsz * seq
    depth = w_in.shape[0]
    mlen = mem.shape[1]
    xf = x.reshape(n, d)
    memf = mem.reshape(bsz * mlen, d)
    v_first = None
    for l in range(depth):
        h = _rmsnorm(xf, norm_mix[l])
        w_small, w_gate = _pack_w_in(w_in[l], vres_in[l - 1] if l > 0 else None)
        p_small = _matmul(h, w_small, tm=1024, tn=1024, name="in_proj")
        vres = (vres_mu[l - 1], vres_v0[l - 1], vres_up[l - 1], v_first) if l > 0 else None
        r, lw, k, v, kk, b, g = _rwkv_prep(p_small, bsz, seq, rwkv_mu[l], rwkv_w0[l], rwkv_w2[l], rwkv_a0[l],
                                           rwkv_a2[l], rwkv_g2[l], rwkv_kk[l], rwkv_ka[l], vres)
        if l == 0:
            v_first = v
        y_rwkv = _rwkv_recurrence(r, lw, k, v, kk, b, g, rwkv_rk[l], rwkv_ln_w[l], rwkv_ln_b[l], bsz, seq)
        y_nsa = _nsa(p_small, bsz, seq, nsa_qk_gain[l], nsa_cmp_pos[l], nsa_cmp_w1[l], nsa_cmp_b1[l], nsa_cmp_w2[l])
        y_conv, pooled = _conv_pool(p_small, bsz, seq, conv_w[l])
        mem_n = _rmsnorm(memf, norm_mem[l])
        kv = _matmul(mem_n, mem_wkv[l], tm=512, tn=512, name="mem_kv").reshape(bsz, mlen, 2 * BRANCH_W)
        y_mem = _mem_attn(p_small, kv, bsz, seq, mem_qk_gain[l])
        merged = _merge(h, w_gate, (y_rwkv, y_nsa, y_conv, y_mem), w_branch[l], pooled, pool_w[l], pool_scale[l])
        xf = _matmul(merged, w_out[l], tm=1024, tn=1024, res=xf, name="out_proj")

        if l % 2 == 0:
            h2 = _rmsnorm(xf, norm_ffn[l])
            e = l // 2
            act = _swiglu_up(h2, ffn_w1[e], ffn_w3[e])
            xf = _matmul(act, ffn_w2[e], tm=512, tn=512, res=xf, name="ffn_down")
        else:
            e = l // 2
            xf = _moe(xf, norm_ffn[l], moe_router[e], moe_w1[e], moe_w3[e], moe_w2[e])
    return xf.reshape(bsz, seq, d)
```

```python
import functools

import jax
import jax.numpy as jnp
import numpy as np
from jax import lax
from jax.experimental import pallas as pl
from jax.experimental.pallas import tpu as pltpu

F32 = jnp.float32
BF16 = jnp.bfloat16

D_MODEL = 2048
BRANCH_W = 512
HEAD = 64
RWKV_HEADS = 8
NSA_KV_HEADS = 2
NSA_GROUP = 4
COMP_L = 32
COMP_STRIDE = 16
SEL_L = 64
SEL_N = 16
WINDOW = 512
MEM_HEADS = 4
MEM_HEAD = 128
POOL_WINDOWS = (2, 4, 8, 16)
N_EXPERTS = 8
NORM_EPS = 1e-6
RWKV_GN_EPS = 64e-5
RWKV_CHUNK = 64

VMEM_LIMIT_BYTES = 56 * 1024 * 1024

C_RWKV, C_NSAQ, C_CONV, C_POOL, C_MEM = 0, 2048, 2560, 4096, 4608
C_KC, C_VC, C_KS, C_VS, C_KW, C_VW, C_GL, C_VRES = 5120, 5248, 5376, 5504, 5632, 5760, 5888, 6016
N_SMALL = 6144


def _params(n_axes):
    return pltpu.CompilerParams(dimension_semantics=("arbitrary",) * n_axes,
                                vmem_limit_bytes=VMEM_LIMIT_BYTES)


def _split(a, n):
    pieces, r = [], a
    for i in range(n):
        p = r.astype(BF16)
        pieces.append(p)
        if i + 1 < n:
            r = r - p.astype(F32)
    return pieces


def _dg(a, b, nt):
    dims = (((1,), (1,)), ((), ())) if nt else (((1,), (0,)), ((), ()))
    return lax.dot_general(a, b, dims, preferred_element_type=F32)


def _mm(a, b, pa=1, pb=1, nt=False):
    sa = _split(a, pa) if a.dtype != BF16 else [a]
    sb = _split(b, pb) if b.dtype != BF16 else [b]
    order = max(len(sa), len(sb))
    acc = None
    for i, x in enumerate(sa):
        for j, y in enumerate(sb):
            if i + j < order:
                t = _dg(x, y, nt)
                acc = t if acc is None else acc + t
    return acc


def _head_sum_matrix(width, head):
    r = lax.broadcasted_iota(jnp.int32, (width, width), 0) // head
    c = lax.broadcasted_iota(jnp.int32, (width, width), 1) // head
    return jnp.where(r == c, 1.0, 0.0).astype(BF16)


def _group_mean(x, head):
    hs = _head_sum_matrix(x.shape[-1], head)
    return _mm(x, hs, pa=3) * (1.0 / head)


def _rmsnorm_kernel(x_ref, g_ref, o_ref):
    x = x_ref[...]
    y = x * lax.rsqrt(jnp.mean(x * x, axis=-1, keepdims=True) + NORM_EPS)
    o_ref[...] = (y * g_ref[...]).astype(o_ref.dtype)


def _rmsnorm(x, g, tm=512):
    m, d = x.shape
    tm = min(tm, m)
    return pl.pallas_call(
        _rmsnorm_kernel,
        grid=(m // tm,),
        in_specs=[pl.BlockSpec((tm, d), lambda i: (i, 0)), pl.BlockSpec((1, d), lambda i: (0, 0))],
        out_specs=pl.BlockSpec((tm, d), lambda i: (i, 0)),
        out_shape=jax.ShapeDtypeStruct((m, d), BF16),
        compiler_params=_params(1),
        name="rmsnorm",
    )(x, g.reshape(1, d))


def _rmsnorm_router_kernel(x_ref, g_ref, wr_ref, o_ref, comb_ref, assign_ref, count_ref, carry_ref):
    tm = x_ref.shape[0]

    @pl.when(pl.program_id(0) == 0)
    def _():
        carry_ref[...] = jnp.zeros_like(carry_ref)

    x = x_ref[...]
    y = x * lax.rsqrt(jnp.mean(x * x, axis=-1, keepdims=True) + NORM_EPS) * g_ref[...]
    o_ref[...] = y
    logits = _mm(y, wr_ref[...], pa=3, pb=3)
    lane = lax.broadcasted_iota(jnp.int32, logits.shape, 1)
    neg = jnp.float32(-3e38)
    lg = jnp.where(lane < N_EXPERTS, logits, neg)
    m1 = jnp.max(lg, axis=-1, keepdims=True)
    i1 = jnp.min(jnp.where(lg == m1, lane, 1 << 20), axis=-1, keepdims=True)
    lg2 = jnp.where(lane == i1, neg, lg)
    m2 = jnp.max(lg2, axis=-1, keepdims=True)
    i2 = jnp.min(jnp.where(lg2 == m2, lane, 1 << 20), axis=-1, keepdims=True)
    e2 = jnp.exp(m2 - m1)
    w1 = 1.0 / (1.0 + e2)
    w2 = e2 / (1.0 + e2)
    comb_ref[...] = jnp.where(lane == i1, w1, 0.0) + jnp.where(lane == i2, w2, 0.0)
    assign = jnp.where((lane == i1) | (lane == i2), 1.0, 0.0)
    assign_ref[...] = assign
    row = lax.broadcasted_iota(jnp.int32, (tm, tm), 0)
    col = lax.broadcasted_iota(jnp.int32, (tm, tm), 1)
    tril = jnp.where(row >= col, 1.0, 0.0).astype(BF16)
    count = jnp.dot(tril, assign.astype(BF16), preferred_element_type=F32) + carry_ref[0:1, :]
    count_ref[...] = count
    carry_ref[0:1, :] = count[tm - 1:tm, :]


def _rmsnorm_router(x, g, router, tm=512):
    m, d = x.shape
    tm = min(tm, m)
    wr = jnp.pad(router, ((0, 0), (0, 128 - router.shape[1])))
    lanes = pl.BlockSpec((tm, 128), lambda i: (i, 0))
    return pl.pallas_call(
        _rmsnorm_router_kernel,
        grid=(m // tm,),
        in_specs=[pl.BlockSpec((tm, d), lambda i: (i, 0)), pl.BlockSpec((1, d), lambda i: (0, 0)),
                  pl.BlockSpec((d, 128), lambda i: (0, 0))],
        out_specs=[pl.BlockSpec((tm, d), lambda i: (i, 0)), lanes, lanes, lanes],
        out_shape=[jax.ShapeDtypeStruct((m, d), F32)] + [jax.ShapeDtypeStruct((m, 128), F32)] * 3,
        scratch_shapes=[pltpu.VMEM((8, 128), F32)],
        compiler_params=_params(1),
        name="rmsnorm_router",
    )(x, g.reshape(1, d), wr)


MOE_TILE = 256


def _row_copy(src_hbm, row, dst, slot, sem):
    return pltpu.make_async_copy(src_hbm.at[pl.ds(row, 1), :], dst.at[pl.ds(slot, 1), :], sem)


ROW_DMA_THREADS = 2


def _gather_rows_kernel(idx_ref, nxt_ref, src_hbm, o_ref, buf_ref, sem):
    tm = o_ref.shape[0]

    i = pl.program_id(0)
    slot = i % 2

    def issue(ids_ref, to_slot):
        def body(q, carry):
            for t in range(ROW_DMA_THREADS):
                r = q * ROW_DMA_THREADS + t
                _row_copy(src_hbm, ids_ref[0, 0, r], buf_ref.at[to_slot], r, sem.at[to_slot]).start(priority=t)
            return carry
        lax.fori_loop(0, tm // ROW_DMA_THREADS, body, 0)

    @pl.when(i == 0)
    def _():
        issue(idx_ref, 0)

    @pl.when(i + 1 < pl.num_programs(0))
    def _():
        issue(nxt_ref, 1 - slot)

    def wait(r, carry):
        _row_copy(src_hbm, 0, buf_ref.at[slot], r, sem.at[slot]).wait()
        return carry

    lax.fori_loop(0, tm, wait, 0)
    o_ref[...] = buf_ref[slot].astype(o_ref.dtype)


def _gather_rows(src, idx, tm=MOE_TILE):
    r = idx.shape[0]
    d = src.shape[1]
    n_tiles = r // tm
    ids = idx.reshape(n_tiles, 1, tm)
    return pl.pallas_call(
        _gather_rows_kernel,
        grid=(n_tiles,),
        in_specs=[pl.BlockSpec((1, 1, tm), lambda i: (i, 0, 0), memory_space=pltpu.SMEM),
                  pl.BlockSpec((1, 1, tm), lambda i: (jnp.minimum(i + 1, n_tiles - 1), 0, 0),
                               memory_space=pltpu.SMEM),
                  pl.BlockSpec(memory_space=pl.ANY)],
        out_specs=pl.BlockSpec((tm, d), lambda i: (i, 0)),
        out_shape=jax.ShapeDtypeStruct((r, d), BF16),
        scratch_shapes=[pltpu.VMEM((2, tm, d), F32), pltpu.SemaphoreType.DMA((2,))],
        compiler_params=_params(1),
        name="moe_gather",
    )(ids, ids, src)


def _expert_changed(te_ref, i):
    return (i == 0) | (te_ref[i] != te_ref[jnp.maximum(i - 1, 0)])


def _moe_up_kernel(te_ref, na_ref, x_ref, w1_ref, w3_ref, o_ref, w1c_ref, w3c_ref):
    i = pl.program_id(1)
    active = i < na_ref[0]

    @pl.when(active & _expert_changed(te_ref, i))
    def _():
        w1c_ref[...] = w1_ref[0].astype(BF16)
        w3c_ref[...] = w3_ref[0].astype(BF16)

    @pl.when(active)
    def _():
        x = x_ref[...]
        a = jnp.dot(x, w1c_ref[...], preferred_element_type=F32)
        b = jnp.dot(x, w3c_ref[...], preferred_element_type=F32)
        o_ref[...] = (a * jax.nn.sigmoid(a) * b).astype(o_ref.dtype)

    @pl.when(jnp.logical_not(active))
    def _():
        o_ref[...] = jnp.zeros_like(o_ref)


def _moe_down_kernel(te_ref, na_ref, a_ref, w_ref, o_ref, wc_ref):
    i = pl.program_id(1)
    active = i < na_ref[0]

    @pl.when(active & _expert_changed(te_ref, i))
    def _():
        wc_ref[...] = w_ref[0].astype(BF16)

    @pl.when(active)
    def _():
        o_ref[...] = jnp.dot(a_ref[...], wc_ref[...], preferred_element_type=F32)

    @pl.when(jnp.logical_not(active))
    def _():
        o_ref[...] = jnp.zeros_like(o_ref)


def _moe_experts(x_sorted, tile_expert, n_active, w1, w3, w2, tm=MOE_TILE):
    rows, d = x_sorted.shape
    ff = w1.shape[2]
    n_tiles = rows // tm
    tn_up, tn_down = ff // 2, d // 2
    once = pl.Buffered(1)
    act = pl.pallas_call(
        _moe_up_kernel,
        grid_spec=pltpu.PrefetchScalarGridSpec(
            num_scalar_prefetch=2,
            grid=(ff // tn_up, n_tiles),
            in_specs=[pl.BlockSpec((tm, d), lambda j, i, te, na: (i, 0)),
                      pl.BlockSpec((1, d, tn_up), lambda j, i, te, na: (te[i], 0, j)),
                      pl.BlockSpec((1, d, tn_up), lambda j, i, te, na: (te[i], 0, j), pipeline_mode=once)],
            out_specs=pl.BlockSpec((tm, tn_up), lambda j, i, te, na: (i, j)),
            scratch_shapes=[pltpu.VMEM((d, tn_up), BF16)] * 2),
        out_shape=jax.ShapeDtypeStruct((rows, ff), BF16),
        compiler_params=_params(2),
        name="moe_up",
    )(tile_expert, n_active, x_sorted, w1, w3)
    return pl.pallas_call(
        _moe_down_kernel,
        grid_spec=pltpu.PrefetchScalarGridSpec(
            num_scalar_prefetch=2,
            grid=(d // tn_down, n_tiles),
            in_specs=[pl.BlockSpec((tm, ff), lambda j, i, te, na: (i, 0)),
                      pl.BlockSpec((1, ff, tn_down), lambda j, i, te, na: (te[i], 0, j))],
            out_specs=pl.BlockSpec((tm, tn_down), lambda j, i, te, na: (i, j)),
            scratch_shapes=[pltpu.VMEM((ff, tn_down), BF16)]),
        out_shape=jax.ShapeDtypeStruct((rows, d), F32),
        compiler_params=_params(2),
        name="moe_down",
    )(tile_expert, n_active, act, w2)


def _moe_combine_kernel(pa_ref, pb_ref, na_ref, nb_ref, y_hbm, x_ref, w_ref, o_ref, buf_ref, sem):
    tm = x_ref.shape[0]
    i = pl.program_id(0)
    slot = i % 2

    def issue(a_ref, b_ref, to_slot):
        def body(r, carry):
            _row_copy(y_hbm, a_ref[0, 0, r], buf_ref.at[to_slot, 0], r, sem.at[to_slot]).start(priority=0)
            _row_copy(y_hbm, b_ref[0, 0, r], buf_ref.at[to_slot, 1], r, sem.at[to_slot]).start(priority=1)
            return carry
        lax.fori_loop(0, tm, body, 0)

    @pl.when(i == 0)
    def _():
        issue(pa_ref, pb_ref, 0)

    @pl.when(i + 1 < pl.num_programs(0))
    def _():
        issue(na_ref, nb_ref, 1 - slot)

    def wait(r, carry):
        _row_copy(y_hbm, 0, buf_ref.at[slot, 0], r, sem.at[slot]).wait()
        _row_copy(y_hbm, 0, buf_ref.at[slot, 1], r, sem.at[slot]).wait()
        return carry

    lax.fori_loop(0, tm, wait, 0)
    w = w_ref[...]
    o_ref[...] = x_ref[...] + w[:, 0:1] * buf_ref[slot, 0] + w[:, 1:2] * buf_ref[slot, 1]


def _moe_combine(x, y_sorted, pos_a, pos_b, weights, tm=MOE_TILE):
    n, d = x.shape
    n_tiles = n // tm
    cur = pl.BlockSpec((1, 1, tm), lambda i: (i, 0, 0), memory_space=pltpu.SMEM)
    nxt = pl.BlockSpec((1, 1, tm), lambda i: (jnp.minimum(i + 1, n_tiles - 1), 0, 0), memory_space=pltpu.SMEM)
    pa, pb = pos_a.reshape(n_tiles, 1, tm), pos_b.reshape(n_tiles, 1, tm)
    return pl.pallas_call(
        _moe_combine_kernel,
        grid=(n_tiles,),
        in_specs=[cur, cur, nxt, nxt, pl.BlockSpec(memory_space=pl.ANY),
                  pl.BlockSpec((tm, d), lambda i: (i, 0)), pl.BlockSpec((tm, 128), lambda i: (i, 0))],
        out_specs=pl.BlockSpec((tm, d), lambda i: (i, 0)),
        out_shape=jax.ShapeDtypeStruct((n, d), F32),
        scratch_shapes=[pltpu.VMEM((2, 2, tm, d), F32), pltpu.SemaphoreType.DMA((2,))],
        compiler_params=_params(1),
        name="moe_combine",
    )(pa, pb, pa, pb, y_sorted, x, weights)


def _moe(x, norm_g, router, w1, w3, w2):
    n, d = x.shape
    tm = min(MOE_TILE, n)
    h2, comb, assign, count = _rmsnorm_router(x, norm_g, router)
    assigned = assign[:, :N_EXPERTS] > 0.5
    count = count[:, :N_EXPERTS].astype(jnp.int32)
    total = count[-1]
    padded = (total + tm - 1) // tm * tm
    ends = jnp.cumsum(padded)
    starts = ends - padded
    dest = starts[None, :] + count - 1
    rows = (2 * n // tm + N_EXPERTS) * tm
    tile_start = jnp.arange(rows // tm, dtype=jnp.int32) * tm
    tile_expert = jnp.minimum(jnp.sum(tile_start[:, None] >= ends[None, :], axis=1), N_EXPERTS - 1).astype(jnp.int32)
    n_active = (ends[-1:] // tm).astype(jnp.int32)
    offset = tile_start[:, None] + jnp.arange(tm, dtype=jnp.int32)[None, :] - starts[tile_expert][:, None]
    src = jnp.sum(count.T[tile_expert][:, None, :] <= offset[:, :, None], axis=-1, dtype=jnp.int32)
    src = jnp.minimum(src, n - 1).reshape(rows)
    first = jnp.argmax(assigned, axis=1)
    second = N_EXPERTS - 1 - jnp.argmax(assigned[:, ::-1], axis=1)
    take = lambda a, i: jnp.take_along_axis(a, i[:, None], axis=1)[:, 0]
    pos_a, pos_b = take(dest, first), take(dest, second)
    weights = jnp.pad(jnp.stack([take(comb, first), take(comb, second)], axis=1), ((0, 0), (0, 126)))

    x_sorted = _gather_rows(h2, src, tm)
    y_sorted = _moe_experts(x_sorted, tile_expert, n_active, w1, w3, w2, tm)
    return _moe_combine(x, y_sorted, pos_a, pos_b, weights, tm)


def _matmul_kernel(*refs, has_res, cached):
    a_ref, w_ref = refs[0], refs[1]
    res_ref = refs[2] if has_res else None
    o_ref = refs[3] if has_res else refs[2]
    if cached:
        wc_ref = refs[-1]

        @pl.when(pl.program_id(1) == 0)
        def _():
            wc_ref[...] = w_ref[...].astype(BF16)

        w = wc_ref[...]
    else:
        w = w_ref[...]
    out = jnp.dot(a_ref[...], w, preferred_element_type=F32)
    if has_res:
        out = out + res_ref[...]
    o_ref[...] = out.astype(o_ref.dtype)


def _matmul(a, w, *, tm, tn, out_dtype=F32, res=None, name="matmul"):
    m, kdim = a.shape
    n = w.shape[1]
    tm, tn = min(tm, m), min(tn, n)
    assert m % tm == 0 and n % tn == 0
    cached = w.dtype != BF16
    in_specs = [pl.BlockSpec((tm, kdim), lambda j, i: (i, 0)), pl.BlockSpec((kdim, tn), lambda j, i: (0, j))]
    args = [a, w]
    if res is not None:
        in_specs.append(pl.BlockSpec((tm, tn), lambda j, i: (i, j)))
        args.append(res)
    return pl.pallas_call(
        functools.partial(_matmul_kernel, has_res=res is not None, cached=cached),
        grid=(n // tn, m // tm),
        in_specs=in_specs,
        out_specs=pl.BlockSpec((tm, tn), lambda j, i: (i, j)),
        out_shape=jax.ShapeDtypeStruct((m, n), out_dtype),
        scratch_shapes=[pltpu.VMEM((kdim, tn), BF16)] if cached else [],
        compiler_params=_params(2),
        name=name,
    )(*args)


def _swiglu_up_kernel(h_ref, w1_ref, w3_ref, o_ref):
    h = h_ref[...]
    a = jnp.dot(h, w1_ref[...].astype(BF16), preferred_element_type=F32)
    b = jnp.dot(h, w3_ref[...].astype(BF16), preferred_element_type=F32)
    o_ref[...] = (a * jax.nn.sigmoid(a) * b).astype(o_ref.dtype)


def _swiglu_up(h, w1, w3, tm=1024, tn=512):
    m, kdim = h.shape
    n = w1.shape[1]
    tm, tn = min(tm, m), min(tn, n)
    assert m % tm == 0 and n % tn == 0
    return pl.pallas_call(
        _swiglu_up_kernel,
        grid=(n // tn, m // tm),
        in_specs=[pl.BlockSpec((tm, kdim), lambda j, i: (i, 0)),
                  pl.BlockSpec((kdim, tn), lambda j, i: (0, j)),
                  pl.BlockSpec((kdim, tn), lambda j, i: (0, j))],
        out_specs=pl.BlockSpec((tm, tn), lambda j, i: (i, j)),
        out_shape=jax.ShapeDtypeStruct((m, n), BF16),
        compiler_params=_params(2),
        name="swiglu_up",
    )(h, w1, w3)


def _shift_rows(x, carry_row):
    rolled = pltpu.roll(x, 1, axis=0)
    row = lax.broadcasted_iota(jnp.int32, x.shape, 0)
    return jnp.where(row == 0, carry_row, rolled)


def _rwkv_prep_kernel(*refs, has_vres):
    (u_ref, mu_ref, w0_ref, w2_ref, a0_ref, a2_ref, g2_ref, kkw_ref, kaw_ref) = refs[:9]
    pos = 9
    if has_vres:
        vd_ref, vmu_ref, v0_ref, vup_ref, vfirst_ref = refs[pos:pos + 5]
        pos += 5
    r_ref, lw_ref, k_ref, v_ref, kk_ref, b_ref, g_ref = refs[pos:pos + 7]
    pos += 7
    cu_ref = refs[pos]
    cv_ref = refs[pos + 1] if has_vres else None
    tt = u_ref.shape[0]

    @pl.when(pl.program_id(1) == 0)
    def _():
        cu_ref[...] = jnp.zeros_like(cu_ref)
        if has_vres:
            cv_ref[...] = jnp.zeros_like(cv_ref)

    u = u_ref[...]
    prev = _shift_rows(u, cu_ref[0:1, :])
    cu_ref[0:1, :] = u[tt - 1:tt, :]
    uf = u + (prev - u) * mu_ref[...]
    r, k, v = uf[:, 0:512], uf[:, 512:1024], uf[:, 1024:1536]
    wd, ad, gd = uf[:, 1536:1664], uf[:, 1664:1792], uf[:, 1792:2048]

    x = w0_ref[...] + _mm(jnp.tanh(wd), w2_ref[...], 2, 2)
    softplus = jnp.maximum(-x, 0.0) + jnp.log(1.0 + jnp.exp(-jnp.abs(x)))
    lw_ref[...] = -jnp.exp(-softplus - 0.5)
    a = jax.nn.sigmoid(a0_ref[...] + _mm(ad, a2_ref[...], 2, 2))
    g_ref[...] = _mm(jax.nn.sigmoid(gd), g2_ref[...], 2, 2)
    if has_vres:
        vd = vd_ref[...]
        vprev = _shift_rows(vd, cv_ref[0:1, :])
        cv_ref[0:1, :] = vd[tt - 1:tt, :]
        vdf = vd + (vprev - vd) * vmu_ref[...]
        v = v + (vfirst_ref[...] - v) * jax.nn.sigmoid(v0_ref[...] + _mm(vdf, vup_ref[...], 2, 2))
    kk = k * kkw_ref[...]
    ss = _group_mean(kk * kk, HEAD) * HEAD
    kk = kk * lax.rsqrt(jnp.maximum(ss, 1e-24))
    r_ref[...] = r
    k_ref[...] = k * (1.0 + (a - 1.0) * kaw_ref[...])
    v_ref[...] = v
    kk_ref[...] = kk
    b_ref[...] = kk * a


def _rwkv_prep(p_small, bsz, seq, mu, w0, w2, a0, a2, g2, kkw, kaw, vres, tt=256):
    tt = min(tt, seq)
    nt = seq // tt
    n = bsz * seq
    row = lambda b, t: (b * nt + t, 0)
    const = lambda b, t: (0, 0)
    pad_rows = lambda w: jnp.pad(w, ((0, 128 - w.shape[0]), (0, 0)))
    mu_p = jnp.concatenate([mu[:1536], jnp.pad(mu[1536:1632], (0, 32)), jnp.pad(mu[1632:1728], (0, 32)), mu[1728:]])
    vec = lambda a: a.reshape(1, -1)
    args = [p_small, vec(mu_p), vec(w0), pad_rows(w2), vec(a0), pad_rows(a2), g2, vec(kkw), vec(kaw)]
    in_specs = [pl.BlockSpec((tt, 2048), row), pl.BlockSpec((1, 2048), const), pl.BlockSpec((1, 512), const),
                pl.BlockSpec((128, 512), const), pl.BlockSpec((1, 512), const), pl.BlockSpec((128, 512), const),
                pl.BlockSpec((256, 512), const), pl.BlockSpec((1, 512), const), pl.BlockSpec((1, 512), const)]
    scratch = [pltpu.VMEM((8, 2048), F32)]
    if vres is not None:
        vmu, v0, vup, vfirst = vres
        args += [p_small, vec(jnp.pad(vmu, (0, 64))), vec(v0), pad_rows(vup), vfirst]
        in_specs += [pl.BlockSpec((tt, 128), lambda b, t: (b * nt + t, C_VRES // 128)), pl.BlockSpec((1, 128), const),
                     pl.BlockSpec((1, 512), const), pl.BlockSpec((128, 512), const), pl.BlockSpec((tt, 512), row)]
        scratch.append(pltpu.VMEM((8, 128), F32))
    return pl.pallas_call(
        functools.partial(_rwkv_prep_kernel, has_vres=vres is not None),
        grid=(bsz, nt),
        in_specs=in_specs,
        out_specs=[pl.BlockSpec((tt, 512), row)] * 7,
        out_shape=[jax.ShapeDtypeStruct((n, 512), F32)] * 7,
        scratch_shapes=scratch,
        compiler_params=_params(2),
        name="rwkv_prep",
    )(*args)


def _rwkv_chunk_kernel(r_ref, lw_ref, k_ref, v_ref, kk_ref, b_ref, g_ref, rk_ref, lnw_ref, lnb_ref, o_ref, s_ref,
                       *, pw):
    nb, c, _ = r_ref.shape

    @pl.when(pl.program_id(0) == 0)
    def _():
        s_ref[...] = jnp.zeros_like(s_ref)

    row = lax.broadcasted_iota(jnp.int32, (c, c), 0)
    col = lax.broadcasted_iota(jnp.int32, (c, c), 1)
    tril_incl = jnp.where(row >= col, 1.0, 0.0).astype(BF16)
    strict = row > col
    incl = row >= col
    eye = jnp.where(row == col, 1.0, 0.0)
    hrow = lax.broadcasted_iota(jnp.int32, (HEAD, HEAD), 0)
    hcol = lax.broadcasted_iota(jnp.int32, (HEAD, HEAD), 1)
    eye_h = jnp.where(hrow == hcol, 1.0, 0.0).astype(BF16)
    mm = functools.partial(_mm, pa=pw, pb=pw)
    transpose = lambda a, pieces: _mm(eye_h, a, pb=pieces, nt=True)

    heads = [(bi, h) for bi in range(nb) for h in range(RWKV_HEADS)]
    per_head = lambda full: [full[bi][:, h * HEAD:(h + 1) * HEAD] for bi, h in heads]
    each = lambda f, *lists: [f(*vals) for vals in zip(*lists)]

    r, lw, k, v, kk, b = ([ref[bi] for bi in range(nb)] for ref in (r_ref, lw_ref, k_ref, v_ref, kk_ref, b_ref))
    cum = [_mm(tril_incl, x, pb=3) for x in lw]
    g_inv = [jnp.exp(-x) for x in cum]
    a_t = per_head(each(lambda kk_, c_, lw_: kk_ * jnp.exp(c_ - lw_), kk, cum, lw))
    b_t = per_head(each(lambda b_, gi: b_ * gi, b, g_inv))
    k_t = per_head(each(lambda k_, gi: k_ * gi, k, g_inv))
    r_t = per_head(each(lambda r_, c_: r_ * jnp.exp(c_), r, cum))
    vh = per_head(v)
    tail = [jnp.exp(x[c - 1:c, :] - x) for x in cum]
    bh = per_head(each(lambda b_, t_: b_ * t_, b, tail))
    kh = per_head(each(lambda k_, t_: k_ * t_, k, tail))
    g_last = per_head([jnp.exp(x[c - 1:c, :]) for x in cum])

    a_ab = each(lambda x, y: jnp.where(strict, mm(x, y, nt=True), 0.0), a_t, b_t)
    a_ak = each(lambda x, y: jnp.where(strict, mm(x, y, nt=True), 0.0), a_t, k_t)
    a_rb = each(lambda x, y: jnp.where(incl, mm(x, y, nt=True), 0.0), r_t, b_t)
    a_rk = each(lambda x, y: jnp.where(incl, mm(x, y, nt=True), 0.0), r_t, k_t)
    x = [eye - n for n in a_ab]
    p = each(mm, a_ab, a_ab)
    steps = int(np.log2(c)) - 1
    for i in range(steps):
        x = each(lambda x_, p_: x_ + mm(x_, p_), x, p)
        if i + 1 < steps:
            p = each(mm, p, p)
    w1 = each(mm, x, a_t)
    akv = each(mm, a_ak, vh)
    w2 = each(mm, x, akv)
    y_in = each(lambda ark, v_, arb, w2_: mm(ark, v_) - mm(arb, w2_), a_rk, vh, a_rb, w2)
    q_h = each(lambda r_, arb, w1_: r_ - mm(arb, w1_), r_t, a_rb, w1)
    bh_t = [transpose(x_, pw) for x_ in bh]
    kh_t = [transpose(x_, pw) for x_ in kh]
    decay = [transpose(jnp.broadcast_to(x_, (HEAD, HEAD)), 3) for x_ in g_last]
    g_s = each(lambda kt_, v_, bt_, w2_: mm(kt_, v_) - mm(bt_, w2_), kh_t, vh, bh_t, w2)
    s = [s_ref[i] for i in range(len(heads))]
    ys = each(lambda yi, q_, s_: yi + mm(q_, s_), y_in, q_h, s)
    w1s = each(mm, w1, s)
    for i, (d_, s_, bt_, ws_, gs_) in enumerate(zip(decay, s, bh_t, w1s, g_s)):
        s_ref[i] = d_ * s_ - mm(bt_, ws_) + gs_

    for bi in range(nb):
        y = jnp.concatenate(ys[bi * RWKV_HEADS:(bi + 1) * RWKV_HEADS], axis=1)
        mean = _group_mean(y, HEAD)
        d = y - mean
        var = _group_mean(d * d, HEAD)
        yn = d * lax.rsqrt(var + RWKV_GN_EPS) * lnw_ref[...] + lnb_ref[...]
        bonus = _group_mean(r[bi] * k[bi] * rk_ref[...], HEAD) * HEAD * v[bi]
        o_ref[bi] = ((yn + bonus) * g_ref[bi]).astype(o_ref.dtype)


def _rwkv_recurrence(r, lw, k, v, kk, b, g, rk, lnw, lnb, bsz, seq, pw=1):
    c = min(RWKV_CHUNK, seq)
    blk = pl.BlockSpec((bsz, c, BRANCH_W), lambda t: (0, t, 0))
    cst = pl.BlockSpec((1, BRANCH_W), lambda t: (0, 0))
    as3 = lambda a: a.reshape(bsz, seq, BRANCH_W)
    out = pl.pallas_call(
        functools.partial(_rwkv_chunk_kernel, pw=pw),
        grid=(seq // c,),
        in_specs=[blk] * 7 + [cst] * 3,
        out_specs=blk,
        out_shape=jax.ShapeDtypeStruct((bsz, seq, BRANCH_W), BF16),
        scratch_shapes=[pltpu.VMEM((bsz * RWKV_HEADS, HEAD, HEAD), F32)],
        compiler_params=_params(1),
        name="rwkv_chunk",
    )(as3(r), as3(lw), as3(k), as3(v), as3(kk), as3(b), as3(g), rk.reshape(1, -1), lnw.reshape(1, -1),
      lnb.reshape(1, -1))
    return out.reshape(bsz * seq, BRANCH_W)


def _eye(n):
    r = lax.broadcasted_iota(jnp.int32, (n, n), 0)
    c = lax.broadcasted_iota(jnp.int32, (n, n), 1)
    return jnp.where(r == c, 1.0, 0.0).astype(BF16)


def _transpose_bf16(x):
    return _dg(_eye(x.shape[1]), x.astype(BF16), nt=True).astype(BF16)


def _nsa_norm_kernel(q_ref, ks_ref, kw_ref, vs_ref, vw_ref, gq_ref, gs_ref, gw_ref,
                     qo_ref, kso_ref, kwo_ref, vso_ref, vwo_ref):
    for x_ref, g_ref, o_ref in ((q_ref, gq_ref, qo_ref), (ks_ref, gs_ref, kso_ref), (kw_ref, gw_ref, kwo_ref)):
        x = x_ref[...]
        ms = _group_mean(x * x, HEAD)
        o_ref[...] = (x * lax.rsqrt(ms + NORM_EPS) * g_ref[...]).astype(o_ref.dtype)
    vso_ref[...] = _transpose_bf16(vs_ref[...])
    vwo_ref[...] = _transpose_bf16(vw_ref[...])


def _nsa_norm(p_small, qk_gain, bsz, seq, tt=512):
    n = p_small.shape[0]
    tt = min(tt, seq)
    nt = seq // tt
    col = lambda c, w: pl.BlockSpec((tt, w), lambda i: (i, c // w))
    cst = lambda w: pl.BlockSpec((1, w), lambda i: (0, 0))
    out = lambda w: pl.BlockSpec((tt, w), lambda i: (i, 0))
    out_t = pl.BlockSpec((128, tt), lambda i: (i // nt, i % nt))
    gq = (jnp.tile(qk_gain[0], 8) * HEAD ** -0.5).reshape(1, 512)
    gs = jnp.tile(qk_gain[2], 2).reshape(1, 128)
    gw = jnp.tile(qk_gain[3], 2).reshape(1, 128)
    return pl.pallas_call(
        _nsa_norm_kernel,
        grid=(n // tt,),
        in_specs=[col(C_NSAQ, 512), col(C_KS, 128), col(C_KW, 128), col(C_VS, 128), col(C_VW, 128),
                  cst(512), cst(128), cst(128)],
        out_specs=[out(512), out(128), out(128), out_t, out_t],
        out_shape=[jax.ShapeDtypeStruct((n, 512), BF16)] + [jax.ShapeDtypeStruct((n, 128), BF16)] * 2
        + [jax.ShapeDtypeStruct((bsz * 128, seq), BF16)] * 2,
        compiler_params=_params(1),
        name="nsa_norm",
    )(p_small, p_small, p_small, p_small, p_small, gq, gs, gw)


def _gelu_tanh(x):
    return 0.5 * x * (1.0 + jnp.tanh(0.7978845608028654 * (x + 0.044715 * x * x * x)))


def _nsa_compress_kernel(x_ref, pos_ref, w1_ref, b1_ref, w2_ref, gain_ref, o_ref):
    half = x_ref.shape[-1]
    x = x_ref[0, 0, 0].astype(BF16)
    w1 = w1_ref[0].astype(BF16)
    z_lo = jnp.dot(x, w1[:half], preferred_element_type=F32)
    z_hi = jnp.dot(x, w1[half:], preferred_element_type=F32)
    nrow = z_hi.shape[0]
    pos = jnp.broadcast_to(pos_ref[0], (8, 2 * half))
    const = _mm(pos, w1, pa=2)[0:1] + b1_ref[0]
    pre = z_lo + pltpu.roll(z_hi, nrow - 1, axis=0) + const
    out = jnp.dot(_gelu_tanh(pre).astype(BF16), w2_ref[0].astype(BF16), preferred_element_type=F32)
    is_key = pl.program_id(0) == 0
    normed = out * lax.rsqrt(jnp.mean(out * out, axis=-1, keepdims=True) + NORM_EPS) * gain_ref[...]
    o_ref[0, 0, 0] = jnp.where(is_key, normed, out)


def _nsa_compress(kc_vc, cmp_pos, cmp_w1, cmp_b1, cmp_w2, gain):
    _, bsz, hkv, nc, wid = kc_vc.shape
    return pl.pallas_call(
        _nsa_compress_kernel,
        grid=(2, bsz, hkv),
        in_specs=[pl.BlockSpec((1, 1, 1, nc, wid), lambda i, b, h: (i, b, h, 0, 0)),
                  pl.BlockSpec((1, 1, 2 * wid), lambda i, b, h: (i, 0, 0)),
                  pl.BlockSpec((1, 2 * wid, 2 * HEAD), lambda i, b, h: (i, 0, 0)),
                  pl.BlockSpec((1, 1, 2 * HEAD), lambda i, b, h: (i, 0, 0)),
                  pl.BlockSpec((1, 2 * HEAD, HEAD), lambda i, b, h: (i, 0, 0)),
                  pl.BlockSpec((1, HEAD), lambda i, b, h: (0, 0))],
        out_specs=pl.BlockSpec((1, 1, 1, nc, HEAD), lambda i, b, h: (i, b, h, 0, 0)),
        out_shape=jax.ShapeDtypeStruct((2, bsz, hkv, nc, HEAD), F32),
        compiler_params=_params(3),
        name="nsa_compress",
    )(kc_vc, cmp_pos.reshape(2, 1, 2 * wid), cmp_w1, cmp_b1.reshape(2, 1, 2 * HEAD), cmp_w2, gain.reshape(1, HEAD))


def _nsa_cmp_kernel(q_ref, kc_ref, vc_ref, ov_ref, o_ref, sel_ref, *, tq):
    hkv, ncmp = kc_ref.shape[1], kc_ref.shape[2]
    nsel = ov_ref.shape[0]
    g = NSA_GROUP
    rows = g * tq
    start = pl.program_id(1) * tq
    q_all = q_ref[...]
    t_pos = start + (lax.broadcasted_iota(jnp.int32, (ncmp, rows), 1) & (tq - 1))
    c_end = lax.broadcasted_iota(jnp.int32, (ncmp, rows), 0) * COMP_STRIDE + (COMP_L - 1)
    mask = c_end <= t_pos
    sid = lax.broadcasted_iota(jnp.int32, (nsel, tq), 0)
    cur = (start + lax.broadcasted_iota(jnp.int32, (nsel, tq), 1)) // SEL_L
    forced = (sid == 0) | (sid == cur) | (sid == cur - 1)
    k_top = min(SEL_N, nsel)
    outs, sels = [], []
    for h in range(hkv):
        q = _group_rows(q_all, h, g)
        s = jnp.where(mask, _dg(kc_ref[0, h].astype(BF16), q, nt=True), -1e30)
        e = jnp.exp(s - jnp.max(s, axis=0, keepdims=True))
        p = jnp.where(mask, e / jnp.sum(e, axis=0, keepdims=True), 0.0)
        outs.append(jnp.dot(_transpose_bf16(vc_ref[0, h]), p.astype(BF16), preferred_element_type=F32))
        p_sum = p[:, 0:tq]
        for i in range(1, g):
            p_sum = p_sum + p[:, i * tq:(i + 1) * tq]
        imp = _mm(ov_ref[...], p_sum, pb=3)
        val = jnp.where(forced, 1e9, jnp.where(sid <= cur, imp, -1e9))
        rank = jnp.zeros(imp.shape, jnp.int32)
        for j in range(nsel):
            cj = val[j:j + 1, :]
            ahead = (cj > val) | ((cj == val) & (sid > j))
            rank = rank + ahead.astype(jnp.int32)
        sels.append(jnp.where((rank < k_top) & (val > -1e8), 1.0, 0.0))
    o_ref[...] = jnp.concatenate(outs, axis=0)
    sel_ref[...] = jnp.concatenate(sels, axis=0).astype(sel_ref.dtype)


def _group_rows(q_all, h, g):
    return jnp.concatenate([q_all[:, (h * g + i) * HEAD:(h * g + i + 1) * HEAD] for i in range(g)], axis=0)


NSA_Q_TILE = 256


def _nsa_cmp(q_n, k_cmp, v_cmp, overlap_t, bsz, seq, tq=NSA_Q_TILE):
    hkv = k_cmp.shape[1]
    nsel, ncmp = overlap_t.shape
    nt = seq // tq
    rows = NSA_GROUP * tq
    return pl.pallas_call(
        functools.partial(_nsa_cmp_kernel, tq=tq),
        grid=(bsz, nt),
        in_specs=[pl.BlockSpec((tq, BRANCH_W), lambda b, t: (b * nt + t, 0)),
                  pl.BlockSpec((1, hkv, ncmp, HEAD), lambda b, t: (b, 0, 0, 0)),
                  pl.BlockSpec((1, hkv, ncmp, HEAD), lambda b, t: (b, 0, 0, 0)),
                  pl.BlockSpec((nsel, ncmp), lambda b, t: (0, 0))],
        out_specs=[pl.BlockSpec((hkv * HEAD, rows), lambda b, t: (b * nt + t, 0)),
                   pl.BlockSpec((hkv * nsel, tq), lambda b, t: (b * nt + t, 0))],
        out_shape=[jax.ShapeDtypeStruct((bsz * nt * hkv * HEAD, rows), F32),
                   jax.ShapeDtypeStruct((bsz * nt * hkv * nsel, tq), BF16)],
        compiler_params=_params(2),
        name="nsa_cmp",
    )(q_n, k_cmp, v_cmp, overlap_t)


SCORE_MASKED = -1e30
SCORE_FLOOR = -1e20


def _nsa_attn_kernel(q_ref, sel_ref, ks_ref, vs_ref, kw_ref, vw_ref, oc_ref, gl_ref, o_ref, *, tq, kt):
    g = NSA_GROUP
    hkv = ks_ref.shape[1] // HEAD
    seq = ks_ref.shape[0]
    nsel = sel_ref.shape[0] // hkv
    rows = g * tq
    start = pl.program_id(1) * tq
    q_all = q_ref[...]
    sel_all = sel_ref[...]
    qs = [_group_rows(q_all, h, g) for h in range(hkv)]
    sels = [sel_all[h * nsel:(h + 1) * nsel, :] for h in range(hkv)]
    head_cols = lambda x, h: x[:, h * HEAD:(h + 1) * HEAD]
    per_query = lambda x: jnp.concatenate([x] * g, axis=1)
    with_ones = lambda vt: jnp.concatenate([vt, jnp.ones((16, vt.shape[1]), BF16)], axis=0)

    key_pos = lax.broadcasted_iota(jnp.int32, (kt, tq), 0)
    t_pos = start + lax.broadcasted_iota(jnp.int32, (kt, tq), 1)
    blk_of_key = lax.broadcasted_iota(jnp.int32, (kt, nsel), 0) // SEL_L
    blk_id = lax.broadcasted_iota(jnp.int32, (kt, nsel), 1)

    def key_tile(j, carry, causal):
        base = pl.multiple_of(j * kt, kt)
        kb2 = ks_ref[pl.ds(base, kt), :]
        vt2 = vs_ref[:, pl.ds(base, kt)]
        expand = jnp.where(blk_of_key + j * (kt // SEL_L) == blk_id, 1.0, 0.0).astype(BF16)
        out = []
        for h in range(hkv):
            m, acc = carry[h]
            keep = jnp.dot(expand, sels[h], preferred_element_type=F32) > 0.5
            if causal:
                keep = keep & (key_pos + base <= t_pos)
            bias = per_query(jnp.where(keep, 0.0, SCORE_MASKED))
            s = _dg(head_cols(kb2, h), qs[h], nt=True) + bias
            m_new = jnp.maximum(m, jnp.max(s, axis=0, keepdims=True))
            p = jnp.exp(s - m_new).astype(BF16)
            vt = with_ones(vt2[h * HEAD:(h + 1) * HEAD, :])
            acc = jnp.exp(m - m_new) * acc + jnp.dot(vt, p, preferred_element_type=F32)
            out.append((m_new, acc))
        return tuple(out)

    init = tuple((jnp.full((1, rows), SCORE_FLOOR, F32), jnp.zeros((HEAD + 16, rows), F32)) for _ in range(hkv))
    n_full = start // kt
    carry = lax.fori_loop(0, n_full, lambda j, c: key_tile(j, c, False), init)
    carry = key_tile(n_full, carry, True)
    o_sel = [acc[0:HEAD] / acc[HEAD:HEAD + 1] for _, acc in carry]

    span = min(WINDOW + tq, seq)
    wbase = pl.multiple_of(jnp.maximum(start - WINDOW, 0), tq) if seq > span else 0
    kb2 = kw_ref[pl.ds(wbase, span), :]
    vt2 = vw_ref[:, pl.ds(wbase, span)]
    tw = start + lax.broadcasted_iota(jnp.int32, (span, tq), 1)
    wpos = wbase + lax.broadcasted_iota(jnp.int32, (span, tq), 0)
    wbias = per_query(jnp.where((wpos <= tw) & (wpos > tw - WINDOW), 0.0, SCORE_MASKED))
    o_win = []
    for h in range(hkv):
        s = _dg(head_cols(kb2, h), qs[h], nt=True) + wbias
        p = jnp.exp(s - jnp.max(s, axis=0, keepdims=True)).astype(BF16)
        acc = jnp.dot(with_ones(vt2[h * HEAD:(h + 1) * HEAD, :]), p, preferred_element_type=F32)
        o_win.append(acc[0:HEAD] / acc[HEAD:HEAD + 1])

    gate_t = _mm(_eye(128), jax.nn.sigmoid(gl_ref[...]), pb=3, nt=True)
    o_cmp = oc_ref[...]
    blocks = []
    for h in range(hkv):
        for i in range(g):
            cols = slice(i * tq, (i + 1) * tq)
            r = (h * g + i) * 3
            blocks.append(gate_t[r:r + 1] * o_cmp[h * HEAD:(h + 1) * HEAD, cols]
                          + gate_t[r + 1:r + 2] * o_sel[h][:, cols] + gate_t[r + 2:r + 3] * o_win[h][:, cols])
    y_t = jnp.concatenate(blocks, axis=0)
    o_ref[...] = _dg(_eye(tq), y_t.astype(BF16), nt=True).astype(o_ref.dtype)


def _nsa_attn(p_small, q_n, sel_t, ks, vs_t, kw, vw_t, o_cmp_t, bsz, seq, tq=NSA_Q_TILE, kt=512):
    kt = min(kt, seq)
    nt = seq // tq
    hkv = NSA_KV_HEADS
    nsel = seq // SEL_L
    row = lambda w: pl.BlockSpec((tq, w), lambda b, t: (b * nt + t, 0))
    tile = lambda r, w: pl.BlockSpec((r, w), lambda b, t: (b * nt + t, 0))
    k_spec = pl.BlockSpec((seq, hkv * HEAD), lambda b, t: (b, 0))
    vt_spec = pl.BlockSpec((hkv * HEAD, seq), lambda b, t: (b, 0))
    return pl.pallas_call(
        functools.partial(_nsa_attn_kernel, tq=tq, kt=kt),
        grid=(bsz, nt),
        in_specs=[row(BRANCH_W), tile(hkv * nsel, tq), k_spec, vt_spec, k_spec, vt_spec,
                  tile(hkv * HEAD, NSA_GROUP * tq),
                  pl.BlockSpec((tq, 128), lambda b, t: (b * nt + t, C_GL // 128))],
        out_specs=row(BRANCH_W),
        out_shape=jax.ShapeDtypeStruct((bsz * seq, BRANCH_W), BF16),
        compiler_params=_params(2),
        name="nsa_attn",
    )(q_n, sel_t, ks, vs_t, kw, vw_t, o_cmp_t, p_small)


def _nsa(p_small, bsz, seq, qk_gain, cmp_pos, cmp_w1, cmp_b1, cmp_w2):
    hkv = NSA_KV_HEADS
    q_n, ks_n, kw_n, vs_t, vw_t = _nsa_norm(p_small, qk_gain, bsz, seq)
    ncmp = seq // COMP_STRIDE
    grouped = lambda c: p_small[:, c:c + 128].reshape(bsz, ncmp, COMP_STRIDE, hkv, HEAD).transpose(0, 3, 1, 2, 4)
    kc_vc = jnp.stack([grouped(C_KC), grouped(C_VC)]).reshape(2, bsz, hkv, ncmp, COMP_STRIDE * HEAD)
    cmp = _nsa_compress(kc_vc, cmp_pos, cmp_w1, cmp_b1, cmp_w2, qk_gain[1])
    nsel = seq // SEL_L
    c0 = np.arange(ncmp)[None, :] * COMP_STRIDE
    s0 = np.arange(nsel)[:, None] * SEL_L
    overlap_t = np.clip(np.minimum(c0 + COMP_L, s0 + SEL_L) - np.maximum(c0, s0), 0, None) / COMP_L
    o_cmp_t, sel_t = _nsa_cmp(q_n, cmp[0], cmp[1], jnp.asarray(overlap_t, BF16), bsz, seq)
    return _nsa_attn(p_small, q_n, sel_t, ks_n, vs_t, kw_n, vw_t, o_cmp_t, bsz, seq)


HALO = 16


def _conv_pool_kernel(bg_ref, cg_ref, xi_ref, pu_ref, cw_ref, conv_ref, pool_ref, cz_ref, cp_ref):
    tt = xi_ref.shape[0]
    ti = pl.program_id(1)

    @pl.when(ti == 0)
    def _():
        cz_ref[...] = jnp.zeros_like(cz_ref)
        cp_ref[...] = jnp.zeros_like(cp_ref)

    def history(x, carry_ref):
        ext = jnp.concatenate([carry_ref[...], x], axis=0)
        carry_ref[...] = x[tt - HALO:tt, :]
        return ext

    lag = lambda ext, s: pltpu.roll(ext, s, axis=0)
    body = lambda ext: ext[HALO:HALO + tt, :]

    z = cg_ref[...] * xi_ref[...]
    ze = history(z, cz_ref)
    cw = cw_ref[...]
    y = cw[2:3, :] * z + cw[1:2, :] * body(lag(ze, 1)) + cw[0:1, :] * body(lag(ze, 2))
    conv_ref[...] = (bg_ref[...] * y).astype(conv_ref.dtype)

    u = pu_ref[...]
    sums = [history(u, cp_ref)]
    for w in (1, 2, 4, 8):
        sums.append(sums[-1] + lag(sums[-1], w))
    count = (ti * tt + 1 + lax.broadcasted_iota(jnp.int32, (tt, 128), 0)).astype(F32)
    outs = []
    for gi, w in enumerate(POOL_WINDOWS):
        sl = slice(gi * 128, (gi + 1) * 128)
        outs.append(body(sums[gi + 1])[:, sl] / jnp.minimum(count, float(w)) - u[:, sl])
    pool_ref[...] = jnp.concatenate(outs, axis=1).astype(pool_ref.dtype)


def _conv_pool(p_small, bsz, seq, conv_w, tt=512):
    tt = min(tt, seq)
    nt = seq // tt
    n = bsz * seq
    col = lambda c: pl.BlockSpec((tt, 512), lambda b, t: (b * nt + t, c // 512))
    row = pl.BlockSpec((tt, 512), lambda b, t: (b * nt + t, 0))
    return pl.pallas_call(
        _conv_pool_kernel,
        grid=(bsz, nt),
        in_specs=[col(C_CONV), col(C_CONV + 512), col(C_CONV + 1024), col(C_POOL),
                  pl.BlockSpec((8, 512), lambda b, t: (0, 0))],
        out_specs=[row, row],
        out_shape=[jax.ShapeDtypeStruct((n, 512), BF16)] * 2,
        scratch_shapes=[pltpu.VMEM((HALO, 512), F32)] * 2,
        compiler_params=_params(2),
        name="conv_pool",
    )(p_small, p_small, p_small, p_small, jnp.pad(conv_w, ((0, 8 - conv_w.shape[0]), (0, 0))))


def _mem_attn_kernel(q_ref, kv_ref, gq_ref, gk_ref, o_ref, *, scale):
    outs = []
    for h in range(MEM_HEADS):
        sl = slice(h * MEM_HEAD, (h + 1) * MEM_HEAD)
        q = q_ref[:, sl]
        q = q * lax.rsqrt(jnp.mean(q * q, axis=-1, keepdims=True) + NORM_EPS) * gq_ref[...]
        k = kv_ref[0, :, sl]
        k = k * lax.rsqrt(jnp.mean(k * k, axis=-1, keepdims=True) + NORM_EPS) * gk_ref[...]
        v = kv_ref[0, :, BRANCH_W + h * MEM_HEAD:BRANCH_W + (h + 1) * MEM_HEAD]
        s = _dg(q.astype(BF16), k.astype(BF16), nt=True) * scale
        e = jnp.exp(s - jnp.max(s, axis=-1, keepdims=True))
        p = e / jnp.sum(e, axis=-1, keepdims=True)
        outs.append(jnp.dot(p.astype(BF16), v.astype(BF16), preferred_element_type=F32))
    o_ref[...] = jnp.concatenate(outs, axis=1).astype(o_ref.dtype)


def _mem_attn(p_small, kv, bsz, seq, qk_gain, tq=512):
    tq = min(tq, seq)
    nt = seq // tq
    mlen = kv.shape[1]
    return pl.pallas_call(
        functools.partial(_mem_attn_kernel, scale=MEM_HEAD ** -0.5),
        grid=(bsz, nt),
        in_specs=[pl.BlockSpec((tq, 512), lambda b, t: (b * nt + t, C_MEM // 512)),
                  pl.BlockSpec((1, mlen, 2 * BRANCH_W), lambda b, t: (b, 0, 0)),
                  pl.BlockSpec((1, MEM_HEAD), lambda b, t: (0, 0)),
                  pl.BlockSpec((1, MEM_HEAD), lambda b, t: (0, 0))],
        out_specs=pl.BlockSpec((tq, 512), lambda b, t: (b * nt + t, 0)),
        out_shape=jax.ShapeDtypeStruct((bsz * seq, 512), BF16),
        compiler_params=_params(2),
        name="mem_attn",
    )(p_small, kv, qk_gain[0].reshape(1, -1), qk_gain[1].reshape(1, -1))


def _merge_kernel(h_ref, *refs):
    wg_refs, y_refs = refs[0:5], refs[5:9]
    wb_ref, pooled_ref, wp_ref, ps_ref, o_ref, wgc_ref, wbc_ref = refs[9:]

    @pl.when(pl.program_id(1) == 0)
    def _():
        for i in range(5):
            wgc_ref[i] = wg_refs[i][...].astype(BF16)
        wbc_ref[...] = wb_ref[...].astype(BF16)

    h = h_ref[...]
    gate = lambda i: jax.nn.sigmoid(jnp.dot(h, wgc_ref[i], preferred_element_type=F32))
    z_pool = jnp.dot(pooled_ref[...], wp_ref[0].astype(BF16), preferred_element_type=F32) * ps_ref[...]
    acc = gate(4) * z_pool
    for i in range(4):
        acc = acc + gate(i) * jnp.dot(y_refs[i][...], wbc_ref[i], preferred_element_type=F32)
    o_ref[...] = acc.astype(o_ref.dtype)


def _merge(h, w_gate, ys, w_branch, pooled, pool_w, pool_scale, tm=512):
    n = h.shape[0]
    tm = min(tm, n)
    tn = 512
    nj = D_MODEL // tn
    once = pl.Buffered(1)
    gate_spec = lambda g: pl.BlockSpec((D_MODEL, tn), lambda j, i: (0, g * nj + j), pipeline_mode=once)
    y_spec = pl.BlockSpec((tm, BRANCH_W), lambda j, i: (i, 0))
    return pl.pallas_call(
        _merge_kernel,
        grid=(nj, n // tm),
        in_specs=[pl.BlockSpec((tm, D_MODEL), lambda j, i: (i, 0))] + [gate_spec(g) for g in range(5)]
        + [y_spec] * 4
        + [pl.BlockSpec((4, BRANCH_W, tn), lambda j, i: (0, 0, j), pipeline_mode=once),
           pl.BlockSpec((tm, 128), lambda j, i: (i, j)),
           pl.BlockSpec((1, 128, tn), lambda j, i: (j, 0, 0)),
           pl.BlockSpec((1, tn), lambda j, i: (0, j))],
        out_specs=pl.BlockSpec((tm, tn), lambda j, i: (i, j)),
        out_shape=jax.ShapeDtypeStruct((n, D_MODEL), BF16),
        scratch_shapes=[pltpu.VMEM((5, D_MODEL, tn), BF16), pltpu.VMEM((4, BRANCH_W, tn), BF16)],
        compiler_params=_params(2),
        name="merge",
    )(h, *([w_gate] * 5), *ys, w_branch, pooled, pool_w, pool_scale.reshape(1, -1))


def _pack_w_in(w, vres_w):
    pad = lambda a, width: jnp.pad(a, ((0, 0), (0, width - a.shape[1])))
    nsa = w[:, 1984:3288]
    vres_cols = pad(vres_w, 128) if vres_w is not None else jnp.zeros((D_MODEL, 128), w.dtype)
    small = jnp.concatenate([
        w[:, 0:1536], pad(w[:, 1536:1632], 128), pad(w[:, 1632:1728], 128), w[:, 1728:1984],
        nsa[:, 0:512], w[:, 3288:4824], w[:, 4824:5336], w[:, 5336:5848],
        nsa[:, 512:1280], pad(nsa[:, 1280:1304], 128), vres_cols], axis=1)
    return small, w[:, 5848:]


def kernel(x, mem, norm_mix, norm_ffn, norm_mem, w_in, rwkv_mu, rwkv_w0, rwkv_w2, rwkv_a0, rwkv_a2, rwkv_g2, rwkv_kk, rwkv_ka, rwkv_rk, rwkv_ln_w, rwkv_ln_b, vres_in, vres_mu, vres_v0, vres_up, nsa_qk_gain, nsa_cmp_pos, nsa_cmp_w1, nsa_cmp_b1, nsa_cmp_w2, conv_w, pool_w, pool_scale, mem_wkv, mem_qk_gain, w_branch, w_out, ffn_w1, ffn_w3, ffn_w2, moe_router, moe_w1, moe_w3, moe_w2):
    bsz, seq, d = x.shape
    n = bsz * seq
    depth = w_in.shape[0]
    mlen = mem.shape[1]
    xf = x.reshape(n, d)
    memf = mem.reshape(bsz * mlen, d)
    v_first = None
    for l in range(depth):
        h = _rmsnorm(xf, norm_mix[l])
        w_small, w_gate = _pack_w_in(w_in[l], vres_in[l - 1] if l > 0 else None)
        p_small = _matmul(h, w_small, tm=1024, tn=1024, name="in_proj")
        vres = (vres_mu[l - 1], vres_v0[l - 1], vres_up[l - 1], v_first) if l > 0 else None
        r, lw, k, v, kk, b, g = _rwkv_prep(p_small, bsz, seq, rwkv_mu[l], rwkv_w0[l], rwkv_w2[l], rwkv_a0[l],
                                           rwkv_a2[l], rwkv_g2[l], rwkv_kk[l], rwkv_ka[l], vres)
        if l == 0:
            v_first = v
        y_rwkv = _rwkv_recurrence(r, lw, k, v, kk, b, g, rwkv_rk[l], rwkv_ln_w[l], rwkv_ln_b[l], bsz, seq)
        y_nsa = _nsa(p_small, bsz, seq, nsa_qk_gain[l], nsa_cmp_pos[l], nsa_cmp_w1[l], nsa_cmp_b1[l], nsa_cmp_w2[l])
        y_conv, pooled = _conv_pool(p_small, bsz, seq, conv_w[l])
        mem_n = _rmsnorm(memf, norm_mem[l])
        kv = _matmul(mem_n, mem_wkv[l], tm=512, tn=512, name="mem_kv").reshape(bsz, mlen, 2 * BRANCH_W)
        y_mem = _mem_attn(p_small, kv, bsz, seq, mem_qk_gain[l])
        merged = _merge(h, w_gate, (y_rwkv, y_nsa, y_conv, y_mem), w_branch[l], pooled, pool_w[l], pool_scale[l])
        xf = _matmul(merged, w_out[l], tm=1024, tn=1024, res=xf, name="out_proj")

        if l % 2 == 0:
            h2 = _rmsnorm(xf, norm_ffn[l])
            e = l // 2
            act = _swiglu_up(h2, ffn_w1[e], ffn_w3[e])
            xf = _matmul(act, ffn_w2[e], tm=512, tn=512, res=xf, name="ffn_down")
        else:
            e = l // 2
            xf = _moe(xf, norm_ffn[l], moe_router[e], moe_w1[e], moe_w3[e], moe_w2[e])
    return xf.reshape(bsz, seq, d)
```

```python
import functools

import jax
import jax.numpy as jnp
import numpy as np
from jax import lax
from jax.experimental import pallas as pl
from jax.experimental.pallas import tpu as pltpu

F32 = jnp.float32
BF16 = jnp.bfloat16

D_MODEL = 2048
BRANCH_W = 512
HEAD = 64
RWKV_HEADS = 8
NSA_KV_HEADS = 2
NSA_GROUP = 4
COMP_L = 32
COMP_STRIDE = 16
SEL_L = 64
SEL_N = 16
WINDOW = 512
MEM_HEADS = 4
MEM_HEAD = 128
POOL_WINDOWS = (2, 4, 8, 16)
N_EXPERTS = 8
NORM_EPS = 1e-6
RWKV_GN_EPS = 64e-5
RWKV_CHUNK = 64

VMEM_LIMIT_BYTES = 56 * 1024 * 1024

C_RWKV, C_NSAQ, C_CONV, C_POOL, C_MEM = 0, 2048, 2560, 4096, 4608
C_KC, C_VC, C_KS, C_VS, C_KW, C_VW, C_GL, C_VRES = 5120, 5248, 5376, 5504, 5632, 5760, 5888, 6016
N_SMALL = 6144


def _params(n_axes):
    return pltpu.CompilerParams(dimension_semantics=("arbitrary",) * n_axes,
                                vmem_limit_bytes=VMEM_LIMIT_BYTES)


def _split(a, n):
    pieces, r = [], a
    for i in range(n):
        p = r.astype(BF16)
        pieces.append(p)
        if i + 1 < n:
            r = r - p.astype(F32)
    return pieces


def _dg(a, b, nt):
    dims = (((1,), (1,)), ((), ())) if nt else (((1,), (0,)), ((), ()))
    return lax.dot_general(a, b, dims, preferred_element_type=F32)


def _mm(a, b, pa=1, pb=1, nt=False):
    sa = _split(a, pa) if a.dtype != BF16 else [a]
    sb = _split(b, pb) if b.dtype != BF16 else [b]
    order = max(len(sa), len(sb))
    acc = None
    for i, x in enumerate(sa):
        for j, y in enumerate(sb):
            if i + j < order:
                t = _dg(x, y, nt)
                acc = t if acc is None else acc + t
    return acc


def _head_sum_matrix(width, head):
    r = lax.broadcasted_iota(jnp.int32, (width, width), 0) // head
    c = lax.broadcasted_iota(jnp.int32, (width, width), 1) // head
    return jnp.where(r == c, 1.0, 0.0).astype(BF16)


def _group_mean(x, head):
    hs = _head_sum_matrix(x.shape[-1], head)
    return _mm(x, hs, pa=3) * (1.0 / head)


def _rmsnorm_kernel(x_ref, g_ref, o_ref):
    x = x_ref[...]
    y = x * lax.rsqrt(jnp.mean(x * x, axis=-1, keepdims=True) + NORM_EPS)
    o_ref[...] = (y * g_ref[...]).astype(o_ref.dtype)


def _rmsnorm(x, g, tm=512):
    m, d = x.shape
    tm = min(tm, m)
    return pl.pallas_call(
        _rmsnorm_kernel,
        grid=(m // tm,),
        in_specs=[pl.BlockSpec((tm, d), lambda i: (i, 0)), pl.BlockSpec((1, d), lambda i: (0, 0))],
        out_specs=pl.BlockSpec((tm, d), lambda i: (i, 0)),
        out_shape=jax.ShapeDtypeStruct((m, d), BF16),
        compiler_params=_params(1),
        name="rmsnorm",
    )(x, g.reshape(1, d))


def _rmsnorm_router_kernel(x_ref, g_ref, wr_ref, o_ref, comb_ref, assign_ref, count_ref, carry_ref):
    tm = x_ref.shape[0]

    @pl.when(pl.program_id(0) == 0)
    def _():
        carry_ref[...] = jnp.zeros_like(carry_ref)

    x = x_ref[...]
    y = x * lax.rsqrt(jnp.mean(x * x, axis=-1, keepdims=True) + NORM_EPS) * g_ref[...]
    o_ref[...] = y
    logits = _mm(y, wr_ref[...], pa=3, pb=3)
    lane = lax.broadcasted_iota(jnp.int32, logits.shape, 1)
    neg = jnp.float32(-3e38)
    lg = jnp.where(lane < N_EXPERTS, logits, neg)
    m1 = jnp.max(lg, axis=-1, keepdims=True)
    i1 = jnp.min(jnp.where(lg == m1, lane, 1 << 20), axis=-1, keepdims=True)
    lg2 = jnp.where(lane == i1, neg, lg)
    m2 = jnp.max(lg2, axis=-1, keepdims=True)
    i2 = jnp.min(jnp.where(lg2 == m2, lane, 1 << 20), axis=-1, keepdims=True)
    e2 = jnp.exp(m2 - m1)
    w1 = 1.0 / (1.0 + e2)
    w2 = e2 / (1.0 + e2)
    comb_ref[...] = jnp.where(lane == i1, w1, 0.0) + jnp.where(lane == i2, w2, 0.0)
    assign = jnp.where((lane == i1) | (lane == i2), 1.0, 0.0)
    assign_ref[...] = assign
    row = lax.broadcasted_iota(jnp.int32, (tm, tm), 0)
    col = lax.broadcasted_iota(jnp.int32, (tm, tm), 1)
    tril = jnp.where(row >= col, 1.0, 0.0).astype(BF16)
    count = jnp.dot(tril, assign.astype(BF16), preferred_element_type=F32) + carry_ref[0:1, :]
    count_ref[...] = count
    carry_ref[0:1, :] = count[tm - 1:tm, :]


def _rmsnorm_router(x, g, router, tm=512):
    m, d = x.shape
    tm = min(tm, m)
    wr = jnp.pad(router, ((0, 0), (0, 128 - router.shape[1])))
    lanes = pl.BlockSpec((tm, 128), lambda i: (i, 0))
    return pl.pallas_call(
        _rmsnorm_router_kernel,
        grid=(m // tm,),
        in_specs=[pl.BlockSpec((tm, d), lambda i: (i, 0)), pl.BlockSpec((1, d), lambda i: (0, 0)),
                  pl.BlockSpec((d, 128), lambda i: (0, 0))],
        out_specs=[pl.BlockSpec((tm, d), lambda i: (i, 0)), lanes, lanes, lanes],
        out_shape=[jax.ShapeDtypeStruct((m, d), F32)] + [jax.ShapeDtypeStruct((m, 128), F32)] * 3,
        scratch_shapes=[pltpu.VMEM((8, 128), F32)],
        compiler_params=_params(1),
        name="rmsnorm_router",
    )(x, g.reshape(1, d), wr)


MOE_TILE = 256


def _row_copy(src_hbm, row, dst, slot, sem):
    return pltpu.make_async_copy(src_hbm.at[pl.ds(row, 1), :], dst.at[pl.ds(slot, 1), :], sem)


ROW_DMA_THREADS = 2


def _gather_rows_kernel(idx_ref, nxt_ref, src_hbm, o_ref, buf_ref, sem):
    tm = o_ref.shape[0]

    i = pl.program_id(0)
    slot = i % 2

    def issue(ids_ref, to_slot):
        def body(q, carry):
            for t in range(ROW_DMA_THREADS):
                r = q * ROW_DMA_THREADS + t
                _row_copy(src_hbm, ids_ref[0, 0, r], buf_ref.at[to_slot], r, sem.at[to_slot]).start(priority=t)
            return carry
        lax.fori_loop(0, tm // ROW_DMA_THREADS, body, 0)

    @pl.when(i == 0)
    def _():
        issue(idx_ref, 0)

    @pl.when(i + 1 < pl.num_programs(0))
    def _():
        issue(nxt_ref, 1 - slot)

    def wait(r, carry):
        _row_copy(src_hbm, 0, buf_ref.at[slot], r, sem.at[slot]).wait()
        return carry

    lax.fori_loop(0, tm, wait, 0)
    o_ref[...] = buf_ref[slot].astype(o_ref.dtype)


def _gather_rows(src, idx, tm=MOE_TILE):
    r = idx.shape[0]
    d = src.shape[1]
    n_tiles = r // tm
    ids = idx.reshape(n_tiles, 1, tm)
    return pl.pallas_call(
        _gather_rows_kernel,
        grid=(n_tiles,),
        in_specs=[pl.BlockSpec((1, 1, tm), lambda i: (i, 0, 0), memory_space=pltpu.SMEM),
                  pl.BlockSpec((1, 1, tm), lambda i: (jnp.minimum(i + 1, n_tiles - 1), 0, 0),
                               memory_space=pltpu.SMEM),
                  pl.BlockSpec(memory_space=pl.ANY)],
        out_specs=pl.BlockSpec((tm, d), lambda i: (i, 0)),
        out_shape=jax.ShapeDtypeStruct((r, d), BF16),
        scratch_shapes=[pltpu.VMEM((2, tm, d), F32), pltpu.SemaphoreType.DMA((2,))],
        compiler_params=_params(1),
        name="moe_gather",
    )(ids, ids, src)


def _expert_changed(te_ref, i):
    return (i == 0) | (te_ref[i] != te_ref[jnp.maximum(i - 1, 0)])


def _moe_up_kernel(te_ref, na_ref, x_ref, w1_ref, w3_ref, o_ref, w1c_ref, w3c_ref):
    i = pl.program_id(1)
    active = i < na_ref[0]

    @pl.when(active & _expert_changed(te_ref, i))
    def _():
        w1c_ref[...] = w1_ref[0].astype(BF16)
        w3c_ref[...] = w3_ref[0].astype(BF16)

    @pl.when(active)
    def _():
        x = x_ref[...]
        a = jnp.dot(x, w1c_ref[...], preferred_element_type=F32)
        b = jnp.dot(x, w3c_ref[...], preferred_element_type=F32)
        o_ref[...] = (a * jax.nn.sigmoid(a) * b).astype(o_ref.dtype)

    @pl.when(jnp.logical_not(active))
    def _():
        o_ref[...] = jnp.zeros_like(o_ref)


def _moe_down_kernel(te_ref, na_ref, a_ref, w_ref, o_ref, wc_ref):
    i = pl.program_id(1)
    active = i < na_ref[0]

    @pl.when(active & _expert_changed(te_ref, i))
    def _():
        wc_ref[...] = w_ref[0].astype(BF16)

    @pl.when(active)
    def _():
        o_ref[...] = jnp.dot(a_ref[...], wc_ref[...], preferred_element_type=F32)

    @pl.when(jnp.logical_not(active))
    def _():
        o_ref[...] = jnp.zeros_like(o_ref)


def _moe_experts(x_sorted, tile_expert, n_active, w1, w3, w2, tm=MOE_TILE):
    rows, d = x_sorted.shape
    ff = w1.shape[2]
    n_tiles = rows // tm
    tn_up, tn_down = ff // 2, d // 2
    once = pl.Buffered(1)
    act = pl.pallas_call(
        _moe_up_kernel,
        grid_spec=pltpu.PrefetchScalarGridSpec(
            num_scalar_prefetch=2,
            grid=(ff // tn_up, n_tiles),
            in_specs=[pl.BlockSpec((tm, d), lambda j, i, te, na: (i, 0)),
                      pl.BlockSpec((1, d, tn_up), lambda j, i, te, na: (te[i], 0, j)),
                      pl.BlockSpec((1, d, tn_up), lambda j, i, te, na: (te[i], 0, j), pipeline_mode=once)],
            out_specs=pl.BlockSpec((tm, tn_up), lambda j, i, te, na: (i, j)),
            scratch_shapes=[pltpu.VMEM((d, tn_up), BF16)] * 2),
        out_shape=jax.ShapeDtypeStruct((rows, ff), BF16),
        compiler_params=_params(2),
        name="moe_up",
    )(tile_expert, n_active, x_sorted, w1, w3)
    return pl.pallas_call(
        _moe_down_kernel,
        grid_spec=pltpu.PrefetchScalarGridSpec(
            num_scalar_prefetch=2,
            grid=(d // tn_down, n_tiles),
            in_specs=[pl.BlockSpec((tm, ff), lambda j, i, te, na: (i, 0)),
                      pl.BlockSpec((1, ff, tn_down), lambda j, i, te, na: (te[i], 0, j))],
            out_specs=pl.BlockSpec((tm, tn_down), lambda j, i, te, na: (i, j)),
            scratch_shapes=[pltpu.VMEM((ff, tn_down), BF16)]),
        out_shape=jax.ShapeDtypeStruct((rows, d), F32),
        compiler_params=_params(2),
        name="moe_down",
    )(tile_expert, n_active, act, w2)


def _moe_combine_kernel(pa_ref, pb_ref, na_ref, nb_ref, y_hbm, x_ref, w_ref, o_ref, buf_ref, sem):
    tm = x_ref.shape[0]
    i = pl.program_id(0)
    slot = i % 2

    def issue(a_ref, b_ref, to_slot):
        def body(r, carry):
            _row_copy(y_hbm, a_ref[0, 0, r], buf_ref.at[to_slot, 0], r, sem.at[to_slot]).start(priority=0)
            _row_copy(y_hbm, b_ref[0, 0, r], buf_ref.at[to_slot, 1], r, sem.at[to_slot]).start(priority=1)
            return carry
        lax.fori_loop(0, tm, body, 0)

    @pl.when(i == 0)
    def _():
        issue(pa_ref, pb_ref, 0)

    @pl.when(i + 1 < pl.num_programs(0))
    def _():
        issue(na_ref, nb_ref, 1 - slot)

    def wait(r, carry):
        _row_copy(y_hbm, 0, buf_ref.at[slot, 0], r, sem.at[slot]).wait()
        _row_copy(y_hbm, 0, buf_ref.at[slot, 1], r, sem.at[slot]).wait()
        return carry

    lax.fori_loop(0, tm, wait, 0)
    w = w_ref[...]
    o_ref[...] = x_ref[...] + w[:, 0:1] * buf_ref[slot, 0] + w[:, 1:2] * buf_ref[slot, 1]


def _moe_combine(x, y_sorted, pos_a, pos_b, weights, tm=MOE_TILE):
    n, d = x.shape
    n_tiles = n // tm
    cur = pl.BlockSpec((1, 1, tm), lambda i: (i, 0, 0), memory_space=pltpu.SMEM)
    nxt = pl.BlockSpec((1, 1, tm), lambda i: (jnp.minimum(i + 1, n_tiles - 1), 0, 0), memory_space=pltpu.SMEM)
    pa, pb = pos_a.reshape(n_tiles, 1, tm), pos_b.reshape(n_tiles, 1, tm)
    return pl.pallas_call(
        _moe_combine_kernel,
        grid=(n_tiles,),
        in_specs=[cur, cur, nxt, nxt, pl.BlockSpec(memory_space=pl.ANY),
                  pl.BlockSpec((tm, d), lambda i: (i, 0)), pl.BlockSpec((tm, 128), lambda i: (i, 0))],
        out_specs=pl.BlockSpec((tm, d), lambda i: (i, 0)),
        out_shape=jax.ShapeDtypeStruct((n, d), F32),
        scratch_shapes=[pltpu.VMEM((2, 2, tm, d), F32), pltpu.SemaphoreType.DMA((2,))],
        compiler_params=_params(1),
        name="moe_combine",
    )(pa, pb, pa, pb, y_sorted, x, weights)


def _moe(x, norm_g, router, w1, w3, w2):
    n, d = x.shape
    tm = min(MOE_TILE, n)
    h2, comb, assign, count = _rmsnorm_router(x, norm_g, router)
    assigned = assign[:, :N_EXPERTS] > 0.5
    count = count[:, :N_EXPERTS].astype(jnp.int32)
    total = count[-1]
    padded = (total + tm - 1) // tm * tm
    ends = jnp.cumsum(padded)
    starts = ends - padded
    dest = starts[None, :] + count - 1
    rows = (2 * n // tm + N_EXPERTS) * tm
    tile_start = jnp.arange(rows // tm, dtype=jnp.int32) * tm
    tile_expert = jnp.minimum(jnp.sum(tile_start[:, None] >= ends[None, :], axis=1), N_EXPERTS - 1).astype(jnp.int32)
    n_active = (ends[-1:] // tm).astype(jnp.int32)
    offset = tile_start[:, None] + jnp.arange(tm, dtype=jnp.int32)[None, :] - starts[tile_expert][:, None]
    src = jnp.sum(count.T[tile_expert][:, None, :] <= offset[:, :, None], axis=-1, dtype=jnp.int32)
    src = jnp.minimum(src, n - 1).reshape(rows)
    first = jnp.argmax(assigned, axis=1)
    second = N_EXPERTS - 1 - jnp.argmax(assigned[:, ::-1], axis=1)
    take = lambda a, i: jnp.take_along_axis(a, i[:, None], axis=1)[:, 0]
    pos_a, pos_b = take(dest, first), take(dest, second)
    weights = jnp.pad(jnp.stack([take(comb, first), take(comb, second)], axis=1), ((0, 0), (0, 126)))

    x_sorted = _gather_rows(h2, src, tm)
    y_sorted = _moe_experts(x_sorted, tile_expert, n_active, w1, w3, w2, tm)
    return _moe_combine(x, y_sorted, pos_a, pos_b, weights, tm)


def _matmul_kernel(*refs, has_res, cached):
    a_ref, w_ref = refs[0], refs[1]
    res_ref = refs[2] if has_res else None
    o_ref = refs[3] if has_res else refs[2]
    if cached:
        wc_ref = refs[-1]

        @pl.when(pl.program_id(1) == 0)
        def _():
            wc_ref[...] = w_ref[...].astype(BF16)

        w = wc_ref[...]
    else:
        w = w_ref[...]
    out = jnp.dot(a_ref[...], w, preferred_element_type=F32)
    if has_res:
        out = out + res_ref[...]
    o_ref[...] = out.astype(o_ref.dtype)


def _matmul(a, w, *, tm, tn, out_dtype=F32, res=None, name="matmul"):
    m, kdim = a.shape
    n = w.shape[1]
    tm, tn = min(tm, m), min(tn, n)
    assert m % tm == 0 and n % tn == 0
    cached = w.dtype != BF16
    in_specs = [pl.BlockSpec((tm, kdim), lambda j, i: (i, 0)), pl.BlockSpec((kdim, tn), lambda j, i: (0, j))]
    args = [a, w]
    if res is not None:
        in_specs.append(pl.BlockSpec((tm, tn), lambda j, i: (i, j)))
        args.append(res)
    return pl.pallas_call(
        functools.partial(_matmul_kernel, has_res=res is not None, cached=cached),
        grid=(n // tn, m // tm),
        in_specs=in_specs,
        out_specs=pl.BlockSpec((tm, tn), lambda j, i: (i, j)),
        out_shape=jax.ShapeDtypeStruct((m, n), out_dtype),
        scratch_shapes=[pltpu.VMEM((kdim, tn), BF16)] if cached else [],
        compiler_params=_params(2),
        name=name,
    )(*args)


def _swiglu_up_kernel(h_ref, w1_ref, w3_ref, o_ref):
    h = h_ref[...]
    a = jnp.dot(h, w1_ref[...].astype(BF16), preferred_element_type=F32)
    b = jnp.dot(h, w3_ref[...].astype(BF16), preferred_element_type=F32)
    o_ref[...] = (a * jax.nn.sigmoid(a) * b).astype(o_ref.dtype)


def _swiglu_up(h, w1, w3, tm=1024, tn=512):
    m, kdim = h.shape
    n = w1.shape[1]
    tm, tn = min(tm, m), min(tn, n)
    assert m % tm == 0 and n % tn == 0
    return pl.pallas_call(
        _swiglu_up_kernel,
        grid=(n // tn, m // tm),
        in_specs=[pl.BlockSpec((tm, kdim), lambda j, i: (i, 0)),
                  pl.BlockSpec((kdim, tn), lambda j, i: (0, j)),
                  pl.BlockSpec((kdim, tn), lambda j, i: (0, j))],
        out_specs=pl.BlockSpec((tm, tn), lambda j, i: (i, j)),
        out_shape=jax.ShapeDtypeStruct((m, n), BF16),
        compiler_params=_params(2),
        name="swiglu_up",
    )(h, w1, w3)


def _shift_rows(x, carry_row):
    rolled = pltpu.roll(x, 1, axis=0)
    row = lax.broadcasted_iota(jnp.int32, x.shape, 0)
    return jnp.where(row == 0, carry_row, rolled)


def _rwkv_prep_kernel(*refs, has_vres):
    (u_ref, mu_ref, w0_ref, w2_ref, a0_ref, a2_ref, g2_ref, kkw_ref, kaw_ref) = refs[:9]
    pos = 9
    if has_vres:
        vd_ref, vmu_ref, v0_ref, vup_ref, vfirst_ref = refs[pos:pos + 5]
        pos += 5
    r_ref, lw_ref, k_ref, v_ref, kk_ref, b_ref, g_ref = refs[pos:pos + 7]
    pos += 7
    cu_ref = refs[pos]
    cv_ref = refs[pos + 1] if has_vres else None
    tt = u_ref.shape[0]

    @pl.when(pl.program_id(1) == 0)
    def _():
        cu_ref[...] = jnp.zeros_like(cu_ref)
        if has_vres:
            cv_ref[...] = jnp.zeros_like(cv_ref)

    u = u_ref[...]
    prev = _shift_rows(u, cu_ref[0:1, :])
    cu_ref[0:1, :] = u[tt - 1:tt, :]
    uf = u + (prev - u) * mu_ref[...]
    r, k, v = uf[:, 0:512], uf[:, 512:1024], uf[:, 1024:1536]
    wd, ad, gd = uf[:, 1536:1664], uf[:, 1664:1792], uf[:, 1792:2048]

    x = w0_ref[...] + _mm(jnp.tanh(wd), w2_ref[...], 2, 2)
    softplus = jnp.maximum(-x, 0.0) + jnp.log(1.0 + jnp.exp(-jnp.abs(x)))
    lw_ref[...] = -jnp.exp(-softplus - 0.5)
    a = jax.nn.sigmoid(a0_ref[...] + _mm(ad, a2_ref[...], 2, 2))
    g_ref[...] = _mm(jax.nn.sigmoid(gd), g2_ref[...], 2, 2)
    if has_vres:
        vd = vd_ref[...]
        vprev = _shift_rows(vd, cv_ref[0:1, :])
        cv_ref[0:1, :] = vd[tt - 1:tt, :]
        vdf = vd + (vprev - vd) * vmu_ref[...]
        v = v + (vfirst_ref[...] - v) * jax.nn.sigmoid(v0_ref[...] + _mm(vdf, vup_ref[...], 2, 2))
    kk = k * kkw_ref[...]
    ss = _group_mean(kk * kk, HEAD) * HEAD
    kk = kk * lax.rsqrt(jnp.maximum(ss, 1e-24))
    r_ref[...] = r
    k_ref[...] = k * (1.0 + (a - 1.0) * kaw_ref[...])
    v_ref[...] = v
    kk_ref[...] = kk
    b_ref[...] = kk * a


def _rwkv_prep(p_small, bsz, seq, mu, w0, w2, a0, a2, g2, kkw, kaw, vres, tt=256):
    tt = min(tt, seq)
    nt = seq // tt
    n = bsz * seq
    row = lambda b, t: (b * nt + t, 0)
    const = lambda b, t: (0, 0)
    pad_rows = lambda w: jnp.pad(w, ((0, 128 - w.shape[0]), (0, 0)))
    mu_p = jnp.concatenate([mu[:1536], jnp.pad(mu[1536:1632], (0, 32)), jnp.pad(mu[1632:1728], (0, 32)), mu[1728:]])
    vec = lambda a: a.reshape(1, -1)
    args = [p_small, vec(mu_p), vec(w0), pad_rows(w2), vec(a0), pad_rows(a2), g2, vec(kkw), vec(kaw)]
    in_specs = [pl.BlockSpec((tt, 2048), row), pl.BlockSpec((1, 2048), const), pl.BlockSpec((1, 512), const),
                pl.BlockSpec((128, 512), const), pl.BlockSpec((1, 512), const), pl.BlockSpec((128, 512), const),
                pl.BlockSpec((256, 512), const), pl.BlockSpec((1, 512), const), pl.BlockSpec((1, 512), const)]
    scratch = [pltpu.VMEM((8, 2048), F32)]
    if vres is not None:
        vmu, v0, vup, vfirst = vres
        args += [p_small, vec(jnp.pad(vmu, (0, 64))), vec(v0), pad_rows(vup), vfirst]
        in_specs += [pl.BlockSpec((tt, 128), lambda b, t: (b * nt + t, C_VRES // 128)), pl.BlockSpec((1, 128), const),
                     pl.BlockSpec((1, 512), const), pl.BlockSpec((128, 512), const), pl.BlockSpec((tt, 512), row)]
        scratch.append(pltpu.VMEM((8, 128), F32))
    return pl.pallas_call(
        functools.partial(_rwkv_prep_kernel, has_vres=vres is not None),
        grid=(bsz, nt),
        in_specs=in_specs,
        out_specs=[pl.BlockSpec((tt, 512), row)] * 7,
        out_shape=[jax.ShapeDtypeStruct((n, 512), F32)] * 7,
        scratch_shapes=scratch,
        compiler_params=_params(2),
        name="rwkv_prep",
    )(*args)


def _rwkv_chunk_kernel(r_ref, lw_ref, k_ref, v_ref, kk_ref, b_ref, g_ref, rk_ref, lnw_ref, lnb_ref, o_ref, s_ref,
                       *, pw):
    nb, c, _ = r_ref.shape

    @pl.when(pl.program_id(0) == 0)
    def _():
        s_ref[...] = jnp.zeros_like(s_ref)

    row = lax.broadcasted_iota(jnp.int32, (c, c), 0)
    col = lax.broadcasted_iota(jnp.int32, (c, c), 1)
    tril_incl = jnp.where(row >= col, 1.0, 0.0).astype(BF16)
    strict = row > col
    incl = row >= col
    eye = jnp.where(row == col, 1.0, 0.0)
    hrow = lax.broadcasted_iota(jnp.int32, (HEAD, HEAD), 0)
    hcol = lax.broadcasted_iota(jnp.int32, (HEAD, HEAD), 1)
    eye_h = jnp.where(hrow == hcol, 1.0, 0.0).astype(BF16)
    mm = functools.partial(_mm, pa=pw, pb=pw)
    transpose = lambda a, pieces: _mm(eye_h, a, pb=pieces, nt=True)

    heads = [(bi, h) for bi in range(nb) for h in range(RWKV_HEADS)]
    per_head = lambda full: [full[bi][:, h * HEAD:(h + 1) * HEAD] for bi, h in heads]
    each = lambda f, *lists: [f(*vals) for vals in zip(*lists)]

    r, lw, k, v, kk, b = ([ref[bi] for bi in range(nb)] for ref in (r_ref, lw_ref, k_ref, v_ref, kk_ref, b_ref))
    cum = [_mm(tril_incl, x, pb=3) for x in lw]
    g_inv = [jnp.exp(-x) for x in cum]
    a_t = per_head(each(lambda kk_, c_, lw_: kk_ * jnp.exp(c_ - lw_), kk, cum, lw))
    b_t = per_head(each(lambda b_, gi: b_ * gi, b, g_inv))
    k_t = per_head(each(lambda k_, gi: k_ * gi, k, g_inv))
    r_t = per_head(each(lambda r_, c_: r_ * jnp.exp(c_), r, cum))
    vh = per_head(v)
    tail = [jnp.exp(x[c - 1:c, :] - x) for x in cum]
    bh = per_head(each(lambda b_, t_: b_ * t_, b, tail))
    kh = per_head(each(lambda k_, t_: k_ * t_, k, tail))
    g_last = per_head([jnp.exp(x[c - 1:c, :]) for x in cum])

    a_ab = each(lambda x, y: jnp.where(strict, mm(x, y, nt=True), 0.0), a_t, b_t)
    a_ak = each(lambda x, y: jnp.where(strict, mm(x, y, nt=True), 0.0), a_t, k_t)
    a_rb = each(lambda x, y: jnp.where(incl, mm(x, y, nt=True), 0.0), r_t, b_t)
    a_rk = each(lambda x, y: jnp.where(incl, mm(x, y, nt=True), 0.0), r_t, k_t)
    x = [eye - n for n in a_ab]
    p = each(mm, a_ab, a_ab)
    steps = int(np.log2(c)) - 1
    for i in range(steps):
        x = each(lambda x_, p_: x_ + mm(x_, p_), x, p)
        if i + 1 < steps:
            p = each(mm, p, p)
    w1 = each(mm, x, a_t)
    akv = each(mm, a_ak, vh)
    w2 = each(mm, x, akv)
    y_in = each(lambda ark, v_, arb, w2_: mm(ark, v_) - mm(arb, w2_), a_rk, vh, a_rb, w2)
    q_h = each(lambda r_, arb, w1_: r_ - mm(arb, w1_), r_t, a_rb, w1)
    bh_t = [transpose(x_, pw) for x_ in bh]
    kh_t = [transpose(x_, pw) for x_ in kh]
    decay = [transpose(jnp.broadcast_to(x_, (HEAD, HEAD)), 3) for x_ in g_last]
    g_s = each(lambda kt_, v_, bt_, w2_: mm(kt_, v_) - mm(bt_, w2_), kh_t, vh, bh_t, w2)
    s = [s_ref[i] for i in range(len(heads))]
    ys = each(lambda yi, q_, s_: yi + mm(q_, s_), y_in, q_h, s)
    w1s = each(mm, w1, s)
    for i, (d_, s_, bt_, ws_, gs_) in enumerate(zip(decay, s, bh_t, w1s, g_s)):
        s_ref[i] = d_ * s_ - mm(bt_, ws_) + gs_

    for bi in range(nb):
        y = jnp.concatenate(ys[bi * RWKV_HEADS:(bi + 1) * RWKV_HEADS], axis=1)
        mean = _group_mean(y, HEAD)
        d = y - mean
        var = _group_mean(d * d, HEAD)
        yn = d * lax.rsqrt(var + RWKV_GN_EPS) * lnw_ref[...] + lnb_ref[...]
        bonus = _group_mean(r[bi] * k[bi] * rk_ref[...], HEAD) * HEAD * v[bi]
        o_ref[bi] = ((yn + bonus) * g_ref[bi]).astype(o_ref.dtype)


def _rwkv_recurrence(r, lw, k, v, kk, b, g, rk, lnw, lnb, bsz, seq, pw=1):
    c = min(RWKV_CHUNK, seq)
    blk = pl.BlockSpec((bsz, c, BRANCH_W), lambda t: (0, t, 0))
    cst = pl.BlockSpec((1, BRANCH_W), lambda t: (0, 0))
    as3 = lambda a: a.reshape(bsz, seq, BRANCH_W)
    out = pl.pallas_call(
        functools.partial(_rwkv_chunk_kernel, pw=pw),
        grid=(seq // c,),
        in_specs=[blk] * 7 + [cst] * 3,
        out_specs=blk,
        out_shape=jax.ShapeDtypeStruct((bsz, seq, BRANCH_W), BF16),
        scratch_shapes=[pltpu.VMEM((bsz * RWKV_HEADS, HEAD, HEAD), F32)],
        compiler_params=_params(1),
        name="rwkv_chunk",
    )(as3(r), as3(lw), as3(k), as3(v), as3(kk), as3(b), as3(g), rk.reshape(1, -1), lnw.reshape(1, -1),
      lnb.reshape(1, -1))
    return out.reshape(bsz * seq, BRANCH_W)


def _eye(n):
    r = lax.broadcasted_iota(jnp.int32, (n, n), 0)
    c = lax.broadcasted_iota(jnp.int32, (n, n), 1)
    return jnp.where(r == c, 1.0, 0.0).astype(BF16)


def _transpose_bf16(x):
    return _dg(_eye(x.shape[1]), x.astype(BF16), nt=True).astype(BF16)


def _nsa_norm_kernel(q_ref, ks_ref, kw_ref, vs_ref, vw_ref, gq_ref, gs_ref, gw_ref,
                     qo_ref, kso_ref, kwo_ref, vso_ref, vwo_ref):
    for x_ref, g_ref, o_ref in ((q_ref, gq_ref, qo_ref), (ks_ref, gs_ref, kso_ref), (kw_ref, gw_ref, kwo_ref)):
        x = x_ref[...]
        ms = _group_mean(x * x, HEAD)
        o_ref[...] = (x * lax.rsqrt(ms + NORM_EPS) * g_ref[...]).astype(o_ref.dtype)
    vso_ref[...] = _transpose_bf16(vs_ref[...])
    vwo_ref[...] = _transpose_bf16(vw_ref[...])


def _nsa_norm(p_small, qk_gain, bsz, seq, tt=512):
    n = p_small.shape[0]
    tt = min(tt, seq)
    nt = seq // tt
    col = lambda c, w: pl.BlockSpec((tt, w), lambda i: (i, c // w))
    cst = lambda w: pl.BlockSpec((1, w), lambda i: (0, 0))
    out = lambda w: pl.BlockSpec((tt, w), lambda i: (i, 0))
    out_t = pl.BlockSpec((128, tt), lambda i: (i // nt, i % nt))
    gq = (jnp.tile(qk_gain[0], 8) * HEAD ** -0.5).reshape(1, 512)
    gs = jnp.tile(qk_gain[2], 2).reshape(1, 128)
    gw = jnp.tile(qk_gain[3], 2).reshape(1, 128)
    return pl.pallas_call(
        _nsa_norm_kernel,
        grid=(n // tt,),
        in_specs=[col(C_NSAQ, 512), col(C_KS, 128), col(C_KW, 128), col(C_VS, 128), col(C_VW, 128),
                  cst(512), cst(128), cst(128)],
        out_specs=[out(512), out(128), out(128), out_t, out_t],
        out_shape=[jax.ShapeDtypeStruct((n, 512), BF16)] + [jax.ShapeDtypeStruct((n, 128), BF16)] * 2
        + [jax.ShapeDtypeStruct((bsz * 128, seq), BF16)] * 2,
        compiler_params=_params(1),
        name="nsa_norm",
    )(p_small, p_small, p_small, p_small, p_small, gq, gs, gw)


def _gelu_tanh(x):
    return 0.5 * x * (1.0 + jnp.tanh(0.7978845608028654 * (x + 0.044715 * x * x * x)))


def _nsa_compress_kernel(x_ref, pos_ref, w1_ref, b1_ref, w2_ref, gain_ref, o_ref):
    half = x_ref.shape[-1]
    x = x_ref[0, 0, 0].astype(BF16)
    w1 = w1_ref[0].astype(BF16)
    z_lo = jnp.dot(x, w1[:half], preferred_element_type=F32)
    z_hi = jnp.dot(x, w1[half:], preferred_element_type=F32)
    nrow = z_hi.shape[0]
    pos = jnp.broadcast_to(pos_ref[0], (8, 2 * half))
    const = _mm(pos, w1, pa=2)[0:1] + b1_ref[0]
    pre = z_lo + pltpu.roll(z_hi, nrow - 1, axis=0) + const
    out = jnp.dot(_gelu_tanh(pre).astype(BF16), w2_ref[0].astype(BF16), preferred_element_type=F32)
    is_key = pl.program_id(0) == 0
    normed = out * lax.rsqrt(jnp.mean(out * out, axis=-1, keepdims=True) + NORM_EPS) * gain_ref[...]
    o_ref[0, 0, 0] = jnp.where(is_key, normed, out)


def _nsa_compress(kc_vc, cmp_pos, cmp_w1, cmp_b1, cmp_w2, gain):
    _, bsz, hkv, nc, wid = kc_vc.shape
    return pl.pallas_call(
        _nsa_compress_kernel,
        grid=(2, bsz, hkv),
        in_specs=[pl.BlockSpec((1, 1, 1, nc, wid), lambda i, b, h: (i, b, h, 0, 0)),
                  pl.BlockSpec((1, 1, 2 * wid), lambda i, b, h: (i, 0, 0)),
                  pl.BlockSpec((1, 2 * wid, 2 * HEAD), lambda i, b, h: (i, 0, 0)),
                  pl.BlockSpec((1, 1, 2 * HEAD), lambda i, b, h: (i, 0, 0)),
                  pl.BlockSpec((1, 2 * HEAD, HEAD), lambda i, b, h: (i, 0, 0)),
                  pl.BlockSpec((1, HEAD), lambda i, b, h: (0, 0))],
        out_specs=pl.BlockSpec((1, 1, 1, nc, HEAD), lambda i, b, h: (i, b, h, 0, 0)),
        out_shape=jax.ShapeDtypeStruct((2, bsz, hkv, nc, HEAD), F32),
        compiler_params=_params(3),
        name="nsa_compress",
    )(kc_vc, cmp_pos.reshape(2, 1, 2 * wid), cmp_w1, cmp_b1.reshape(2, 1, 2 * HEAD), cmp_w2, gain.reshape(1, HEAD))


def _nsa_cmp_kernel(q_ref, kc_ref, vc_ref, ov_ref, o_ref, sel_ref, *, tq):
    hkv, ncmp = kc_ref.shape[1], kc_ref.shape[2]
    nsel = ov_ref.shape[0]
    g = NSA_GROUP
    rows = g * tq
    start = pl.program_id(1) * tq
    q_all = q_ref[...]
    t_pos = start + (lax.broadcasted_iota(jnp.int32, (ncmp, rows), 1) & (tq - 1))
    c_end = lax.broadcasted_iota(jnp.int32, (ncmp, rows), 0) * COMP_STRIDE + (COMP_L - 1)
    mask = c_end <= t_pos
    sid = lax.broadcasted_iota(jnp.int32, (nsel, tq), 0)
    cur = (start + lax.broadcasted_iota(jnp.int32, (nsel, tq), 1)) // SEL_L
    forced = (sid == 0) | (sid == cur) | (sid == cur - 1)
    k_top = min(SEL_N, nsel)
    outs, sels = [], []
    for h in range(hkv):
        q = _group_rows(q_all, h, g)
        s = jnp.where(mask, _dg(kc_ref[0, h].astype(BF16), q, nt=True), -1e30)
        e = jnp.exp(s - jnp.max(s, axis=0, keepdims=True))
        p = jnp.where(mask, e / jnp.sum(e, axis=0, keepdims=True), 0.0)
        outs.append(jnp.dot(_transpose_bf16(vc_ref[0, h]), p.astype(BF16), preferred_element_type=F32))
        p_sum = p[:, 0:tq]
        for i in range(1, g):
            p_sum = p_sum + p[:, i * tq:(i + 1) * tq]
        imp = _mm(ov_ref[...], p_sum, pb=3)
        val = jnp.where(forced, 1e9, jnp.where(sid <= cur, imp, -1e9))
        rank = jnp.zeros(imp.shape, jnp.int32)
        for j in range(nsel):
            cj = val[j:j + 1, :]
            ahead = (cj > val) | ((cj == val) & (sid > j))
            rank = rank + ahead.astype(jnp.int32)
        sels.append(jnp.where((rank < k_top) & (val > -1e8), 1.0, 0.0))
    o_ref[...] = jnp.concatenate(outs, axis=0)
    sel_ref[...] = jnp.concatenate(sels, axis=0).astype(sel_ref.dtype)


def _group_rows(q_all, h, g):
    return jnp.concatenate([q_all[:, (h * g + i) * HEAD:(h * g + i + 1) * HEAD] for i in range(g)], axis=0)


NSA_Q_TILE = 256


def _nsa_cmp(q_n, k_cmp, v_cmp, overlap_t, bsz, seq, tq=NSA_Q_TILE):
    hkv = k_cmp.shape[1]
    nsel, ncmp = overlap_t.shape
    nt = seq // tq
    rows = NSA_GROUP * tq
    return pl.pallas_call(
        functools.partial(_nsa_cmp_kernel, tq=tq),
        grid=(bsz, nt),
        in_specs=[pl.BlockSpec((tq, BRANCH_W), lambda b, t: (b * nt + t, 0)),
                  pl.BlockSpec((1, hkv, ncmp, HEAD), lambda b, t: (b, 0, 0, 0)),
                  pl.BlockSpec((1, hkv, ncmp, HEAD), lambda b, t: (b, 0, 0, 0)),
                  pl.BlockSpec((nsel, ncmp), lambda b, t: (0, 0))],
        out_specs=[pl.BlockSpec((hkv * HEAD, rows), lambda b, t: (b * nt + t, 0)),
                   pl.BlockSpec((hkv * nsel, tq), lambda b, t: (b * nt + t, 0))],
        out_shape=[jax.ShapeDtypeStruct((bsz * nt * hkv * HEAD, rows), F32),
                   jax.ShapeDtypeStruct((bsz * nt * hkv * nsel, tq), BF16)],
        compiler_params=_params(2),
        name="nsa_cmp",
    )(q_n, k_cmp, v_cmp, overlap_t)


SCORE_MASKED = -1e30
SCORE_FLOOR = -1e20


def _nsa_attn_kernel(q_ref, sel_ref, ks_ref, vs_ref, kw_ref, vw_ref, oc_ref, gl_ref, o_ref, *, tq, kt):
    g = NSA_GROUP
    hkv = ks_ref.shape[1] // HEAD
    seq = ks_ref.shape[0]
    nsel = sel_ref.shape[0] // hkv
    rows = g * tq
    start = pl.program_id(1) * tq
    q_all = q_ref[...]
    sel_all = sel_ref[...]
    qs = [_group_rows(q_all, h, g) for h in range(hkv)]
    sels = [sel_all[h * nsel:(h + 1) * nsel, :] for h in range(hkv)]
    head_cols = lambda x, h: x[:, h * HEAD:(h + 1) * HEAD]
    per_query = lambda x: jnp.concatenate([x] * g, axis=1)
    with_ones = lambda vt: jnp.concatenate([vt, jnp.ones((16, vt.shape[1]), BF16)], axis=0)

    key_pos = lax.broadcasted_iota(jnp.int32, (kt, tq), 0)
    t_pos = start + lax.broadcasted_iota(jnp.int32, (kt, tq), 1)
    blk_of_key = lax.broadcasted_iota(jnp.int32, (kt, nsel), 0) // SEL_L
    blk_id = lax.broadcasted_iota(jnp.int32, (kt, nsel), 1)

    def key_tile(j, carry, causal):
        base = pl.multiple_of(j * kt, kt)
        kb2 = ks_ref[pl.ds(base, kt), :]
        vt2 = vs_ref[:, pl.ds(base, kt)]
        expand = jnp.where(blk_of_key + j * (kt // SEL_L) == blk_id, 1.0, 0.0).astype(BF16)
        out = []
        for h in range(hkv):
            m, acc = carry[h]
            keep = jnp.dot(expand, sels[h], preferred_element_type=F32) > 0.5
            if causal:
                keep = keep & (key_pos + base <= t_pos)
            bias = per_query(jnp.where(keep, 0.0, SCORE_MASKED))
            s = _dg(head_cols(kb2, h), qs[h], nt=True) + bias
            m_new = jnp.maximum(m, jnp.max(s, axis=0, keepdims=True))
            p = jnp.exp((s - m_new).astype(BF16))
            vt = with_ones(vt2[h * HEAD:(h + 1) * HEAD, :])
            acc = jnp.exp(m - m_new) * acc + jnp.dot(vt, p, preferred_element_type=F32)
            out.append((m_new, acc))
        return tuple(out)

    init = tuple((jnp.full((1, rows), SCORE_FLOOR, F32), jnp.zeros((HEAD + 16, rows), F32)) for _ in range(hkv))
    n_full = start // kt
    carry = lax.fori_loop(0, n_full, lambda j, c: key_tile(j, c, False), init)
    carry = key_tile(n_full, carry, True)
    o_sel = [acc[0:HEAD] / acc[HEAD:HEAD + 1] for _, acc in carry]

    span = min(WINDOW + tq, seq)
    wbase = pl.multiple_of(jnp.maximum(start - WINDOW, 0), tq) if seq > span else 0
    kb2 = kw_ref[pl.ds(wbase, span), :]
    vt2 = vw_ref[:, pl.ds(wbase, span)]
    tw = start + lax.broadcasted_iota(jnp.int32, (span, tq), 1)
    wpos = wbase + lax.broadcasted_iota(jnp.int32, (span, tq), 0)
    wbias = per_query(jnp.where((wpos <= tw) & (wpos > tw - WINDOW), 0.0, SCORE_MASKED))
    o_win = []
    for h in range(hkv):
        s = _dg(head_cols(kb2, h), qs[h], nt=True) + wbias
        p = jnp.exp((s - jnp.max(s, axis=0, keepdims=True)).astype(BF16))
        acc = jnp.dot(with_ones(vt2[h * HEAD:(h + 1) * HEAD, :]), p, preferred_element_type=F32)
        o_win.append(acc[0:HEAD] / acc[HEAD:HEAD + 1])

    gate_t = _mm(_eye(128), jax.nn.sigmoid(gl_ref[...]), pb=3, nt=True)
    o_cmp = oc_ref[...]
    blocks = []
    for h in range(hkv):
        for i in range(g):
            cols = slice(i * tq, (i + 1) * tq)
            r = (h * g + i) * 3
            blocks.append(gate_t[r:r + 1] * o_cmp[h * HEAD:(h + 1) * HEAD, cols]
                          + gate_t[r + 1:r + 2] * o_sel[h][:, cols] + gate_t[r + 2:r + 3] * o_win[h][:, cols])
    y_t = jnp.concatenate(blocks, axis=0)
    o_ref[...] = _dg(_eye(tq), y_t.astype(BF16), nt=True).astype(o_ref.dtype)


def _nsa_attn(p_small, q_n, sel_t, ks, vs_t, kw, vw_t, o_cmp_t, bsz, seq, tq=NSA_Q_TILE, kt=512):
    kt = min(kt, seq)
    nt = seq // tq
    hkv = NSA_KV_HEADS
    nsel = seq // SEL_L
    row = lambda w: pl.BlockSpec((tq, w), lambda b, t: (b * nt + t, 0))
    tile = lambda r, w: pl.BlockSpec((r, w), lambda b, t: (b * nt + t, 0))
    k_spec = pl.BlockSpec((seq, hkv * HEAD), lambda b, t: (b, 0))
    vt_spec = pl.BlockSpec((hkv * HEAD, seq), lambda b, t: (b, 0))
    return pl.pallas_call(
        functools.partial(_nsa_attn_kernel, tq=tq, kt=kt),
        grid=(bsz, nt),
        in_specs=[row(BRANCH_W), tile(hkv * nsel, tq), k_spec, vt_spec, k_spec, vt_spec,
                  tile(hkv * HEAD, NSA_GROUP * tq),
                  pl.BlockSpec((tq, 128), lambda b, t: (b * nt + t, C_GL // 128))],
        out_specs=row(BRANCH_W),
        out_shape=jax.ShapeDtypeStruct((bsz * seq, BRANCH_W), BF16),
        compiler_params=_params(2),
        name="nsa_attn",
    )(q_n, sel_t, ks, vs_t, kw, vw_t, o_cmp_t, p_small)


def _nsa(p_small, bsz, seq, qk_gain, cmp_pos, cmp_w1, cmp_b1, cmp_w2):
    hkv = NSA_KV_HEADS
    q_n, ks_n, kw_n, vs_t, vw_t = _nsa_norm(p_small, qk_gain, bsz, seq)
    ncmp = seq // COMP_STRIDE
    grouped = lambda c: p_small[:, c:c + 128].reshape(bsz, ncmp, COMP_STRIDE, hkv, HEAD).transpose(0, 3, 1, 2, 4)
    kc_vc = jnp.stack([grouped(C_KC), grouped(C_VC)]).reshape(2, bsz, hkv, ncmp, COMP_STRIDE * HEAD)
    cmp = _nsa_compress(kc_vc, cmp_pos, cmp_w1, cmp_b1, cmp_w2, qk_gain[1])
    nsel = seq // SEL_L
    c0 = np.arange(ncmp)[None, :] * COMP_STRIDE
    s0 = np.arange(nsel)[:, None] * SEL_L
    overlap_t = np.clip(np.minimum(c0 + COMP_L, s0 + SEL_L) - np.maximum(c0, s0), 0, None) / COMP_L
    o_cmp_t, sel_t = _nsa_cmp(q_n, cmp[0], cmp[1], jnp.asarray(overlap_t, BF16), bsz, seq)
    return _nsa_attn(p_small, q_n, sel_t, ks_n, vs_t, kw_n, vw_t, o_cmp_t, bsz, seq)


HALO = 16


def _conv_pool_kernel(bg_ref, cg_ref, xi_ref, pu_ref, cw_ref, conv_ref, pool_ref, cz_ref, cp_ref):
    tt = xi_ref.shape[0]
    ti = pl.program_id(1)

    @pl.when(ti == 0)
    def _():
        cz_ref[...] = jnp.zeros_like(cz_ref)
        cp_ref[...] = jnp.zeros_like(cp_ref)

    def history(x, carry_ref):
        ext = jnp.concatenate([carry_ref[...], x], axis=0)
        carry_ref[...] = x[tt - HALO:tt, :]
        return ext

    lag = lambda ext, s: pltpu.roll(ext, s, axis=0)
    body = lambda ext: ext[HALO:HALO + tt, :]

    z = cg_ref[...] * xi_ref[...]
    ze = history(z, cz_ref)
    cw = cw_ref[...]
    y = cw[2:3, :] * z + cw[1:2, :] * body(lag(ze, 1)) + cw[0:1, :] * body(lag(ze, 2))
    conv_ref[...] = (bg_ref[...] * y).astype(conv_ref.dtype)

    u = pu_ref[...]
    sums = [history(u, cp_ref)]
    for w in (1, 2, 4, 8):
        sums.append(sums[-1] + lag(sums[-1], w))
    count = (ti * tt + 1 + lax.broadcasted_iota(jnp.int32, (tt, 128), 0)).astype(F32)
    outs = []
    for gi, w in enumerate(POOL_WINDOWS):
        sl = slice(gi * 128, (gi + 1) * 128)
        outs.append(body(sums[gi + 1])[:, sl] / jnp.minimum(count, float(w)) - u[:, sl])
    pool_ref[...] = jnp.concatenate(outs, axis=1).astype(pool_ref.dtype)


def _conv_pool(p_small, bsz, seq, conv_w, tt=512):
    tt = min(tt, seq)
    nt = seq // tt
    n = bsz * seq
    col = lambda c: pl.BlockSpec((tt, 512), lambda b, t: (b * nt + t, c // 512))
    row = pl.BlockSpec((tt, 512), lambda b, t: (b * nt + t, 0))
    return pl.pallas_call(
        _conv_pool_kernel,
        grid=(bsz, nt),
        in_specs=[col(C_CONV), col(C_CONV + 512), col(C_CONV + 1024), col(C_POOL),
                  pl.BlockSpec((8, 512), lambda b, t: (0, 0))],
        out_specs=[row, row],
        out_shape=[jax.ShapeDtypeStruct((n, 512), BF16)] * 2,
        scratch_shapes=[pltpu.VMEM((HALO, 512), F32)] * 2,
        compiler_params=_params(2),
        name="conv_pool",
    )(p_small, p_small, p_small, p_small, jnp.pad(conv_w, ((0, 8 - conv_w.shape[0]), (0, 0))))


def _mem_attn_kernel(q_ref, kv_ref, gq_ref, gk_ref, o_ref, *, scale):
    outs = []
    for h in range(MEM_HEADS):
        sl = slice(h * MEM_HEAD, (h + 1) * MEM_HEAD)
        q = q_ref[:, sl]
        q = q * lax.rsqrt(jnp.mean(q * q, axis=-1, keepdims=True) + NORM_EPS) * gq_ref[...]
        k = kv_ref[0, :, sl]
        k = k * lax.rsqrt(jnp.mean(k * k, axis=-1, keepdims=True) + NORM_EPS) * gk_ref[...]
        v = kv_ref[0, :, BRANCH_W + h * MEM_HEAD:BRANCH_W + (h + 1) * MEM_HEAD]
        s = _dg(q.astype(BF16), k.astype(BF16), nt=True) * scale
        e = jnp.exp(s - jnp.max(s, axis=-1, keepdims=True))
        p = e / jnp.sum(e, axis=-1, keepdims=True)
        outs.append(jnp.dot(p.astype(BF16), v.astype(BF16), preferred_element_type=F32))
    o_ref[...] = jnp.concatenate(outs, axis=1).astype(o_ref.dtype)


def _mem_attn(p_small, kv, bsz, seq, qk_gain, tq=512):
    tq = min(tq, seq)
    nt = seq // tq
    mlen = kv.shape[1]
    return pl.pallas_call(
        functools.partial(_mem_attn_kernel, scale=MEM_HEAD ** -0.5),
        grid=(bsz, nt),
        in_specs=[pl.BlockSpec((tq, 512), lambda b, t: (b * nt + t, C_MEM // 512)),
                  pl.BlockSpec((1, mlen, 2 * BRANCH_W), lambda b, t: (b, 0, 0)),
                  pl.BlockSpec((1, MEM_HEAD), lambda b, t: (0, 0)),
                  pl.BlockSpec((1, MEM_HEAD), lambda b, t: (0, 0))],
        out_specs=pl.BlockSpec((tq, 512), lambda b, t: (b * nt + t, 0)),
        out_shape=jax.ShapeDtypeStruct((bsz * seq, 512), BF16),
        compiler_params=_params(2),
        name="mem_attn",
    )(p_small, kv, qk_gain[0].reshape(1, -1), qk_gain[1].reshape(1, -1))


def _merge_kernel(h_ref, *refs):
    wg_refs, y_refs = refs[0:5], refs[5:9]
    wb_ref, pooled_ref, wp_ref, ps_ref, o_ref, wgc_ref, wbc_ref = refs[9:]

    @pl.when(pl.program_id(1) == 0)
    def _():
        for i in range(5):
            wgc_ref[i] = wg_refs[i][...].astype(BF16)
        wbc_ref[...] = wb_ref[...].astype(BF16)

    h = h_ref[...]
    gate = lambda i: jax.nn.sigmoid(jnp.dot(h, wgc_ref[i], preferred_element_type=F32))
    z_pool = jnp.dot(pooled_ref[...], wp_ref[0].astype(BF16), preferred_element_type=F32) * ps_ref[...]
    acc = gate(4) * z_pool
    for i in range(4):
        acc = acc + gate(i) * jnp.dot(y_refs[i][...], wbc_ref[i], preferred_element_type=F32)
    o_ref[...] = acc.astype(o_ref.dtype)


def _merge(h, w_gate, ys, w_branch, pooled, pool_w, pool_scale, tm=512):
    n = h.shape[0]
    tm = min(tm, n)
    tn = 512
    nj = D_MODEL // tn
    once = pl.Buffered(1)
    gate_spec = lambda g: pl.BlockSpec((D_MODEL, tn), lambda j, i: (0, g * nj + j), pipeline_mode=once)
    y_spec = pl.BlockSpec((tm, BRANCH_W), lambda j, i: (i, 0))
    return pl.pallas_call(
        _merge_kernel,
        grid=(nj, n // tm),
        in_specs=[pl.BlockSpec((tm, D_MODEL), lambda j, i: (i, 0))] + [gate_spec(g) for g in range(5)]
        + [y_spec] * 4
        + [pl.BlockSpec((4, BRANCH_W, tn), lambda j, i: (0, 0, j), pipeline_mode=once),
           pl.BlockSpec((tm, 128), lambda j, i: (i, j)),
           pl.BlockSpec((1, 128, tn), lambda j, i: (j, 0, 0)),
           pl.BlockSpec((1, tn), lambda j, i: (0, j))],
        out_specs=pl.BlockSpec((tm, tn), lambda j, i: (i, j)),
        out_shape=jax.ShapeDtypeStruct((n, D_MODEL), BF16),
        scratch_shapes=[pltpu.VMEM((5, D_MODEL, tn), BF16), pltpu.VMEM((4, BRANCH_W, tn), BF16)],
        compiler_params=_params(2),
        name="merge",
    )(h, *([w_gate] * 5), *ys, w_branch, pooled, pool_w, pool_scale.reshape(1, -1))


def _pack_w_in(w, vres_w):
    pad = lambda a, width: jnp.pad(a, ((0, 0), (0, width - a.shape[1])))
    nsa = w[:, 1984:3288]
    vres_cols = pad(vres_w, 128) if vres_w is not None else jnp.zeros((D_MODEL, 128), w.dtype)
    small = jnp.concatenate([
        w[:, 0:1536], pad(w[:, 1536:1632], 128), pad(w[:, 1632:1728], 128), w[:, 1728:1984],
        nsa[:, 0:512], w[:, 3288:4824], w[:, 4824:5336], w[:, 5336:5848],
        nsa[:, 512:1280], pad(nsa[:, 1280:1304], 128), vres_cols], axis=1)
    return small, w[:, 5848:]


def kernel(x, mem, norm_mix, norm_ffn, norm_mem, w_in, rwkv_mu, rwkv_w0, rwkv_w2, rwkv_a0, rwkv_a2, rwkv_g2, rwkv_kk, rwkv_ka, rwkv_rk, rwkv_ln_w, rwkv_ln_b, vres_in, vres_mu, vres_v0, vres_up, nsa_qk_gain, nsa_cmp_pos, nsa_cmp_w1, nsa_cmp_b1, nsa_cmp_w2, conv_w, pool_w, pool_scale, mem_wkv, mem_qk_gain, w_branch, w_out, ffn_w1, ffn_w3, ffn_w2, moe_router, moe_w1, moe_w3, moe_w2):
    bsz, seq, d = x.shape
    n = bsz * seq
    depth = w_in.shape[0]
    mlen = mem.shape[1]
    xf = x.reshape(n, d)
    memf = mem.reshape(bsz * mlen, d)
    v_first = None
    for l in range(depth):
        h = _rmsnorm(xf, norm_mix[l])
        w_small, w_gate = _pack_w_in(w_in[l], vres_in[l - 1] if l > 0 else None)
        p_small = _matmul(h, w_small, tm=1024, tn=1024, name="in_proj")
        vres = (vres_mu[l - 1], vres_v0[l - 1], vres_up[l - 1], v_first) if l > 0 else None
        r, lw, k, v, kk, b, g = _rwkv_prep(p_small, bsz, seq, rwkv_mu[l], rwkv_w0[l], rwkv_w2[l], rwkv_a0[l],
                                           rwkv_a2[l], rwkv_g2[l], rwkv_kk[l], rwkv_ka[l], vres)
        if l == 0:
            v_first = v
        y_rwkv = _rwkv_recurrence(r, lw, k, v, kk, b, g, rwkv_rk[l], rwkv_ln_w[l], rwkv_ln_b[l], bsz, seq)
        y_nsa = _nsa(p_small, bsz, seq, nsa_qk_gain[l], nsa_cmp_pos[l], nsa_cmp_w1[l], nsa_cmp_b1[l], nsa_cmp_w2[l])
        y_conv, pooled = _conv_pool(p_small, bsz, seq, conv_w[l])
        mem_n = _rmsnorm(memf, norm_mem[l])
        kv = _matmul(mem_n, mem_wkv[l], tm=512, tn=512, name="mem_kv").reshape(bsz, mlen, 2 * BRANCH_W)
        y_mem = _mem_attn(p_small, kv, bsz, seq, mem_qk_gain[l])
        merged = _merge(h, w_gate, (y_rwkv, y_nsa, y_conv, y_mem), w_branch[l], pooled, pool_w[l], pool_scale[l])
        xf = _matmul(merged, w_out[l], tm=1024, tn=1024, res=xf, name="out_proj")

        if l % 2 == 0:
            h2 = _rmsnorm(xf, norm_ffn[l])
            e = l // 2
            act = _swiglu_up(h2, ffn_w1[e], ffn_w3[e])
            xf = _matmul(act, ffn_w2[e], tm=512, tn=512, res=xf, name="ffn_down")
        else:
            e = l // 2
            xf = _moe(xf, norm_ffn[l], moe_router[e], moe_w1[e], moe_w3[e], moe_w2[e])
    return xf.reshape(bsz, seq, d)
```

```python
import functools

import jax
import jax.numpy as jnp
import numpy as np
from jax import lax
from jax.experimental import pallas as pl
from jax.experimental.pallas import tpu as pltpu

F32 = jnp.float32
BF16 = jnp.bfloat16

D_MODEL = 2048
BRANCH_W = 512
HEAD = 64
RWKV_HEADS = 8
NSA_KV_HEADS = 2
NSA_GROUP = 4
COMP_L = 32
COMP_STRIDE = 16
SEL_L = 64
SEL_N = 16
WINDOW = 512
MEM_HEADS = 4
MEM_HEAD = 128
POOL_WINDOWS = (2, 4, 8, 16)
N_EXPERTS = 8
NORM_EPS = 1e-6
RWKV_GN_EPS = 64e-5
RWKV_CHUNK = 64

VMEM_LIMIT_BYTES = 56 * 1024 * 1024

C_RWKV, C_NSAQ, C_CONV, C_POOL, C_MEM = 0, 2048, 2560, 4096, 4608
C_KC, C_VC, C_KS, C_VS, C_KW, C_VW, C_GL, C_VRES = 5120, 5248, 5376, 5504, 5632, 5760, 5888, 6016
N_SMALL = 6144


def _params(n_axes):
    return pltpu.CompilerParams(dimension_semantics=("arbitrary",) * n_axes,
                                vmem_limit_bytes=VMEM_LIMIT_BYTES)


def _split(a, n):
    pieces, r = [], a
    for i in range(n):
        p = r.astype(BF16)
        pieces.append(p)
        if i + 1 < n:
            r = r - p.astype(F32)
    return pieces


def _dg(a, b, nt):
    dims = (((1,), (1,)), ((), ())) if nt else (((1,), (0,)), ((), ()))
    return lax.dot_general(a, b, dims, preferred_element_type=F32)


def _mm(a, b, pa=1, pb=1, nt=False):
    sa = _split(a, pa) if a.dtype != BF16 else [a]
    sb = _split(b, pb) if b.dtype != BF16 else [b]
    order = max(len(sa), len(sb))
    acc = None
    for i, x in enumerate(sa):
        for j, y in enumerate(sb):
            if i + j < order:
                t = _dg(x, y, nt)
                acc = t if acc is None else acc + t
    return acc


def _head_sum_matrix(width, head):
    r = lax.broadcasted_iota(jnp.int32, (width, width), 0) // head
    c = lax.broadcasted_iota(jnp.int32, (width, width), 1) // head
    return jnp.where(r == c, 1.0, 0.0).astype(BF16)


def _group_mean(x, head):
    hs = _head_sum_matrix(x.shape[-1], head)
    return _mm(x, hs, pa=3) * (1.0 / head)


def _rmsnorm_kernel(x_ref, g_ref, o_ref):
    x = x_ref[...]
    y = x * lax.rsqrt(jnp.mean(x * x, axis=-1, keepdims=True) + NORM_EPS)
    o_ref[...] = (y * g_ref[...]).astype(o_ref.dtype)


def _rmsnorm(x, g, tm=512):
    m, d = x.shape
    tm = min(tm, m)
    return pl.pallas_call(
        _rmsnorm_kernel,
        grid=(m // tm,),
        in_specs=[pl.BlockSpec((tm, d), lambda i: (i, 0)), pl.BlockSpec((1, d), lambda i: (0, 0))],
        out_specs=pl.BlockSpec((tm, d), lambda i: (i, 0)),
        out_shape=jax.ShapeDtypeStruct((m, d), BF16),
        compiler_params=_params(1),
        name="rmsnorm",
    )(x, g.reshape(1, d))


def _rmsnorm_router_kernel(x_ref, g_ref, wr_ref, o_ref, comb_ref, assign_ref, count_ref, carry_ref):
    tm = x_ref.shape[0]

    @pl.when(pl.program_id(0) == 0)
    def _():
        carry_ref[...] = jnp.zeros_like(carry_ref)

    x = x_ref[...]
    y = x * lax.rsqrt(jnp.mean(x * x, axis=-1, keepdims=True) + NORM_EPS) * g_ref[...]
    o_ref[...] = y
    logits = _mm(y, wr_ref[...], pa=3, pb=3)
    lane = lax.broadcasted_iota(jnp.int32, logits.shape, 1)
    neg = jnp.float32(-3e38)
    lg = jnp.where(lane < N_EXPERTS, logits, neg)
    m1 = jnp.max(lg, axis=-1, keepdims=True)
    i1 = jnp.min(jnp.where(lg == m1, lane, 1 << 20), axis=-1, keepdims=True)
    lg2 = jnp.where(lane == i1, neg, lg)
    m2 = jnp.max(lg2, axis=-1, keepdims=True)
    i2 = jnp.min(jnp.where(lg2 == m2, lane, 1 << 20), axis=-1, keepdims=True)
    e2 = jnp.exp(m2 - m1)
    w1 = 1.0 / (1.0 + e2)
    w2 = e2 / (1.0 + e2)
    comb_ref[...] = jnp.where(lane == i1, w1, 0.0) + jnp.where(lane == i2, w2, 0.0)
    assign = jnp.where((lane == i1) | (lane == i2), 1.0, 0.0)
    assign_ref[...] = assign
    row = lax.broadcasted_iota(jnp.int32, (tm, tm), 0)
    col = lax.broadcasted_iota(jnp.int32, (tm, tm), 1)
    tril = jnp.where(row >= col, 1.0, 0.0).astype(BF16)
    count = jnp.dot(tril, assign.astype(BF16), preferred_element_type=F32) + carry_ref[0:1, :]
    count_ref[...] = count
    carry_ref[0:1, :] = count[tm - 1:tm, :]


def _rmsnorm_router(x, g, router, tm=512):
    m, d = x.shape
    tm = min(tm, m)
    wr = jnp.pad(router, ((0, 0), (0, 128 - router.shape[1])))
    lanes = pl.BlockSpec((tm, 128), lambda i: (i, 0))
    return pl.pallas_call(
        _rmsnorm_router_kernel,
        grid=(m // tm,),
        in_specs=[pl.BlockSpec((tm, d), lambda i: (i, 0)), pl.BlockSpec((1, d), lambda i: (0, 0)),
                  pl.BlockSpec((d, 128), lambda i: (0, 0))],
        out_specs=[pl.BlockSpec((tm, d), lambda i: (i, 0)), lanes, lanes, lanes],
        out_shape=[jax.ShapeDtypeStruct((m, d), F32)] + [jax.ShapeDtypeStruct((m, 128), F32)] * 3,
        scratch_shapes=[pltpu.VMEM((8, 128), F32)],
        compiler_params=_params(1),
        name="rmsnorm_router",
    )(x, g.reshape(1, d), wr)


MOE_TILE = 256


def _row_copy(src_hbm, row, dst, slot, sem):
    return pltpu.make_async_copy(src_hbm.at[pl.ds(row, 1), :], dst.at[pl.ds(slot, 1), :], sem)


ROW_DMA_THREADS = 2


def _gather_rows_kernel(idx_ref, nxt_ref, src_hbm, o_ref, buf_ref, sem):
    tm = o_ref.shape[0]

    i = pl.program_id(0)
    slot = i % 2

    def issue(ids_ref, to_slot):
        def body(q, carry):
            for t in range(ROW_DMA_THREADS):
                r = q * ROW_DMA_THREADS + t
                _row_copy(src_hbm, ids_ref[0, 0, r], buf_ref.at[to_slot], r, sem.at[to_slot]).start(priority=t)
            return carry
        lax.fori_loop(0, tm // ROW_DMA_THREADS, body, 0)

    @pl.when(i == 0)
    def _():
        issue(idx_ref, 0)

    @pl.when(i + 1 < pl.num_programs(0))
    def _():
        issue(nxt_ref, 1 - slot)

    def wait(r, carry):
        _row_copy(src_hbm, 0, buf_ref.at[slot], r, sem.at[slot]).wait()
        return carry

    lax.fori_loop(0, tm, wait, 0)
    o_ref[...] = buf_ref[slot].astype(o_ref.dtype)


def _gather_rows(src, idx, tm=MOE_TILE):
    r = idx.shape[0]
    d = src.shape[1]
    n_tiles = r // tm
    ids = idx.reshape(n_tiles, 1, tm)
    return pl.pallas_call(
        _gather_rows_kernel,
        grid=(n_tiles,),
        in_specs=[pl.BlockSpec((1, 1, tm), lambda i: (i, 0, 0), memory_space=pltpu.SMEM),
                  pl.BlockSpec((1, 1, tm), lambda i: (jnp.minimum(i + 1, n_tiles - 1), 0, 0),
                               memory_space=pltpu.SMEM),
                  pl.BlockSpec(memory_space=pl.ANY)],
        out_specs=pl.BlockSpec((tm, d), lambda i: (i, 0)),
        out_shape=jax.ShapeDtypeStruct((r, d), BF16),
        scratch_shapes=[pltpu.VMEM((2, tm, d), F32), pltpu.SemaphoreType.DMA((2,))],
        compiler_params=_params(1),
        name="moe_gather",
    )(ids, ids, src)


def _expert_changed(te_ref, i):
    return (i == 0) | (te_ref[i] != te_ref[jnp.maximum(i - 1, 0)])


def _moe_up_kernel(te_ref, na_ref, x_ref, w1_ref, w3_ref, o_ref, w1c_ref, w3c_ref):
    i = pl.program_id(1)
    active = i < na_ref[0]

    @pl.when(active & _expert_changed(te_ref, i))
    def _():
        w1c_ref[...] = w1_ref[0].astype(BF16)
        w3c_ref[...] = w3_ref[0].astype(BF16)

    @pl.when(active)
    def _():
        x = x_ref[...]
        a = jnp.dot(x, w1c_ref[...], preferred_element_type=F32)
        b = jnp.dot(x, w3c_ref[...], preferred_element_type=F32)
        o_ref[...] = (a * jax.nn.sigmoid(a) * b).astype(o_ref.dtype)

    @pl.when(jnp.logical_not(active))
    def _():
        o_ref[...] = jnp.zeros_like(o_ref)


def _moe_down_kernel(te_ref, na_ref, a_ref, w_ref, o_ref, wc_ref):
    i = pl.program_id(1)
    active = i < na_ref[0]

    @pl.when(active & _expert_changed(te_ref, i))
    def _():
        wc_ref[...] = w_ref[0].astype(BF16)

    @pl.when(active)
    def _():
        o_ref[...] = jnp.dot(a_ref[...], wc_ref[...], preferred_element_type=F32)

    @pl.when(jnp.logical_not(active))
    def _():
        o_ref[...] = jnp.zeros_like(o_ref)


def _moe_experts(x_sorted, tile_expert, n_active, w1, w3, w2, tm=MOE_TILE):
    rows, d = x_sorted.shape
    ff = w1.shape[2]
    n_tiles = rows // tm
    tn_up, tn_down = ff // 2, d // 2
    once = pl.Buffered(1)
    act = pl.pallas_call(
        _moe_up_kernel,
        grid_spec=pltpu.PrefetchScalarGridSpec(
            num_scalar_prefetch=2,
            grid=(ff // tn_up, n_tiles),
            in_specs=[pl.BlockSpec((tm, d), lambda j, i, te, na: (i, 0)),
                      pl.BlockSpec((1, d, tn_up), lambda j, i, te, na: (te[i], 0, j)),
                      pl.BlockSpec((1, d, tn_up), lambda j, i, te, na: (te[i], 0, j), pipeline_mode=once)],
            out_specs=pl.BlockSpec((tm, tn_up), lambda j, i, te, na: (i, j)),
            scratch_shapes=[pltpu.VMEM((d, tn_up), BF16)] * 2),
        out_shape=jax.ShapeDtypeStruct((rows, ff), BF16),
        compiler_params=_params(2),
        name="moe_up",
    )(tile_expert, n_active, x_sorted, w1, w3)
    return pl.pallas_call(
        _moe_down_kernel,
        grid_spec=pltpu.PrefetchScalarGridSpec(
            num_scalar_prefetch=2,
            grid=(d // tn_down, n_tiles),
            in_specs=[pl.BlockSpec((tm, ff), lambda j, i, te, na: (i, 0)),
                      pl.BlockSpec((1, ff, tn_down), lambda j, i, te, na: (te[i], 0, j))],
            out_specs=pl.BlockSpec((tm, tn_down), lambda j, i, te, na: (i, j)),
            scratch_shapes=[pltpu.VMEM((ff, tn_down), BF16)]),
        out_shape=jax.ShapeDtypeStruct((rows, d), F32),
        compiler_params=_params(2),
        name="moe_down",
    )(tile_expert, n_active, act, w2)


def _moe_combine_kernel(pa_ref, pb_ref, na_ref, nb_ref, y_hbm, x_ref, w_ref, o_ref, buf_ref, sem):
    tm = x_ref.shape[0]
    i = pl.program_id(0)
    slot = i % 2

    def issue(a_ref, b_ref, to_slot):
        def body(r, carry):
            _row_copy(y_hbm, a_ref[0, 0, r], buf_ref.at[to_slot, 0], r, sem.at[to_slot]).start(priority=0)
            _row_copy(y_hbm, b_ref[0, 0, r], buf_ref.at[to_slot, 1], r, sem.at[to_slot]).start(priority=1)
            return carry
        lax.fori_loop(0, tm, body, 0)

    @pl.when(i == 0)
    def _():
        issue(pa_ref, pb_ref, 0)

    @pl.when(i + 1 < pl.num_programs(0))
    def _():
        issue(na_ref, nb_ref, 1 - slot)

    def wait(r, carry):
        _row_copy(y_hbm, 0, buf_ref.at[slot, 0], r, sem.at[slot]).wait()
        _row_copy(y_hbm, 0, buf_ref.at[slot, 1], r, sem.at[slot]).wait()
        return carry

    lax.fori_loop(0, tm, wait, 0)
    w = w_ref[...]
    o_ref[...] = x_ref[...] + w[:, 0:1] * buf_ref[slot, 0] + w[:, 1:2] * buf_ref[slot, 1]


def _moe_combine(x, y_sorted, pos_a, pos_b, weights, tm=MOE_TILE):
    n, d = x.shape
    n_tiles = n // tm
    cur = pl.BlockSpec((1, 1, tm), lambda i: (i, 0, 0), memory_space=pltpu.SMEM)
    nxt = pl.BlockSpec((1, 1, tm), lambda i: (jnp.minimum(i + 1, n_tiles - 1), 0, 0), memory_space=pltpu.SMEM)
    pa, pb = pos_a.reshape(n_tiles, 1, tm), pos_b.reshape(n_tiles, 1, tm)
    return pl.pallas_call(
        _moe_combine_kernel,
        grid=(n_tiles,),
        in_specs=[cur, cur, nxt, nxt, pl.BlockSpec(memory_space=pl.ANY),
                  pl.BlockSpec((tm, d), lambda i: (i, 0)), pl.BlockSpec((tm, 128), lambda i: (i, 0))],
        out_specs=pl.BlockSpec((tm, d), lambda i: (i, 0)),
        out_shape=jax.ShapeDtypeStruct((n, d), F32),
        scratch_shapes=[pltpu.VMEM((2, 2, tm, d), F32), pltpu.SemaphoreType.DMA((2,))],
        compiler_params=_params(1),
        name="moe_combine",
    )(pa, pb, pa, pb, y_sorted, x, weights)


def _moe(x, norm_g, router, w1, w3, w2):
    n, d = x.shape
    tm = min(MOE_TILE, n)
    h2, comb, assign, count = _rmsnorm_router(x, norm_g, router)
    assigned = assign[:, :N_EXPERTS] > 0.5
    count = count[:, :N_EXPERTS].astype(jnp.int32)
    total = count[-1]
    padded = (total + tm - 1) // tm * tm
    ends = jnp.cumsum(padded)
    starts = ends - padded
    dest = starts[None, :] + count - 1
    rows = (2 * n // tm + N_EXPERTS) * tm
    tile_start = jnp.arange(rows // tm, dtype=jnp.int32) * tm
    tile_expert = jnp.minimum(jnp.sum(tile_start[:, None] >= ends[None, :], axis=1), N_EXPERTS - 1).astype(jnp.int32)
    n_active = (ends[-1:] // tm).astype(jnp.int32)
    offset = tile_start[:, None] + jnp.arange(tm, dtype=jnp.int32)[None, :] - starts[tile_expert][:, None]
    src = jnp.sum(count.T[tile_expert][:, None, :] <= offset[:, :, None], axis=-1, dtype=jnp.int32)
    src = jnp.minimum(src, n - 1).reshape(rows)
    first = jnp.argmax(assigned, axis=1)
    second = N_EXPERTS - 1 - jnp.argmax(assigned[:, ::-1], axis=1)
    take = lambda a, i: jnp.take_along_axis(a, i[:, None], axis=1)[:, 0]
    pos_a, pos_b = take(dest, first), take(dest, second)
    weights = jnp.pad(jnp.stack([take(comb, first), take(comb, second)], axis=1), ((0, 0), (0, 126)))

    x_sorted = _gather_rows(h2, src, tm)
    y_sorted = _moe_experts(x_sorted, tile_expert, n_active, w1, w3, w2, tm)
    return _moe_combine(x, y_sorted, pos_a, pos_b, weights, tm)


def _matmul_kernel(*refs, has_res, cached):
    a_ref, w_ref = refs[0], refs[1]
    res_ref = refs[2] if has_res else None
    o_ref = refs[3] if has_res else refs[2]
    if cached:
        wc_ref = refs[-1]

        @pl.when(pl.program_id(1) == 0)
        def _():
            wc_ref[...] = w_ref[...].astype(BF16)

        w = wc_ref[...]
    else:
        w = w_ref[...]
    out = jnp.dot(a_ref[...], w, preferred_element_type=F32)
    if has_res:
        out = out + res_ref[...]
    o_ref[...] = out.astype(o_ref.dtype)


def _matmul(a, w, *, tm, tn, out_dtype=F32, res=None, name="matmul"):
    m, kdim = a.shape
    n = w.shape[1]
    tm, tn = min(tm, m), min(tn, n)
    assert m % tm == 0 and n % tn == 0
    cached = w.dtype != BF16
    in_specs = [pl.BlockSpec((tm, kdim), lambda j, i: (i, 0)), pl.BlockSpec((kdim, tn), lambda j, i: (0, j))]
    args = [a, w]
    if res is not None:
        in_specs.append(pl.BlockSpec((tm, tn), lambda j, i: (i, j)))
        args.append(res)
    return pl.pallas_call(
        functools.partial(_matmul_kernel, has_res=res is not None, cached=cached),
        grid=(n // tn, m // tm),
        in_specs=in_specs,
        out_specs=pl.BlockSpec((tm, tn), lambda j, i: (i, j)),
        out_shape=jax.ShapeDtypeStruct((m, n), out_dtype),
        scratch_shapes=[pltpu.VMEM((kdim, tn), BF16)] if cached else [],
        compiler_params=_params(2),
        name=name,
    )(*args)


def _swiglu_up_kernel(h_ref, w1_ref, w3_ref, o_ref):
    h = h_ref[...]
    a = jnp.dot(h, w1_ref[...].astype(BF16), preferred_element_type=F32)
    b = jnp.dot(h, w3_ref[...].astype(BF16), preferred_element_type=F32)
    o_ref[...] = (a * jax.nn.sigmoid(a) * b).astype(o_ref.dtype)


def _swiglu_up(h, w1, w3, tm=1024, tn=512):
    m, kdim = h.shape
    n = w1.shape[1]
    tm, tn = min(tm, m), min(tn, n)
    assert m % tm == 0 and n % tn == 0
    return pl.pallas_call(
        _swiglu_up_kernel,
        grid=(n // tn, m // tm),
        in_specs=[pl.BlockSpec((tm, kdim), lambda j, i: (i, 0)),
                  pl.BlockSpec((kdim, tn), lambda j, i: (0, j)),
                  pl.BlockSpec((kdim, tn), lambda j, i: (0, j))],
        out_specs=pl.BlockSpec((tm, tn), lambda j, i: (i, j)),
        out_shape=jax.ShapeDtypeStruct((m, n), BF16),
        compiler_params=_params(2),
        name="swiglu_up",
    )(h, w1, w3)


def _shift_rows(x, carry_row):
    rolled = pltpu.roll(x, 1, axis=0)
    row = lax.broadcasted_iota(jnp.int32, x.shape, 0)
    return jnp.where(row == 0, carry_row, rolled)


def _rwkv_prep_kernel(*refs, has_vres):
    (u_ref, mu_ref, w0_ref, w2_ref, a0_ref, a2_ref, g2_ref, kkw_ref, kaw_ref) = refs[:9]
    pos = 9
    if has_vres:
        vd_ref, vmu_ref, v0_ref, vup_ref, vfirst_ref = refs[pos:pos + 5]
        pos += 5
    r_ref, lw_ref, k_ref, v_ref, kk_ref, b_ref, g_ref = refs[pos:pos + 7]
    pos += 7
    cu_ref = refs[pos]
    cv_ref = refs[pos + 1] if has_vres else None
    tt = u_ref.shape[0]

    @pl.when(pl.program_id(1) == 0)
    def _():
        cu_ref[...] = jnp.zeros_like(cu_ref)
        if has_vres:
            cv_ref[...] = jnp.zeros_like(cv_ref)

    u = u_ref[...]
    prev = _shift_rows(u, cu_ref[0:1, :])
    cu_ref[0:1, :] = u[tt - 1:tt, :]
    uf = u + (prev - u) * mu_ref[...]
    r, k, v = uf[:, 0:512], uf[:, 512:1024], uf[:, 1024:1536]
    wd, ad, gd = uf[:, 1536:1664], uf[:, 1664:1792], uf[:, 1792:2048]

    x = w0_ref[...] + _mm(jnp.tanh(wd), w2_ref[...], 2, 2)
    softplus = jnp.maximum(-x, 0.0) + jnp.log(1.0 + jnp.exp(-jnp.abs(x)))
    lw_ref[...] = -jnp.exp(-softplus - 0.5)
    a = jax.nn.sigmoid(a0_ref[...] + _mm(ad, a2_ref[...], 2, 2))
    g_ref[...] = _mm(jax.nn.sigmoid(gd), g2_ref[...], 2, 2)
    if has_vres:
        vd = vd_ref[...]
        vprev = _shift_rows(vd, cv_ref[0:1, :])
        cv_ref[0:1, :] = vd[tt - 1:tt, :]
        vdf = vd + (vprev - vd) * vmu_ref[...]
        v = v + (vfirst_ref[...] - v) * jax.nn.sigmoid(v0_ref[...] + _mm(vdf, vup_ref[...], 2, 2))
    kk = k * kkw_ref[...]
    ss = _group_mean(kk * kk, HEAD) * HEAD
    kk = kk * lax.rsqrt(jnp.maximum(ss, 1e-24))
    r_ref[...] = r
    k_ref[...] = k * (1.0 + (a - 1.0) * kaw_ref[...])
    v_ref[...] = v
    kk_ref[...] = kk
    b_ref[...] = kk * a


def _rwkv_prep(p_small, bsz, seq, mu, w0, w2, a0, a2, g2, kkw, kaw, vres, tt=256):
    tt = min(tt, seq)
    nt = seq // tt
    n = bsz * seq
    row = lambda b, t: (b * nt + t, 0)
    const = lambda b, t: (0, 0)
    pad_rows = lambda w: jnp.pad(w, ((0, 128 - w.shape[0]), (0, 0)))
    mu_p = jnp.concatenate([mu[:1536], jnp.pad(mu[1536:1632], (0, 32)), jnp.pad(mu[1632:1728], (0, 32)), mu[1728:]])
    vec = lambda a: a.reshape(1, -1)
    args = [p_small, vec(mu_p), vec(w0), pad_rows(w2), vec(a0), pad_rows(a2), g2, vec(kkw), vec(kaw)]
    in_specs = [pl.BlockSpec((tt, 2048), row), pl.BlockSpec((1, 2048), const), pl.BlockSpec((1, 512), const),
                pl.BlockSpec((128, 512), const), pl.BlockSpec((1, 512), const), pl.BlockSpec((128, 512), const),
                pl.BlockSpec((256, 512), const), pl.BlockSpec((1, 512), const), pl.BlockSpec((1, 512), const)]
    scratch = [pltpu.VMEM((8, 2048), F32)]
    if vres is not None:
        vmu, v0, vup, vfirst = vres
        args += [p_small, vec(jnp.pad(vmu, (0, 64))), vec(v0), pad_rows(vup), vfirst]
        in_specs += [pl.BlockSpec((tt, 128), lambda b, t: (b * nt + t, C_VRES // 128)), pl.BlockSpec((1, 128), const),
                     pl.BlockSpec((1, 512), const), pl.BlockSpec((128, 512), const), pl.BlockSpec((tt, 512), row)]
        scratch.append(pltpu.VMEM((8, 128), F32))
    return pl.pallas_call(
        functools.partial(_rwkv_prep_kernel, has_vres=vres is not None),
        grid=(bsz, nt),
        in_specs=in_specs,
        out_specs=[pl.BlockSpec((tt, 512), row)] * 7,
        out_shape=[jax.ShapeDtypeStruct((n, 512), F32)] * 7,
        scratch_shapes=scratch,
        compiler_params=_params(2),
        name="rwkv_prep",
    )(*args)


def _rwkv_chunk_kernel(r_ref, lw_ref, k_ref, v_ref, kk_ref, b_ref, g_ref, rk_ref, lnw_ref, lnb_ref, o_ref, s_ref,
                       *, pw):
    nb, c, _ = r_ref.shape

    @pl.when(pl.program_id(0) == 0)
    def _():
        s_ref[...] = jnp.zeros_like(s_ref)

    row = lax.broadcasted_iota(jnp.int32, (c, c), 0)
    col = lax.broadcasted_iota(jnp.int32, (c, c), 1)
    tril_incl = jnp.where(row >= col, 1.0, 0.0).astype(BF16)
    strict = row > col
    incl = row >= col
    eye = jnp.where(row == col, 1.0, 0.0)
    hrow = lax.broadcasted_iota(jnp.int32, (HEAD, HEAD), 0)
    hcol = lax.broadcasted_iota(jnp.int32, (HEAD, HEAD), 1)
    eye_h = jnp.where(hrow == hcol, 1.0, 0.0).astype(BF16)
    mm = functools.partial(_mm, pa=pw, pb=pw)
    transpose = lambda a, pieces: _mm(eye_h, a, pb=pieces, nt=True)

    heads = [(bi, h) for bi in range(nb) for h in range(RWKV_HEADS)]
    per_head = lambda full: [full[bi][:, h * HEAD:(h + 1) * HEAD] for bi, h in heads]
    each = lambda f, *lists: [f(*vals) for vals in zip(*lists)]

    r, lw, k, v, kk, b = ([ref[bi] for bi in range(nb)] for ref in (r_ref, lw_ref, k_ref, v_ref, kk_ref, b_ref))
    cum = [_mm(tril_incl, x, pb=3) for x in lw]
    g_inv = [jnp.exp(-x) for x in cum]
    a_t = per_head(each(lambda kk_, c_, lw_: kk_ * jnp.exp(c_ - lw_), kk, cum, lw))
    b_t = per_head(each(lambda b_, gi: b_ * gi, b, g_inv))
    k_t = per_head(each(lambda k_, gi: k_ * gi, k, g_inv))
    r_t = per_head(each(lambda r_, c_: r_ * jnp.exp(c_), r, cum))
    vh = per_head(v)
    tail = [jnp.exp(x[c - 1:c, :] - x) for x in cum]
    bh = per_head(each(lambda b_, t_: b_ * t_, b, tail))
    kh = per_head(each(lambda k_, t_: k_ * t_, k, tail))
    g_last = per_head([jnp.exp(x[c - 1:c, :]) for x in cum])

    a_ab = each(lambda x, y: jnp.where(strict, mm(x, y, nt=True), 0.0), a_t, b_t)
    a_ak = each(lambda x, y: jnp.where(strict, mm(x, y, nt=True), 0.0), a_t, k_t)
    a_rb = each(lambda x, y: jnp.where(incl, mm(x, y, nt=True), 0.0), r_t, b_t)
    a_rk = each(lambda x, y: jnp.where(incl, mm(x, y, nt=True), 0.0), r_t, k_t)
    x = [eye - n for n in a_ab]
    p = each(mm, a_ab, a_ab)
    steps = int(np.log2(c)) - 1
    for i in range(steps):
        x = each(lambda x_, p_: x_ + mm(x_, p_), x, p)
        if i + 1 < steps:
            p = each(mm, p, p)
    w1 = each(mm, x, a_t)
    akv = each(mm, a_ak, vh)
    w2 = each(mm, x, akv)
    y_in = each(lambda ark, v_, arb, w2_: mm(ark, v_) - mm(arb, w2_), a_rk, vh, a_rb, w2)
    q_h = each(lambda r_, arb, w1_: r_ - mm(arb, w1_), r_t, a_rb, w1)
    bh_t = [transpose(x_, pw) for x_ in bh]
    kh_t = [transpose(x_, pw) for x_ in kh]
    decay = [transpose(jnp.broadcast_to(x_, (HEAD, HEAD)), 3) for x_ in g_last]
    g_s = each(lambda kt_, v_, bt_, w2_: mm(kt_, v_) - mm(bt_, w2_), kh_t, vh, bh_t, w2)
    s = [s_ref[i] for i in range(len(heads))]
    ys = each(lambda yi, q_, s_: yi + mm(q_, s_), y_in, q_h, s)
    w1s = each(mm, w1, s)
    for i, (d_, s_, bt_, ws_, gs_) in enumerate(zip(decay, s, bh_t, w1s, g_s)):
        s_ref[i] = d_ * s_ - mm(bt_, ws_) + gs_

    centred = [y_ - jnp.mean(y_, axis=-1, keepdims=True) for y_ in ys]
    normed = [d_ * lax.rsqrt(jnp.mean(d_ * d_, axis=-1, keepdims=True) + RWKV_GN_EPS) for d_ in centred]
    rkw = per_head([r[bi] * k[bi] * rk_ref[...] for bi in range(nb)])
    bonus = each(lambda w_, v_: jnp.sum(w_, axis=-1, keepdims=True) * v_, rkw, vh)
    for bi in range(nb):
        sl = slice(bi * RWKV_HEADS, (bi + 1) * RWKV_HEADS)
        yn = jnp.concatenate(normed[sl], axis=1) * lnw_ref[...] + lnb_ref[...]
        o_ref[bi] = ((yn + jnp.concatenate(bonus[sl], axis=1)) * g_ref[bi]).astype(o_ref.dtype)


def _rwkv_recurrence(r, lw, k, v, kk, b, g, rk, lnw, lnb, bsz, seq, pw=1):
    c = min(RWKV_CHUNK, seq)
    blk = pl.BlockSpec((bsz, c, BRANCH_W), lambda t: (0, t, 0))
    cst = pl.BlockSpec((1, BRANCH_W), lambda t: (0, 0))
    as3 = lambda a: a.reshape(bsz, seq, BRANCH_W)
    out = pl.pallas_call(
        functools.partial(_rwkv_chunk_kernel, pw=pw),
        grid=(seq // c,),
        in_specs=[blk] * 7 + [cst] * 3,
        out_specs=blk,
        out_shape=jax.ShapeDtypeStruct((bsz, seq, BRANCH_W), BF16),
        scratch_shapes=[pltpu.VMEM((bsz * RWKV_HEADS, HEAD, HEAD), F32)],
        compiler_params=_params(1),
        name="rwkv_chunk",
    )(as3(r), as3(lw), as3(k), as3(v), as3(kk), as3(b), as3(g), rk.reshape(1, -1), lnw.reshape(1, -1),
      lnb.reshape(1, -1))
    return out.reshape(bsz * seq, BRANCH_W)


def _eye(n):
    r = lax.broadcasted_iota(jnp.int32, (n, n), 0)
    c = lax.broadcasted_iota(jnp.int32, (n, n), 1)
    return jnp.where(r == c, 1.0, 0.0).astype(BF16)


def _transpose_bf16(x):
    return _dg(_eye(x.shape[1]), x.astype(BF16), nt=True).astype(BF16)


def _nsa_norm_kernel(q_ref, ks_ref, kw_ref, vs_ref, vw_ref, gq_ref, gs_ref, gw_ref,
                     qo_ref, kso_ref, kwo_ref, vso_ref, vwo_ref):
    for x_ref, g_ref, o_ref in ((q_ref, gq_ref, qo_ref), (ks_ref, gs_ref, kso_ref), (kw_ref, gw_ref, kwo_ref)):
        x = x_ref[...]
        ms = _group_mean(x * x, HEAD)
        o_ref[...] = (x * lax.rsqrt(ms + NORM_EPS) * g_ref[...]).astype(o_ref.dtype)
    vso_ref[...] = _transpose_bf16(vs_ref[...])
    vwo_ref[...] = _transpose_bf16(vw_ref[...])


def _nsa_norm(p_small, qk_gain, bsz, seq, tt=512):
    n = p_small.shape[0]
    tt = min(tt, seq)
    nt = seq // tt
    col = lambda c, w: pl.BlockSpec((tt, w), lambda i: (i, c // w))
    cst = lambda w: pl.BlockSpec((1, w), lambda i: (0, 0))
    out = lambda w: pl.BlockSpec((tt, w), lambda i: (i, 0))
    out_t = pl.BlockSpec((128, tt), lambda i: (i // nt, i % nt))
    gq = (jnp.tile(qk_gain[0], 8) * HEAD ** -0.5).reshape(1, 512)
    gs = jnp.tile(qk_gain[2], 2).reshape(1, 128)
    gw = jnp.tile(qk_gain[3], 2).reshape(1, 128)
    return pl.pallas_call(
        _nsa_norm_kernel,
        grid=(n // tt,),
        in_specs=[col(C_NSAQ, 512), col(C_KS, 128), col(C_KW, 128), col(C_VS, 128), col(C_VW, 128),
                  cst(512), cst(128), cst(128)],
        out_specs=[out(512), out(128), out(128), out_t, out_t],
        out_shape=[jax.ShapeDtypeStruct((n, 512), BF16)] + [jax.ShapeDtypeStruct((n, 128), BF16)] * 2
        + [jax.ShapeDtypeStruct((bsz * 128, seq), BF16)] * 2,
        compiler_params=_params(1),
        name="nsa_norm",
    )(p_small, p_small, p_small, p_small, p_small, gq, gs, gw)


def _gelu_tanh(x):
    return 0.5 * x * (1.0 + jnp.tanh(0.7978845608028654 * (x + 0.044715 * x * x * x)))


def _nsa_compress_kernel(x_ref, pos_ref, w1_ref, b1_ref, w2_ref, gain_ref, o_ref):
    half = x_ref.shape[-1]
    x = x_ref[0, 0, 0].astype(BF16)
    w1 = w1_ref[0].astype(BF16)
    z_lo = jnp.dot(x, w1[:half], preferred_element_type=F32)
    z_hi = jnp.dot(x, w1[half:], preferred_element_type=F32)
    nrow = z_hi.shape[0]
    pos = jnp.broadcast_to(pos_ref[0], (8, 2 * half))
    const = _mm(pos, w1, pa=2)[0:1] + b1_ref[0]
    pre = z_lo + pltpu.roll(z_hi, nrow - 1, axis=0) + const
    out = jnp.dot(_gelu_tanh(pre).astype(BF16), w2_ref[0].astype(BF16), preferred_element_type=F32)
    is_key = pl.program_id(0) == 0
    normed = out * lax.rsqrt(jnp.mean(out * out, axis=-1, keepdims=True) + NORM_EPS) * gain_ref[...]
    o_ref[0, 0, 0] = jnp.where(is_key, normed, out)


def _nsa_compress(kc_vc, cmp_pos, cmp_w1, cmp_b1, cmp_w2, gain):
    _, bsz, hkv, nc, wid = kc_vc.shape
    return pl.pallas_call(
        _nsa_compress_kernel,
        grid=(2, bsz, hkv),
        in_specs=[pl.BlockSpec((1, 1, 1, nc, wid), lambda i, b, h: (i, b, h, 0, 0)),
                  pl.BlockSpec((1, 1, 2 * wid), lambda i, b, h: (i, 0, 0)),
                  pl.BlockSpec((1, 2 * wid, 2 * HEAD), lambda i, b, h: (i, 0, 0)),
                  pl.BlockSpec((1, 1, 2 * HEAD), lambda i, b, h: (i, 0, 0)),
                  pl.BlockSpec((1, 2 * HEAD, HEAD), lambda i, b, h: (i, 0, 0)),
                  pl.BlockSpec((1, HEAD), lambda i, b, h: (0, 0))],
        out_specs=pl.BlockSpec((1, 1, 1, nc, HEAD), lambda i, b, h: (i, b, h, 0, 0)),
        out_shape=jax.ShapeDtypeStruct((2, bsz, hkv, nc, HEAD), F32),
        compiler_params=_params(3),
        name="nsa_compress",
    )(kc_vc, cmp_pos.reshape(2, 1, 2 * wid), cmp_w1, cmp_b1.reshape(2, 1, 2 * HEAD), cmp_w2, gain.reshape(1, HEAD))


def _nsa_cmp_kernel(q_ref, kc_ref, vc_ref, ov_ref, o_ref, sel_ref, *, tq):
    hkv, ncmp = kc_ref.shape[1], kc_ref.shape[2]
    nsel = ov_ref.shape[0]
    g = NSA_GROUP
    rows = g * tq
    start = pl.program_id(1) * tq
    q_all = q_ref[...]
    t_pos = start + (lax.broadcasted_iota(jnp.int32, (ncmp, rows), 1) & (tq - 1))
    c_end = lax.broadcasted_iota(jnp.int32, (ncmp, rows), 0) * COMP_STRIDE + (COMP_L - 1)
    mask = c_end <= t_pos
    sid = lax.broadcasted_iota(jnp.int32, (nsel, tq), 0)
    cur = (start + lax.broadcasted_iota(jnp.int32, (nsel, tq), 1)) // SEL_L
    forced = (sid == 0) | (sid == cur) | (sid == cur - 1)
    k_top = min(SEL_N, nsel)
    outs, sels = [], []
    for h in range(hkv):
        q = _group_rows(q_all, h, g)
        s = jnp.where(mask, _dg(kc_ref[0, h].astype(BF16), q, nt=True), -1e30)
        e = jnp.exp(s - jnp.max(s, axis=0, keepdims=True))
        p = jnp.where(mask, e / jnp.sum(e, axis=0, keepdims=True), 0.0)
        outs.append(jnp.dot(_transpose_bf16(vc_ref[0, h]), p.astype(BF16), preferred_element_type=F32))
        p_sum = p[:, 0:tq]
        for i in range(1, g):
            p_sum = p_sum + p[:, i * tq:(i + 1) * tq]
        imp = _mm(ov_ref[...], p_sum, pb=3)
        val = jnp.where(forced, 1e9, jnp.where(sid <= cur, imp, -1e9))
        rank = jnp.zeros(imp.shape, jnp.int32)
        for j in range(nsel):
            cj = val[j:j + 1, :]
            ahead = (cj > val) | ((cj == val) & (sid > j))
            rank = rank + ahead.astype(jnp.int32)
        sels.append(jnp.where((rank < k_top) & (val > -1e8), 1.0, 0.0))
    o_ref[...] = jnp.concatenate(outs, axis=0)
    sel_ref[...] = jnp.concatenate(sels, axis=0).astype(sel_ref.dtype)


def _group_rows(q_all, h, g):
    return jnp.concatenate([q_all[:, (h * g + i) * HEAD:(h * g + i + 1) * HEAD] for i in range(g)], axis=0)


NSA_Q_TILE = 256


def _nsa_cmp(q_n, k_cmp, v_cmp, overlap_t, bsz, seq, tq=NSA_Q_TILE):
    hkv = k_cmp.shape[1]
    nsel, ncmp = overlap_t.shape
    nt = seq // tq
    rows = NSA_GROUP * tq
    return pl.pallas_call(
        functools.partial(_nsa_cmp_kernel, tq=tq),
        grid=(bsz, nt),
        in_specs=[pl.BlockSpec((tq, BRANCH_W), lambda b, t: (b * nt + t, 0)),
                  pl.BlockSpec((1, hkv, ncmp, HEAD), lambda b, t: (b, 0, 0, 0)),
                  pl.BlockSpec((1, hkv, ncmp, HEAD), lambda b, t: (b, 0, 0, 0)),
                  pl.BlockSpec((nsel, ncmp), lambda b, t: (0, 0))],
        out_specs=[pl.BlockSpec((hkv * HEAD, rows), lambda b, t: (b * nt + t, 0)),
                   pl.BlockSpec((hkv * nsel, tq), lambda b, t: (b * nt + t, 0))],
        out_shape=[jax.ShapeDtypeStruct((bsz * nt * hkv * HEAD, rows), F32),
                   jax.ShapeDtypeStruct((bsz * nt * hkv * nsel, tq), BF16)],
        compiler_params=_params(2),
        name="nsa_cmp",
    )(q_n, k_cmp, v_cmp, overlap_t)


SCORE_MASKED = -1e30
SCORE_FLOOR = -1e20


def _nsa_attn_kernel(q_ref, sel_ref, ks_ref, vs_ref, kw_ref, vw_ref, oc_ref, gl_ref, o_ref, *, tq, kt):
    g = NSA_GROUP
    hkv = ks_ref.shape[1] // HEAD
    seq = ks_ref.shape[0]
    nsel = sel_ref.shape[0] // hkv
    rows = g * tq
    start = pl.program_id(1) * tq
    q_all = q_ref[...]
    sel_all = sel_ref[...]
    qs = [_group_rows(q_all, h, g) for h in range(hkv)]
    sels = [sel_all[h * nsel:(h + 1) * nsel, :] for h in range(hkv)]
    head_cols = lambda x, h: x[:, h * HEAD:(h + 1) * HEAD]
    per_query = lambda x: jnp.concatenate([x] * g, axis=1)
    with_ones = lambda vt: jnp.concatenate([vt, jnp.ones((16, vt.shape[1]), BF16)], axis=0)

    key_pos = lax.broadcasted_iota(jnp.int32, (kt, tq), 0)
    t_pos = start + lax.broadcasted_iota(jnp.int32, (kt, tq), 1)
    blk_of_key = lax.broadcasted_iota(jnp.int32, (kt, nsel), 0) // SEL_L
    blk_id = lax.broadcasted_iota(jnp.int32, (kt, nsel), 1)

    def key_tile(j, carry, causal):
        base = pl.multiple_of(j * kt, kt)
        kb2 = ks_ref[pl.ds(base, kt), :]
        vt2 = vs_ref[:, pl.ds(base, kt)]
        expand = jnp.where(blk_of_key + j * (kt // SEL_L) == blk_id, 1.0, 0.0).astype(BF16)
        out = []
        for h in range(hkv):
            m, acc = carry[h]
            keep = jnp.dot(expand, sels[h], preferred_element_type=F32) > 0.5
            if causal:
                keep = keep & (key_pos + base <= t_pos)
            bias = per_query(jnp.where(keep, 0.0, SCORE_MASKED))
            s = _dg(head_cols(kb2, h), qs[h], nt=True) + bias
            m_new = jnp.maximum(m, jnp.max(s, axis=0, keepdims=True))
            p = jnp.exp((s - m_new).astype(BF16))
            vt = with_ones(vt2[h * HEAD:(h + 1) * HEAD, :])
            acc = jnp.exp(m - m_new) * acc + jnp.dot(vt, p, preferred_element_type=F32)
            out.append((m_new, acc))
        return tuple(out)

    init = tuple((jnp.full((1, rows), SCORE_FLOOR, F32), jnp.zeros((HEAD + 16, rows), F32)) for _ in range(hkv))
    n_full = start // kt
    carry = lax.fori_loop(0, n_full, lambda j, c: key_tile(j, c, False), init)
    carry = key_tile(n_full, carry, True)
    o_sel = [acc[0:HEAD] / acc[HEAD:HEAD + 1] for _, acc in carry]

    span = min(WINDOW + tq, seq)
    wbase = pl.multiple_of(jnp.maximum(start - WINDOW, 0), tq) if seq > span else 0
    kb2 = kw_ref[pl.ds(wbase, span), :]
    vt2 = vw_ref[:, pl.ds(wbase, span)]
    tw = start + lax.broadcasted_iota(jnp.int32, (span, tq), 1)
    wpos = wbase + lax.broadcasted_iota(jnp.int32, (span, tq), 0)
    wbias = per_query(jnp.where((wpos <= tw) & (wpos > tw - WINDOW), 0.0, SCORE_MASKED))
    o_win = []
    for h in range(hkv):
        s = _dg(head_cols(kb2, h), qs[h], nt=True) + wbias
        p = jnp.exp((s - jnp.max(s, axis=0, keepdims=True)).astype(BF16))
        acc = jnp.dot(with_ones(vt2[h * HEAD:(h + 1) * HEAD, :]), p, preferred_element_type=F32)
        o_win.append(acc[0:HEAD] / acc[HEAD:HEAD + 1])

    gate_t = _mm(_eye(128), jax.nn.sigmoid(gl_ref[...]), pb=3, nt=True)
    o_cmp = oc_ref[...]
    blocks = []
    for h in range(hkv):
        for i in range(g):
            cols = slice(i * tq, (i + 1) * tq)
            r = (h * g + i) * 3
            blocks.append(gate_t[r:r + 1] * o_cmp[h * HEAD:(h + 1) * HEAD, cols]
                          + gate_t[r + 1:r + 2] * o_sel[h][:, cols] + gate_t[r + 2:r + 3] * o_win[h][:, cols])
    y_t = jnp.concatenate(blocks, axis=0)
    o_ref[...] = _dg(_eye(tq), y_t.astype(BF16), nt=True).astype(o_ref.dtype)


def _nsa_attn(p_small, q_n, sel_t, ks, vs_t, kw, vw_t, o_cmp_t, bsz, seq, tq=NSA_Q_TILE, kt=512):
    kt = min(kt, seq)
    nt = seq // tq
    hkv = NSA_KV_HEADS
    nsel = seq // SEL_L
    row = lambda w: pl.BlockSpec((tq, w), lambda b, t: (b * nt + t, 0))
    tile = lambda r, w: pl.BlockSpec((r, w), lambda b, t: (b * nt + t, 0))
    k_spec = pl.BlockSpec((seq, hkv * HEAD), lambda b, t: (b, 0))
    vt_spec = pl.BlockSpec((hkv * HEAD, seq), lambda b, t: (b, 0))
    return pl.pallas_call(
        functools.partial(_nsa_attn_kernel, tq=tq, kt=kt),
        grid=(bsz, nt),
        in_specs=[row(BRANCH_W), tile(hkv * nsel, tq), k_spec, vt_spec, k_spec, vt_spec,
                  tile(hkv * HEAD, NSA_GROUP * tq),
                  pl.BlockSpec((tq, 128), lambda b, t: (b * nt + t, C_GL // 128))],
        out_specs=row(BRANCH_W),
        out_shape=jax.ShapeDtypeStruct((bsz * seq, BRANCH_W), BF16),
        compiler_params=_params(2),
        name="nsa_attn",
    )(q_n, sel_t, ks, vs_t, kw, vw_t, o_cmp_t, p_small)


def _nsa(p_small, bsz, seq, qk_gain, cmp_pos, cmp_w1, cmp_b1, cmp_w2):
    hkv = NSA_KV_HEADS
    q_n, ks_n, kw_n, vs_t, vw_t = _nsa_norm(p_small, qk_gain, bsz, seq)
    ncmp = seq // COMP_STRIDE
    grouped = lambda c: p_small[:, c:c + 128].reshape(bsz, ncmp, COMP_STRIDE, hkv, HEAD).transpose(0, 3, 1, 2, 4)
    kc_vc = jnp.stack([grouped(C_KC), grouped(C_VC)]).reshape(2, bsz, hkv, ncmp, COMP_STRIDE * HEAD)
    cmp = _nsa_compress(kc_vc, cmp_pos, cmp_w1, cmp_b1, cmp_w2, qk_gain[1])
    nsel = seq // SEL_L
    c0 = np.arange(ncmp)[None, :] * COMP_STRIDE
    s0 = np.arange(nsel)[:, None] * SEL_L
    overlap_t = np.clip(np.minimum(c0 + COMP_L, s0 + SEL_L) - np.maximum(c0, s0), 0, None) / COMP_L
    o_cmp_t, sel_t = _nsa_cmp(q_n, cmp[0], cmp[1], jnp.asarray(overlap_t, BF16), bsz, seq)
    return _nsa_attn(p_small, q_n, sel_t, ks_n, vs_t, kw_n, vw_t, o_cmp_t, bsz, seq)


HALO = 16


def _conv_pool_kernel(bg_ref, cg_ref, xi_ref, pu_ref, cw_ref, conv_ref, pool_ref, cz_ref, cp_ref):
    tt = xi_ref.shape[0]
    ti = pl.program_id(1)

    @pl.when(ti == 0)
    def _():
        cz_ref[...] = jnp.zeros_like(cz_ref)
        cp_ref[...] = jnp.zeros_like(cp_ref)

    def history(x, carry_ref):
        ext = jnp.concatenate([carry_ref[...], x], axis=0)
        carry_ref[...] = x[tt - HALO:tt, :]
        return ext

    lag = lambda ext, s: pltpu.roll(ext, s, axis=0)
    body = lambda ext: ext[HALO:HALO + tt, :]

    z = cg_ref[...] * xi_ref[...]
    ze = history(z, cz_ref)
    cw = cw_ref[...]
    y = cw[2:3, :] * z + cw[1:2, :] * body(lag(ze, 1)) + cw[0:1, :] * body(lag(ze, 2))
    conv_ref[...] = (bg_ref[...] * y).astype(conv_ref.dtype)

    u = pu_ref[...]
    sums = [history(u, cp_ref)]
    for w in (1, 2, 4, 8):
        sums.append(sums[-1] + lag(sums[-1], w))
    count = (ti * tt + 1 + lax.broadcasted_iota(jnp.int32, (tt, 128), 0)).astype(F32)
    outs = []
    for gi, w in enumerate(POOL_WINDOWS):
        sl = slice(gi * 128, (gi + 1) * 128)
        outs.append(body(sums[gi + 1])[:, sl] / jnp.minimum(count, float(w)) - u[:, sl])
    pool_ref[...] = jnp.concatenate(outs, axis=1).astype(pool_ref.dtype)


def _conv_pool(p_small, bsz, seq, conv_w, tt=512):
    tt = min(tt, seq)
    nt = seq // tt
    n = bsz * seq
    col = lambda c: pl.BlockSpec((tt, 512), lambda b, t: (b * nt + t, c // 512))
    row = pl.BlockSpec((tt, 512), lambda b, t: (b * nt + t, 0))
    return pl.pallas_call(
        _conv_pool_kernel,
        grid=(bsz, nt),
        in_specs=[col(C_CONV), col(C_CONV + 512), col(C_CONV + 1024), col(C_POOL),
                  pl.BlockSpec((8, 512), lambda b, t: (0, 0))],
        out_specs=[row, row],
        out_shape=[jax.ShapeDtypeStruct((n, 512), BF16)] * 2,
        scratch_shapes=[pltpu.VMEM((HALO, 512), F32)] * 2,
        compiler_params=_params(2),
        name="conv_pool",
    )(p_small, p_small, p_small, p_small, jnp.pad(conv_w, ((0, 8 - conv_w.shape[0]), (0, 0))))


def _mem_attn_kernel(q_ref, kv_ref, gq_ref, gk_ref, o_ref, *, scale):
    outs = []
    for h in range(MEM_HEADS):
        sl = slice(h * MEM_HEAD, (h + 1) * MEM_HEAD)
        q = q_ref[:, sl]
        q = q * lax.rsqrt(jnp.mean(q * q, axis=-1, keepdims=True) + NORM_EPS) * gq_ref[...]
        k = kv_ref[0, :, sl]
        k = k * lax.rsqrt(jnp.mean(k * k, axis=-1, keepdims=True) + NORM_EPS) * gk_ref[...]
        v = kv_ref[0, :, BRANCH_W + h * MEM_HEAD:BRANCH_W + (h + 1) * MEM_HEAD]
        s = _dg(q.astype(BF16), k.astype(BF16), nt=True) * scale
        e = jnp.exp(s - jnp.max(s, axis=-1, keepdims=True))
        p = e / jnp.sum(e, axis=-1, keepdims=True)
        outs.append(jnp.dot(p.astype(BF16), v.astype(BF16), preferred_element_type=F32))
    o_ref[...] = jnp.concatenate(outs, axis=1).astype(o_ref.dtype)


def _mem_attn(p_small, kv, bsz, seq, qk_gain, tq=512):
    tq = min(tq, seq)
    nt = seq // tq
    mlen = kv.shape[1]
    return pl.pallas_call(
        functools.partial(_mem_attn_kernel, scale=MEM_HEAD ** -0.5),
        grid=(bsz, nt),
        in_specs=[pl.BlockSpec((tq, 512), lambda b, t: (b * nt + t, C_MEM // 512)),
                  pl.BlockSpec((1, mlen, 2 * BRANCH_W), lambda b, t: (b, 0, 0)),
                  pl.BlockSpec((1, MEM_HEAD), lambda b, t: (0, 0)),
                  pl.BlockSpec((1, MEM_HEAD), lambda b, t: (0, 0))],
        out_specs=pl.BlockSpec((tq, 512), lambda b, t: (b * nt + t, 0)),
        out_shape=jax.ShapeDtypeStruct((bsz * seq, 512), BF16),
        compiler_params=_params(2),
        name="mem_attn",
    )(p_small, kv, qk_gain[0].reshape(1, -1), qk_gain[1].reshape(1, -1))


def _merge_kernel(h_ref, *refs):
    wg_refs, y_refs = refs[0:5], refs[5:9]
    wb_ref, pooled_ref, wp_ref, ps_ref, o_ref, wgc_ref, wbc_ref = refs[9:]

    @pl.when(pl.program_id(1) == 0)
    def _():
        for i in range(5):
            wgc_ref[i] = wg_refs[i][...].astype(BF16)
        wbc_ref[...] = wb_ref[...].astype(BF16)

    h = h_ref[...]
    gate = lambda i: jax.nn.sigmoid(jnp.dot(h, wgc_ref[i], preferred_element_type=F32))
    z_pool = jnp.dot(pooled_ref[...], wp_ref[0].astype(BF16), preferred_element_type=F32) * ps_ref[...]
    acc = gate(4) * z_pool
    for i in range(4):
        acc = acc + gate(i) * jnp.dot(y_refs[i][...], wbc_ref[i], preferred_element_type=F32)
    o_ref[...] = acc.astype(o_ref.dtype)


def _merge(h, w_gate, ys, w_branch, pooled, pool_w, pool_scale, tm=512):
    n = h.shape[0]
    tm = min(tm, n)
    tn = 512
    nj = D_MODEL // tn
    once = pl.Buffered(1)
    gate_spec = lambda g: pl.BlockSpec((D_MODEL, tn), lambda j, i: (0, g * nj + j), pipeline_mode=once)
    y_spec = pl.BlockSpec((tm, BRANCH_W), lambda j, i: (i, 0))
    return pl.pallas_call(
        _merge_kernel,
        grid=(nj, n // tm),
        in_specs=[pl.BlockSpec((tm, D_MODEL), lambda j, i: (i, 0))] + [gate_spec(g) for g in range(5)]
        + [y_spec] * 4
        + [pl.BlockSpec((4, BRANCH_W, tn), lambda j, i: (0, 0, j), pipeline_mode=once),
           pl.BlockSpec((tm, 128), lambda j, i: (i, j)),
           pl.BlockSpec((1, 128, tn), lambda j, i: (j, 0, 0)),
           pl.BlockSpec((1, tn), lambda j, i: (0, j))],
        out_specs=pl.BlockSpec((tm, tn), lambda j, i: (i, j)),
        out_shape=jax.ShapeDtypeStruct((n, D_MODEL), BF16),
        scratch_shapes=[pltpu.VMEM((5, D_MODEL, tn), BF16), pltpu.VMEM((4, BRANCH_W, tn), BF16)],
        compiler_params=_params(2),
        name="merge",
    )(h, *([w_gate] * 5), *ys, w_branch, pooled, pool_w, pool_scale.reshape(1, -1))


def _pack_w_in(w, vres_w):
    pad = lambda a, width: jnp.pad(a, ((0, 0), (0, width - a.shape[1])))
    nsa = w[:, 1984:3288]
    vres_cols = pad(vres_w, 128) if vres_w is not None else jnp.zeros((D_MODEL, 128), w.dtype)
    small = jnp.concatenate([
        w[:, 0:1536], pad(w[:, 1536:1632], 128), pad(w[:, 1632:1728], 128), w[:, 1728:1984],
        nsa[:, 0:512], w[:, 3288:4824], w[:, 4824:5336], w[:, 5336:5848],
        nsa[:, 512:1280], pad(nsa[:, 1280:1304], 128), vres_cols], axis=1)
    return small, w[:, 5848:]


def kernel(x, mem, norm_mix, norm_ffn, norm_mem, w_in, rwkv_mu, rwkv_w0, rwkv_w2, rwkv_a0, rwkv_a2, rwkv_g2, rwkv_kk, rwkv_ka, rwkv_rk, rwkv_ln_w, rwkv_ln_b, vres_in, vres_mu, vres_v0, vres_up, nsa_qk_gain, nsa_cmp_pos, nsa_cmp_w1, nsa_cmp_b1, nsa_cmp_w2, conv_w, pool_w, pool_scale, mem_wkv, mem_qk_gain, w_branch, w_out, ffn_w1, ffn_w3, ffn_w2, moe_router, moe_w1, moe_w3, moe_w2):
    bsz, seq, d = x.shape
    n = bsz * seq
    depth = w_in.shape[0]
    mlen = mem.shape[1]
    xf = x.reshape(n, d)
    memf = mem.reshape(bsz * mlen, d)
    v_first = None
    for l in range(depth):
        h = _rmsnorm(xf, norm_mix[l])
        w_small, w_gate = _pack_w_in(w_in[l], vres_in[l - 1] if l > 0 else None)
        p_small = _matmul(h, w_small, tm=1024, tn=1024, name="in_proj")
        vres = (vres_mu[l - 1], vres_v0[l - 1], vres_up[l - 1], v_first) if l > 0 else None
        r, lw, k, v, kk, b, g = _rwkv_prep(p_small, bsz, seq, rwkv_mu[l], rwkv_w0[l], rwkv_w2[l], rwkv_a0[l],
                                           rwkv_a2[l], rwkv_g2[l], rwkv_kk[l], rwkv_ka[l], vres)
        if l == 0:
            v_first = v
        y_rwkv = _rwkv_recurrence(r, lw, k, v, kk, b, g, rwkv_rk[l], rwkv_ln_w[l], rwkv_ln_b[l], bsz, seq)
        y_nsa = _nsa(p_small, bsz, seq, nsa_qk_gain[l], nsa_cmp_pos[l], nsa_cmp_w1[l], nsa_cmp_b1[l], nsa_cmp_w2[l])
        y_conv, pooled = _conv_pool(p_small, bsz, seq, conv_w[l])
        mem_n = _rmsnorm(memf, norm_mem[l])
        kv = _matmul(mem_n, mem_wkv[l], tm=512, tn=512, name="mem_kv").reshape(bsz, mlen, 2 * BRANCH_W)
        y_mem = _mem_attn(p_small, kv, bsz, seq, mem_qk_gain[l])
        merged = _merge(h, w_gate, (y_rwkv, y_nsa, y_conv, y_mem), w_branch[l], pooled, pool_w[l], pool_scale[l])
        xf = _matmul(merged, w_out[l], tm=1024, tn=1024, res=xf, name="out_proj")

        if l % 2 == 0:
            h2 = _rmsnorm(xf, norm_ffn[l])
            e = l // 2
            act = _swiglu_up(h2, ffn_w1[e], ffn_w3[e])
            xf = _matmul(act, ffn_w2[e], tm=512, tn=512, res=xf, name="ffn_down")
        else:
            e = l // 2
            xf = _moe(xf, norm_ffn[l], moe_router[e], moe_w1[e], moe_w3[e], moe_w2[e])
    return xf.reshape(bsz, seq, d)
```
